```python
import math
import jax, jax.numpy as jnp
from jax import lax
import numpy as np

D_MODEL = 1024
BATCH = 32
SEQ = 256
DEPTH = 4
DEC_BATCH = 2
DEC_SEQ = 4096
PAST_LEN = 256

GRID_W = 64
N_MIXERS = 4
Q_BLOCK = 128
ROPE_THETA = 10000.0
NORM_EPS = 1e-6
A_HEADS = 16
A_KV_HEADS = 4
A_HEAD_DIM = 64
A_GROUP = A_HEADS // A_KV_HEADS
B_HEADS = 8
B_HEAD_DIM = 64
B_SUBLN_EPS = 1e-5
C_HEADS = 4
C_KEY_DIM = 256
C_VAL_DIM = 512
C_CHUNK = 128
D_BANDS = 16
D_EMB = 1 + 2 * D_BANDS
D_FILTER_HIDDEN = 64
D_ORDER = 2
D_DIRS = 2
D_FAST_DECAY_PCT = 0.3
D_SLOW_DECAY_PCT = 1.5
D_DECAY_TARGET = 1e-2
D_MOD_SHIFT = 0.05
FFN_DIM = 2816
N_A = (DEPTH + 3) // 4
N_B = (DEPTH + 2) // 4
N_C = (DEPTH + 1) // 4
N_D = DEPTH // 4

kernel_name = 'hybrid_diffusion_prefix_trunk_step'


def rms_norm(x, g, eps=NORM_EPS):
    xf = x.astype(jnp.float32)
    y = xf * lax.rsqrt(jnp.mean(xf * xf, axis=-1, keepdims=True) + eps)
    return (y * g.astype(jnp.float32)).astype(x.dtype)


def grid_positions(n_tok):
    rows = n_tok // GRID_W
    t = jnp.arange(rows * GRID_W)
    return (t // GRID_W).astype(jnp.float32), (t % GRID_W).astype(jnp.float32)


def rope_1d(x, pos):
    half = x.shape[-1] // 2
    inv_freq = ROPE_THETA ** (-jnp.arange(half, dtype=jnp.float32) / half)
    ang = pos[:, None] * inv_freq[None, :]
    cos = jnp.cos(ang)[None, :, None, :]
    sin = jnp.sin(ang)[None, :, None, :]
    x1, x2 = x[..., :half], x[..., half:]
    return jnp.concatenate([x1 * cos - x2 * sin, x1 * sin + x2 * cos], axis=-1)


def rope_2d(x, pos):
    row, col = pos
    xf = x.astype(jnp.float32)
    a = x.shape[-1] // 2
    return jnp.concatenate([rope_1d(xf[..., :a], row), rope_1d(xf[..., a:], col)], axis=-1).astype(x.dtype)


def over_query_blocks(fn, q):
    B, L = q.shape[:2]
    nb = L // Q_BLOCK
    qb = jnp.moveaxis(q.reshape((B, nb, Q_BLOCK) + q.shape[2:]), 1, 0)
    ob = lax.map(fn, qb)
    return jnp.moveaxis(ob, 0, 1).reshape((B, L) + ob.shape[3:])


def dwconv3(x, w, b):
    xp = jnp.pad(x, ((0, 0), (1, 1), (0, 0)))
    return xp[:, :-2] * w[0] + xp[:, 1:-1] * w[1] + xp[:, 2:] * w[2] + b


def attn_mixer(h, w_qkv, q_g, k_g, w_o, pos, ctx_kv):
    B, L, _ = h.shape
    qd, kd = A_HEADS * A_HEAD_DIM, A_KV_HEADS * A_HEAD_DIM
    qkv = h @ w_qkv
    q = rms_norm(qkv[..., :qd].reshape(B, L, A_HEADS, A_HEAD_DIM), q_g)
    k = rms_norm(qkv[..., qd:qd + kd].reshape(B, L, A_KV_HEADS, A_HEAD_DIM), k_g)
    v = qkv[..., qd + kd:].reshape(B, L, A_KV_HEADS, A_HEAD_DIM)
    if pos is not None:
        q = rope_2d(q, pos)
        k = rope_2d(k, pos)
    if ctx_kv is None:
        k_all, v_all = k, v
    else:
        k_all = jnp.concatenate([ctx_kv[0].astype(k.dtype), k], axis=1)
        v_all = jnp.concatenate([ctx_kv[1].astype(v.dtype), v], axis=1)
    scale = A_HEAD_DIM ** -0.5

    def block(qb):
        s = jnp.einsum('bqkgd,bskd->bkgqs', qb, k_all).astype(jnp.float32) * scale
        p = jax.nn.softmax(s, axis=-1).astype(v_all.dtype)
        return jnp.einsum('bkgqs,bskd->bqkgd', p, v_all)

    o = over_query_blocks(block, q.reshape(B, L, A_KV_HEADS, A_GROUP, A_HEAD_DIM))
    return o.reshape(B, L, qd) @ w_o, k, v


def diff_mixer(h, w_qkv, lam, subln_g, w_o, layer_idx, pos, ctx_kv):
    B, L, _ = h.shape
    d = B_HEAD_DIM
    w = B_HEADS * 2 * d
    qkv = h @ w_qkv
    q = qkv[..., :w].reshape(B, L, B_HEADS, 2 * d)
    k = qkv[..., w:2 * w].reshape(B, L, B_HEADS, 2 * d)
    v = qkv[..., 2 * w:].reshape(B, L, B_HEADS, 2 * d)
    if pos is not None:
        q = rope_2d(q.reshape(B, L, 2 * B_HEADS, d), pos).reshape(B, L, B_HEADS, 2 * d)
        k = rope_2d(k.reshape(B, L, 2 * B_HEADS, d), pos).reshape(B, L, B_HEADS, 2 * d)
    lam_init = 0.8 - 0.6 * math.exp(-0.3 * layer_idx)
    lf = lam.astype(jnp.float32)
    lam_full = jnp.exp(jnp.sum(lf[0] * lf[1])) - jnp.exp(jnp.sum(lf[2] * lf[3])) + lam_init
    if ctx_kv is None:
        k_all, v_all = k, v
    else:
        k_all = jnp.concatenate([ctx_kv[0].astype(k.dtype), k], axis=1)
        v_all = jnp.concatenate([ctx_kv[1].astype(v.dtype), v], axis=1)
    k1, k2 = k_all[..., :d], k_all[..., d:]
    scale = d ** -0.5

    def block(qb):
        s1 = jnp.einsum('bqhd,bshd->bhqs', qb[..., :d], k1).astype(jnp.float32) * scale
        s2 = jnp.einsum('bqhd,bshd->bhqs', qb[..., d:], k2).astype(jnp.float32) * scale
        a = jax.nn.softmax(s1, axis=-1) - lam_full * jax.nn.softmax(s2, axis=-1)
        return jnp.einsum('bhqs,bshe->bqhe', a.astype(v_all.dtype), v_all)

    o = over_query_blocks(block, q)
    o = rms_norm(o, subln_g, B_SUBLN_EPS) * (1.0 - lam_init)
    return o.reshape(B, L, w) @ w_o, k, v


def retention_scan(q, k, v, log_g, s0):
    B, L, H, _ = q.shape
    nc = L // C_CHUNK
    idx = jnp.arange(C_CHUNK, dtype=jnp.float32)
    rel = idx[:, None] - idx[None, :]
    causal = rel >= 0
    intra = jnp.where(causal[None], jnp.exp(jnp.where(causal, rel, 0.0)[None] * log_g[:, None, None]), 0.0)
    q_dec = jnp.exp((idx + 1.0)[:, None] * log_g[None, :])
    k_dec = jnp.exp((C_CHUNK - 1.0 - idx)[:, None] * log_g[None, :])
    chunk_dec = jnp.exp(C_CHUNK * log_g)

    def to_chunks(t):
        return jnp.moveaxis(t.reshape(B, nc, C_CHUNK, H, t.shape[-1]), 1, 0)

    def step(s, qkv_c):
        qc, kc, vc = qkv_c
        sc = jnp.einsum('bnhd,bmhd->bhnm', qc, kc) * intra[None]
        o = jnp.einsum('bhnm,bmhe->bnhe', sc, vc) + jnp.einsum('bnhd,bhde->bnhe', qc, s) * q_dec[None, :, :, None]
        s = s * chunk_dec[None, :, None, None] + jnp.einsum('bmhd,bmhe->bhde', kc * k_dec[None, :, :, None], vc)
        return s, o

    s_fin, o = lax.scan(step, s0, (to_chunks(q), to_chunks(k), to_chunks(v)))
    return jnp.moveaxis(o, 0, 1).reshape(B, L, H, v.shape[-1]), s_fin


def ret_mixer(h, w_in, log_decay, gn_g, w_o, s0):
    B, L, _ = h.shape
    qk, vd = C_HEADS * C_KEY_DIM, C_HEADS * C_VAL_DIM
    proj = h @ w_in
    q = proj[..., :qk].reshape(B, L, C_HEADS, C_KEY_DIM).astype(jnp.float32)
    k = proj[..., qk:2 * qk].reshape(B, L, C_HEADS, C_KEY_DIM).astype(jnp.float32) * (C_KEY_DIM ** -0.5)
    v = proj[..., 2 * qk:2 * qk + vd].reshape(B, L, C_HEADS, C_VAL_DIM).astype(jnp.float32)
    g = proj[..., 2 * qk + vd:]
    log_g = -jnp.abs(log_decay.astype(jnp.float32))
    if s0 is None:
        s0 = jnp.zeros((B, 2, C_HEADS, C_KEY_DIM, C_VAL_DIM), jnp.float32)
    else:
        s0 = s0.astype(jnp.float32)
    o_f, s_f = retention_scan(q, k, v, log_g[0], s0[:, 0])
    o_b, s_b = retention_scan(q[:, ::-1], k[:, ::-1], v[:, ::-1], log_g[1], s0[:, 1])
    o = rms_norm(o_f + o_b[:, ::-1], gn_g.reshape(C_HEADS, C_VAL_DIM)).astype(h.dtype)
    out = (jax.nn.silu(g) * o.reshape(B, L, vd)) @ w_o
    return out, jnp.stack([s_f, s_b], axis=1).astype(h.dtype)


def hyena_filters(L, w1, b1, w2, b2, w3, freq):
    t = jnp.arange(L, dtype=jnp.float32)
    t_norm = t / max(L - 1, 1)
    bands = jnp.linspace(1e-4, D_BANDS - 1, D_BANDS, dtype=jnp.float32)
    ang = 2.0 * math.pi * t[:, None] * bands[None, :] / L
    z = jnp.concatenate([t_norm[:, None], jnp.cos(ang), -jnp.sin(ang)], axis=-1)
    a = jnp.sin(freq[0] * (z @ w1 + b1))
    a = jnp.sin(freq[1] * (a @ w2 + b2))
    filt = (a @ w3).astype(jnp.float32).reshape(L, D_DIRS, D_ORDER, D_MODEL)
    max_decay = math.log(D_DECAY_TARGET) / D_FAST_DECAY_PCT
    min_decay = math.log(D_DECAY_TARGET) / D_SLOW_DECAY_PCT
    deltas = jnp.abs(jnp.linspace(min_decay, max_decay, D_MODEL, dtype=jnp.float32))
    window = jnp.exp(-t_norm[:, None] * deltas[None, :]) + D_MOD_SHIFT
    filt = filt * window[:, None, None, :]
    return filt / (jnp.sum(jnp.abs(filt), axis=0, keepdims=True) + 1e-6)


def fft_conv(u, h):
    L = u.shape[1]
    uf = jnp.fft.rfft(u, n=2 * L, axis=1)
    hf = jnp.fft.rfft(h, n=2 * L, axis=0)
    return jnp.fft.irfft(uf * hf[None], n=2 * L, axis=1)[:, :L]


def hyena_mixer(h, w_in, sc_w, sc_b, f_w1, f_b1, f_w2, f_b2, f_w3, f_freq, f_skip, w_o):
    B, L, _ = h.shape
    z = dwconv3(h @ w_in, sc_w, sc_b).astype(jnp.float32)
    x1, x2, v = jnp.split(z, 3, axis=-1)
    filt = hyena_filters(L, f_w1, f_b1, f_w2, f_b2, f_w3, f_freq)
    skip = f_skip.astype(jnp.float32)
    y = v
    for n, gate in enumerate((x1, x2)):
        fwd = fft_conv(y, filt[:, 0, n])
        bwd = fft_conv(y[:, ::-1], filt[:, 1, n])[:, ::-1]
        y = gate * (fwd + bwd + skip[n] * y)
    return y.astype(h.dtype) @ w_o


def conv_ffn(h, w_up, cw, cb, w_down):
    a, b = jnp.split(dwconv3(h @ w_up, cw, cb), 2, axis=-1)
    return (jax.nn.silu(a) * b) @ w_down


def trunk(x, cond, latent, cache, W):
    L = x.shape[1]
    pos = grid_positions(L) if latent else None
    new = {'attn_k': [], 'attn_v': [], 'diff_k': [], 'diff_v': [], 'ret_s': []}
    for l in range(DEPTH):
        m, j = l % N_MIXERS, l // N_MIXERS
        mod = (jax.nn.silu(cond) @ W['w_mod'][l] + W['b_mod'][l])[:, None, :]
        sh1, sc1, g1, sh2, sc2, g2 = jnp.split(mod, 6, axis=-1)
        h = rms_norm(x, W['norm1_g'][l]) * (1.0 + sc1) + sh1
        if m == 0:
            ctx = (cache['attn_k'][:, j], cache['attn_v'][:, j]) if latent else None
            o, k, v = attn_mixer(h, W['attn_w_qkv'][j], W['attn_q_g'][j], W['attn_k_g'][j], W['attn_w_o'][j], pos, ctx)
            if not latent:
                new['attn_k'].append(k)
                new['attn_v'].append(v)
        elif m == 1:
            ctx = (cache['diff_k'][:, j], cache['diff_v'][:, j]) if latent else None
            o, k, v = diff_mixer(h, W['diff_w_qkv'][j], W['diff_lambda'][j], W['diff_subln_g'][j], W['diff_w_o'][j], l, pos, ctx)
            if not latent:
                new['diff_k'].append(k)
                new['diff_v'].append(v)
        elif m == 2:
            s0 = cache['ret_s'][:, j] if latent else None
            o, s = ret_mixer(h, W['ret_w_in'][j], W['ret_log_decay'][j], W['ret_gn_g'][j], W['ret_w_o'][j], s0)
            if not latent:
                new['ret_s'].append(s)
        else:
            o = hyena_mixer(h, W['hyena_w_in'][j], W['hyena_sc_w'][j], W['hyena_sc_b'][j], W['hyena_f_w1'][j], W['hyena_f_b1'][j], W['hyena_f_w2'][j], W['hyena_f_b2'][j], W['hyena_f_w3'][j], W['hyena_f_freq'][j], W['hyena_f_skip'][j], W['hyena_w_o'][j])
        x = x + g1 * o
        h = rms_norm(x, W['norm2_g'][l]) * (1.0 + sc2) + sh2
        x = x + g2 * conv_ffn(h, W['ffn_w_up'][l], W['ffn_conv_w'][l], W['ffn_conv_b'][l], W['ffn_w_down'][l])
    return rms_norm(x, W['final_g']), new


def setup_inputs(seed: int = 0) -> dict:
    key = jax.random.key(seed)
    ks = iter(jax.random.split(key, 64))
    f32 = jnp.float32
    D = D_MODEL

    def nrm(shape, scale):
        return jax.random.normal(next(ks), shape, f32) * scale

    def gain(shape):
        return 1.0 + nrm(shape, 0.02)

    base_decay = jnp.log(1.0 - 2.0 ** (-5.0 - jnp.arange(C_HEADS, dtype=f32)))
    inp = {}
    inp['x_prompt'] = nrm((BATCH, SEQ, D), 1.0)
    inp['x_sample'] = nrm((DEC_BATCH, DEC_SEQ, D), 1.0)
    inp['cache_attn_k'] = nrm((DEC_BATCH, N_A, PAST_LEN, A_KV_HEADS, A_HEAD_DIM), 1.0)
    inp['cache_attn_v'] = nrm((DEC_BATCH, N_A, PAST_LEN, A_KV_HEADS, A_HEAD_DIM), 1.0)
    inp['cache_diff_k'] = nrm((DEC_BATCH, N_B, PAST_LEN, B_HEADS, 2 * B_HEAD_DIM), 1.0)
    inp['cache_diff_v'] = nrm((DEC_BATCH, N_B, PAST_LEN, B_HEADS, 2 * B_HEAD_DIM), 1.0)
    inp['state_ret'] = nrm((DEC_BATCH, N_C, 2, C_HEADS, C_KEY_DIM, C_VAL_DIM), 1.0)
    inp['c'] = nrm((DEC_BATCH, D), 1.0)
    inp['c_ctx'] = nrm((D,), 1.0)
    inp['w_mod'] = nrm((DEPTH, D, 6 * D), D ** -0.5)
    inp['b_mod'] = nrm((DEPTH, 6 * D), 0.01)
    inp['norm1_g'] = gain((DEPTH, D))
    inp['norm2_g'] = gain((DEPTH, D))
    inp['final_g'] = gain((D,))
    inp['attn_w_qkv'] = nrm((N_A, D, (A_HEADS + 2 * A_KV_HEADS) * A_HEAD_DIM), D ** -0.5)
    inp['attn_q_g'] = gain((N_A, A_HEAD_DIM))
    inp['attn_k_g'] = gain((N_A, A_HEAD_DIM))
    inp['attn_w_o'] = nrm((N_A, A_HEADS * A_HEAD_DIM, D), (A_HEADS * A_HEAD_DIM) ** -0.5)
    inp['diff_w_qkv'] = nrm((N_B, D, 3 * B_HEADS * 2 * B_HEAD_DIM), D ** -0.5)
    inp['diff_lambda'] = nrm((N_B, 4, B_HEAD_DIM), 0.1)
    inp['diff_subln_g'] = gain((N_B, 2 * B_HEAD_DIM))
    inp['diff_w_o'] = nrm((N_B, B_HEADS * 2 * B_HEAD_DIM, D), (B_HEADS * 2 * B_HEAD_DIM) ** -0.5)
    inp['ret_w_in'] = nrm((N_C, D, 2 * C_HEADS * C_KEY_DIM + 2 * C_HEADS * C_VAL_DIM), D ** -0.5)
    inp['ret_log_decay'] = base_decay[None, None, :] * jnp.exp(nrm((N_C, 2, C_HEADS), 0.1))
    inp['ret_gn_g'] = gain((N_C, C_HEADS * C_VAL_DIM))
    inp['ret_w_o'] = nrm((N_C, C_HEADS * C_VAL_DIM, D), (C_HEADS * C_VAL_DIM) ** -0.5)
    inp['hyena_w_in'] = nrm((N_D, D, 3 * D), D ** -0.5)
    inp['hyena_sc_w'] = nrm((N_D, 3, 3 * D), 3 ** -0.5)
    inp['hyena_sc_b'] = nrm((N_D, 3 * D), 0.01)
    inp['hyena_f_w1'] = nrm((N_D, D_EMB, D_FILTER_HIDDEN), D_EMB ** -0.5)
    inp['hyena_f_b1'] = nrm((N_D, D_FILTER_HIDDEN), 0.01)
    inp['hyena_f_w2'] = nrm((N_D, D_FILTER_HIDDEN, D_FILTER_HIDDEN), D_FILTER_HIDDEN ** -0.5)
    inp['hyena_f_b2'] = nrm((N_D, D_FILTER_HIDDEN), 0.01)
    inp['hyena_f_w3'] = nrm((N_D, D_FILTER_HIDDEN, D_DIRS * D_ORDER * D), D_FILTER_HIDDEN ** -0.5)
    inp['hyena_f_freq'] = gain((N_D, 2, D_FILTER_HIDDEN))
    inp['hyena_f_skip'] = nrm((N_D, D_ORDER, D), 0.5)
    inp['hyena_w_o'] = nrm((N_D, D, D), D ** -0.5)
    inp['ffn_w_up'] = nrm((DEPTH, D, 2 * FFN_DIM), D ** -0.5)
    inp['ffn_conv_w'] = nrm((DEPTH, 3, 2 * FFN_DIM), 3 ** -0.5)
    inp['ffn_conv_b'] = nrm((DEPTH, 2 * FFN_DIM), 0.01)
    inp['ffn_w_down'] = nrm((DEPTH, FFN_DIM, D), FFN_DIM ** -0.5)
    return inp


def reference(x_prompt, x_sample, cache_attn_k, cache_attn_v, cache_diff_k, cache_diff_v, state_ret, c, c_ctx, w_mod, b_mod, norm1_g, norm2_g, final_g, attn_w_qkv, attn_q_g, attn_k_g, attn_w_o, diff_w_qkv, diff_lambda, diff_subln_g, diff_w_o, ret_w_in, ret_log_decay, ret_gn_g, ret_w_o, hyena_w_in, hyena_sc_w, hyena_sc_b, hyena_f_w1, hyena_f_b1, hyena_f_w2, hyena_f_b2, hyena_f_w3, hyena_f_freq, hyena_f_skip, hyena_w_o, ffn_w_up, ffn_conv_w, ffn_conv_b, ffn_w_down):
    W = {
        'w_mod': w_mod, 'b_mod': b_mod, 'norm1_g': norm1_g, 'norm2_g': norm2_g, 'final_g': final_g,
        'attn_w_qkv': attn_w_qkv, 'attn_q_g': attn_q_g, 'attn_k_g': attn_k_g, 'attn_w_o': attn_w_o,
        'diff_w_qkv': diff_w_qkv, 'diff_lambda': diff_lambda, 'diff_subln_g': diff_subln_g, 'diff_w_o': diff_w_o,
        'ret_w_in': ret_w_in, 'ret_log_decay': ret_log_decay, 'ret_gn_g': ret_gn_g, 'ret_w_o': ret_w_o,
        'hyena_w_in': hyena_w_in, 'hyena_sc_w': hyena_sc_w, 'hyena_sc_b': hyena_sc_b,
        'hyena_f_w1': hyena_f_w1, 'hyena_f_b1': hyena_f_b1, 'hyena_f_w2': hyena_f_w2, 'hyena_f_b2': hyena_f_b2,
        'hyena_f_w3': hyena_f_w3, 'hyena_f_freq': hyena_f_freq, 'hyena_f_skip': hyena_f_skip, 'hyena_w_o': hyena_w_o,
        'ffn_w_up': ffn_w_up, 'ffn_conv_w': ffn_conv_w, 'ffn_conv_b': ffn_conv_b, 'ffn_w_down': ffn_w_down,
    }
    y_prompt, st = trunk(x_prompt, c_ctx[None, :], False, None, W)
    cache = {'attn_k': cache_attn_k, 'attn_v': cache_attn_v, 'diff_k': cache_diff_k, 'diff_v': cache_diff_v, 'ret_s': state_ret}
    y_sample, _ = trunk(x_sample, c, True, cache, W)
    new_attn_k = jnp.stack(st['attn_k'], axis=1)
    new_attn_v = jnp.stack(st['attn_v'], axis=1)
    new_diff_k = jnp.stack(st['diff_k'], axis=1)
    new_diff_v = jnp.stack(st['diff_v'], axis=1)
    new_state_ret = jnp.stack(st['ret_s'], axis=1)
    return (y_prompt, y_sample, new_attn_k, new_attn_v, new_diff_k, new_diff_v, new_state_ret)
```

```python
import functools
import math

import jax
import jax.numpy as jnp
import numpy as np
from jax import lax
from jax.experimental import pallas as pl
from jax.experimental.pallas import tpu as pltpu

F32 = jnp.float32
BF16 = jnp.bfloat16

D_MODEL = 1024
BATCH = 32
SEQ = 256
DEPTH = 4
DEC_BATCH = 2
DEC_SEQ = 4096
PAST_LEN = 256
GRID_W = 64
ROPE_THETA = 10000.0
NORM_EPS = 1e-6
A_HEADS = 16
A_KV_HEADS = 4
A_HEAD_DIM = 64
A_GROUP = A_HEADS // A_KV_HEADS
B_HEADS = 8
B_HEAD_DIM = 64
B_SUBLN_EPS = 1e-5
C_HEADS = 4
C_KEY_DIM = 256
C_VAL_DIM = 512
D_BANDS = 16
D_EMB = 1 + 2 * D_BANDS
D_FILTER_HIDDEN = 64
D_FAST_DECAY_PCT = 0.3
D_SLOW_DECAY_PCT = 1.5
D_DECAY_TARGET = 1e-2
D_MOD_SHIFT = 0.05
FFN_DIM = 2816

T_CTX = BATCH * SEQ
T_LAT = DEC_BATCH * DEC_SEQ
T_ALL = T_CTX + T_LAT
N_SEG = 1 + DEC_BATCH
SEG_ROWS = 8

HALO = 16
RET_CHUNK = 256
VMEM_LIMIT = 56 * 1024 * 1024

LAT_N = 2 * DEC_SEQ
LAT_N1 = 64
LAT_N2 = LAT_N // LAT_N1
CTX_N = 2 * SEQ


def _cparams(*sem):
    return pltpu.CompilerParams(dimension_semantics=sem, vmem_limit_bytes=VMEM_LIMIT)


def _seg_of_tile(i, tm):
    start = i * tm
    return jnp.where(start < T_CTX, 0, 1 + (start - T_CTX) // DEC_SEQ)


def _silu(x):
    return x * jax.nn.sigmoid(x)


def _bdot(a, b):
    return jnp.dot(a, b, preferred_element_type=F32)


def _norm_mod(x, g, sh, sc):
    ms = jnp.mean(x * x, axis=-1, keepdims=True)
    y = (x * lax.rsqrt(ms + NORM_EPS)) * g
    return y * (1.0 + sc) + sh


def _mod_kernel(c_ref, w_ref, b_ref, o_ref):
    s = _silu(c_ref[...]).astype(BF16)
    o_ref[...] = _bdot(s, w_ref[...].astype(BF16)) + b_ref[...]


def modulation(cond, w_mod, b_mod):
    tn = 1536
    n = w_mod.shape[-1]
    return pl.pallas_call(
        _mod_kernel,
        grid=(DEPTH, n // tn),
        in_specs=[
            pl.BlockSpec((SEG_ROWS, D_MODEL), lambda l, j: (0, 0)),
            pl.BlockSpec((None, D_MODEL, tn), lambda l, j: (l, 0, j)),
            pl.BlockSpec((None, 1, tn), lambda l, j: (l, 0, j)),
        ],
        out_specs=pl.BlockSpec((None, SEG_ROWS, tn), lambda l, j: (l, 0, j)),
        out_shape=jax.ShapeDtypeStruct((DEPTH, SEG_ROWS, n), F32),
        compiler_params=_cparams("arbitrary", "arbitrary"),
        name="modulation",
    )(cond, w_mod, b_mod.reshape(DEPTH, 1, n))


def _proj_kernel(x_ref, g_ref, sh_ref, sc_ref, w_ref, o_ref, h_ref):
    @pl.when(pl.program_id(1) == 0)
    def _():
        h_ref[...] = _norm_mod(x_ref[...], g_ref[...], sh_ref[...], sc_ref[...]).astype(BF16)

    o_ref[...] = _bdot(h_ref[...], w_ref[...]).astype(o_ref.dtype)


def norm_proj(x, g, sh, sc, w, *, tm=1024, tn=512, out_dtype=F32):
    t, d = x.shape
    n = w.shape[1]
    seg = lambda i, j: (_seg_of_tile(i, tm), 0, 0)
    return pl.pallas_call(
        _proj_kernel,
        grid=(t // tm, n // tn),
        in_specs=[
            pl.BlockSpec((tm, d), lambda i, j: (i, 0)),
            pl.BlockSpec((1, d), lambda i, j: (0, 0)),
            pl.BlockSpec((None, 1, d), seg),
            pl.BlockSpec((None, 1, d), seg),
            pl.BlockSpec((d, tn), lambda i, j: (0, j)),
        ],
        out_specs=pl.BlockSpec((tm, tn), lambda i, j: (i, j)),
        out_shape=jax.ShapeDtypeStruct((t, n), out_dtype),
        scratch_shapes=[pltpu.VMEM((tm, d), BF16)],
        compiler_params=_cparams("parallel", "arbitrary"),
        name="norm_proj",
    )(x, g.reshape(1, d), sh, sc, w)


def _conv3(u, cw, cb, first, last, tm):
    rows = u.shape[0]
    up = pltpu.roll(u, 1, 0)[HALO:HALO + tm]
    uc = u[HALO:HALO + tm]
    un = pltpu.roll(u, rows - 1, 0)[HALO:HALO + tm]
    up = jnp.where(first, 0.0, up)
    un = jnp.where(last, 0.0, un)
    return up * cw[0:1] + uc * cw[1:2] + un * cw[2:3] + cb


def _seq_edges(i, tm):
    row = i * tm + lax.broadcasted_iota(jnp.int32, (tm, 1), 0)
    seq = jnp.where(i * tm < T_CTX, SEQ, DEC_SEQ)
    pos = row & (seq - 1)
    return pos == 0, pos == seq - 1


def _fill_h(h_ref, xp_ref, x_ref, xn_ref, g_ref, sh_ref, sc_ref, tm):
    g, sh, sc = g_ref[...], sh_ref[...], sc_ref[...]
    h_ref[0:HALO] = _norm_mod(xp_ref[...], g, sh, sc).astype(BF16)
    h_ref[HALO:HALO + tm] = _norm_mod(x_ref[...], g, sh, sc).astype(BF16)
    h_ref[HALO + tm:] = _norm_mod(xn_ref[...], g, sh, sc).astype(BF16)


def _proj_conv_kernel(xp_ref, x_ref, xn_ref, g_ref, sh_ref, sc_ref, w_ref, cw_ref, cb_ref, o_ref, h_ref, *, tm):
    i = pl.program_id(0)

    @pl.when(pl.program_id(1) == 0)
    def _():
        _fill_h(h_ref, xp_ref, x_ref, xn_ref, g_ref, sh_ref, sc_ref, tm)

    first, last = _seq_edges(i, tm)
    u = _bdot(h_ref[...], w_ref[...])
    o_ref[...] = _conv3(u, cw_ref[...], cb_ref[...], first, last, tm)


def _proj_glu_kernel(xp_ref, x_ref, xn_ref, g_ref, sh_ref, sc_ref, wa_ref, wb_ref, cwa_ref, cwb_ref, cba_ref, cbb_ref,
                     o_ref, h_ref, *, tm):
    i = pl.program_id(0)

    @pl.when(pl.program_id(1) == 0)
    def _():
        _fill_h(h_ref, xp_ref, x_ref, xn_ref, g_ref, sh_ref, sc_ref, tm)

    first, last = _seq_edges(i, tm)
    h = h_ref[...]
    a = _conv3(_bdot(h, wa_ref[...]), cwa_ref[...], cba_ref[...], first, last, tm)
    b = _conv3(_bdot(h, wb_ref[...]), cwb_ref[...], cbb_ref[...], first, last, tm)
    o_ref[...] = (_silu(a) * b).astype(o_ref.dtype)


def _halo_specs(t, tm, d):
    per = tm // HALO
    last_blk = t // HALO - 1
    return [
        pl.BlockSpec((HALO, d), lambda i, j: (jnp.maximum(i * per - 1, 0), 0)),
        pl.BlockSpec((tm, d), lambda i, j: (i, 0)),
        pl.BlockSpec((HALO, d), lambda i, j: (jnp.minimum((i + 1) * per, last_blk), 0)),
    ]


def norm_proj_conv(x, g, sh, sc, w, cw, cb, *, tm=1024, tn=512):
    t, d = x.shape
    n = w.shape[1]
    per_group = d // tn
    seg = lambda i, j: (_seg_of_tile(i, tm), 0, 0)
    return pl.pallas_call(
        functools.partial(_proj_conv_kernel, tm=tm),
        grid=(t // tm, n // tn),
        in_specs=_halo_specs(t, tm, d) + [
            pl.BlockSpec((1, d), lambda i, j: (0, 0)),
            pl.BlockSpec((None, 1, d), seg),
            pl.BlockSpec((None, 1, d), seg),
            pl.BlockSpec((d, tn), lambda i, j: (0, j)),
            pl.BlockSpec((3, tn), lambda i, j: (0, j)),
            pl.BlockSpec((1, tn), lambda i, j: (0, j)),
        ],
        out_specs=pl.BlockSpec((None, tm, tn), lambda i, j: (j // per_group, i, j % per_group)),
        out_shape=jax.ShapeDtypeStruct((n // d, t, d), F32),
        scratch_shapes=[pltpu.VMEM((tm + 2 * HALO, d), BF16)],
        compiler_params=_cparams("parallel", "arbitrary"),
        name="norm_proj_conv",
    )(x, x, x, g.reshape(1, d), sh, sc, w, cw, cb.reshape(1, n))


def norm_proj_glu(x, g, sh, sc, w, cw, cb, *, tm=1024, tn=256):
    t, d = x.shape
    f = w.shape[1] // 2
    nb = f // tn
    cb = cb.reshape(1, 2 * f)
    seg = lambda i, j: (_seg_of_tile(i, tm), 0, 0)
    return pl.pallas_call(
        functools.partial(_proj_glu_kernel, tm=tm),
        grid=(t // tm, nb),
        in_specs=_halo_specs(t, tm, d) + [
            pl.BlockSpec((1, d), lambda i, j: (0, 0)),
            pl.BlockSpec((None, 1, d), seg),
            pl.BlockSpec((None, 1, d), seg),
            pl.BlockSpec((d, tn), lambda i, j: (0, j)),
            pl.BlockSpec((d, tn), lambda i, j: (0, j + nb)),
            pl.BlockSpec((3, tn), lambda i, j: (0, j)),
            pl.BlockSpec((3, tn), lambda i, j: (0, j + nb)),
            pl.BlockSpec((1, tn), lambda i, j: (0, j)),
            pl.BlockSpec((1, tn), lambda i, j: (0, j + nb)),
        ],
        out_specs=pl.BlockSpec((tm, tn), lambda i, j: (i, j)),
        out_shape=jax.ShapeDtypeStruct((t, f), BF16),
        scratch_shapes=[pltpu.VMEM((tm + 2 * HALO, d), BF16)],
        compiler_params=_cparams("parallel", "arbitrary"),
        name="norm_proj_glu",
    )(x, x, x, g.reshape(1, d), sh, sc, w, w, cw, cw, cb, cb)


def _group_specs(a_ctx, a_lat, tm, width):
    nctx = T_CTX // tm
    off = nctx if a_lat.shape[0] == T_ALL else 0
    return [
        pl.BlockSpec((tm, width), lambda i: (jnp.minimum(i, nctx - 1), 0)),
        pl.BlockSpec((tm, width), lambda i: (off + jnp.maximum(i - nctx, 0), 0)),
    ]


def _pick_group(ac_ref, al_ref, tm):
    return jnp.where(pl.program_id(0) < T_CTX // tm, ac_ref[...], al_ref[...])


def _out_proj_kernel(ac_ref, al_ref, w_ref, x_ref, gate_ref, *rest, final, tm):
    o_ref = rest[-1]
    a = _pick_group(ac_ref, al_ref, tm).astype(BF16)
    y = x_ref[...] + gate_ref[...] * _bdot(a, w_ref[...])
    if final:
        fg_ref = rest[0]
        ms = jnp.mean(y * y, axis=-1, keepdims=True)
        y = (y * lax.rsqrt(ms + NORM_EPS)) * fg_ref[...]
    o_ref[...] = y


def out_proj_residual(a_ctx, a_lat, w, x, gate, final_g=None, *, tm=512):
    t, d = x.shape
    k = w.shape[0]
    seg = lambda i: (_seg_of_tile(i, tm), 0, 0)
    in_specs = _group_specs(a_ctx, a_lat, tm, k) + [
        pl.BlockSpec((k, d), lambda i: (0, 0)),
        pl.BlockSpec((tm, d), lambda i: (i, 0)),
        pl.BlockSpec((None, 1, d), seg),
    ]
    args = [a_ctx, a_lat, w, x, gate]
    if final_g is not None:
        in_specs.append(pl.BlockSpec((1, d), lambda i: (0, 0)))
        args.append(final_g.reshape(1, d))
    return pl.pallas_call(
        functools.partial(_out_proj_kernel, final=final_g is not None, tm=tm),
        grid=(t // tm,),
        in_specs=in_specs,
        out_specs=pl.BlockSpec((tm, d), lambda i: (i, 0)),
        out_shape=jax.ShapeDtypeStruct((t, d), F32),
        compiler_params=_cparams("parallel"),
        name="out_proj_residual",
    )(*args)


def _head_mean_sq(x, head_dim):
    n = x.shape[1]
    x2 = x * x
    hi = x2.astype(BF16)
    lo = (x2 - hi.astype(F32)).astype(BF16)
    blk = 256
    r = lax.broadcasted_iota(jnp.int32, (blk, blk), 0) // head_dim
    c = lax.broadcasted_iota(jnp.int32, (blk, blk), 1) // head_dim
    ones = (r == c).astype(BF16)
    parts = []
    for s in range(0, n, blk):
        parts.append(_bdot(hi[:, s:s + blk], ones) + _bdot(lo[:, s:s + blk], ones))
    ss = parts[0] if len(parts) == 1 else jnp.concatenate(parts, axis=1)
    return ss * (1.0 / head_dim)


def _rope(x, cos, sin):
    n = x.shape[1]
    lane = lax.broadcasted_iota(jnp.int32, (1, 128), 1)
    lower = (lane & 31) < 16
    outs = []
    for s in range(0, n, 128):
        xs = x[:, s:s + 128]
        partner = jnp.where(lower, pltpu.roll(xs, 128 - 16, 1), pltpu.roll(xs, 16, 1))
        outs.append(xs * cos + partner * sin)
    return jnp.concatenate(outs, axis=1)


def _qk_prep_kernel(*refs, nq, nk, norm, rope, scale):
    it = iter(refs)
    q_ref, k_ref, v_ref = next(it), next(it), next(it)
    if norm:
        qg_ref, kg_ref = next(it), next(it)
    if rope:
        cos_ref, sin_ref = next(it), next(it)
    qo_ref, ko_ref, kvo_ref = next(it), next(it), next(it)
    q = q_ref[...]
    k = k_ref[...]
    if norm:
        q = (q * lax.rsqrt(_head_mean_sq(q, A_HEAD_DIM) + NORM_EPS)) * qg_ref[...]
        k = (k * lax.rsqrt(_head_mean_sq(k, A_HEAD_DIM) + NORM_EPS)) * kg_ref[...]
    if rope:
        cos, sin = cos_ref[...], sin_ref[...]
        q = _rope(q, cos, sin)
        k = _rope(k, cos, sin)
    qo_ref[...] = (q * scale).astype(BF16)
    ko_ref[...] = k
    kvo_ref[:, 0:nk] = k.astype(BF16)
    kvo_ref[:, nk:] = v_ref[...].astype(BF16)


def qk_prep(qkv, *, row0, rows, nq, nk, q_g=None, k_g=None, rope=None, scale, tm=512):
    norm = q_g is not None
    r0 = row0 // tm
    qb = nq // nk
    in_specs = [
        pl.BlockSpec((tm, nq), lambda i: (i + r0, 0)),
        pl.BlockSpec((tm, nk), lambda i: (i + r0, qb)),
        pl.BlockSpec((tm, nk), lambda i: (i + r0, qb + 1)),
    ]
    args = [qkv, qkv, qkv]
    if norm:
        in_specs += [pl.BlockSpec((1, nq), lambda i: (0, 0)), pl.BlockSpec((1, nk), lambda i: (0, 0))]
        args += [jnp.tile(q_g, nq // q_g.shape[0]).reshape(1, nq), jnp.tile(k_g, nk // k_g.shape[0]).reshape(1, nk)]
    if rope is not None:
        per = DEC_SEQ // tm
        in_specs += [pl.BlockSpec((tm, 128), lambda i: (i % per, 0))] * 2
        args += list(rope)
    return pl.pallas_call(
        functools.partial(_qk_prep_kernel, nq=nq, nk=nk, norm=norm, rope=rope is not None, scale=scale),
        grid=(rows // tm,),
        in_specs=in_specs,
        out_specs=[
            pl.BlockSpec((tm, nq), lambda i: (i, 0)),
            pl.BlockSpec((tm, nk), lambda i: (i, 0)),
            pl.BlockSpec((tm, 2 * nk), lambda i: (i, 0)),
        ],
        out_shape=[
            jax.ShapeDtypeStruct((rows, nq), BF16),
            jax.ShapeDtypeStruct((rows, nk), F32),
            jax.ShapeDtypeStruct((rows, 2 * nk), BF16),
        ],
        compiler_params=_cparams("parallel"),
        name="qk_prep",
    )(*args)


def rope_tables():
    t = jnp.arange(DEC_SEQ)
    row = (t // GRID_W).astype(F32)
    col = (t % GRID_W).astype(F32)
    half = A_HEAD_DIM // 4
    inv_freq = ROPE_THETA ** (-jnp.arange(half, dtype=F32) / half)
    ar = row[:, None] * inv_freq[None, :]
    ac = col[:, None] * inv_freq[None, :]
    cos = jnp.concatenate([jnp.cos(ar), jnp.cos(ar), jnp.cos(ac), jnp.cos(ac)], axis=1)
    sin = jnp.concatenate([-jnp.sin(ar), jnp.sin(ar), -jnp.sin(ac), jnp.sin(ac)], axis=1)
    return jnp.tile(cos, (1, 2)), jnp.tile(sin, (1, 2))


def _gqa_kernel(q_ref, kv_ref, o_ref, *, tq):
    d = A_HEAD_DIM
    kvw = A_KV_HEADS * d
    for kh in range(A_KV_HEADS):
        k_h = kv_ref[:, kh * d:(kh + 1) * d]
        v_h = kv_ref[:, kvw + kh * d:kvw + (kh + 1) * d]
        base = kh * A_GROUP * d
        qs = jnp.concatenate([q_ref[:, base + g * d:base + (g + 1) * d] for g in range(A_GROUP)], axis=0)
        s = lax.dot_general(qs, k_h, (((1,), (1,)), ((), ())), preferred_element_type=F32)
        m = jnp.max(s, axis=-1, keepdims=True)
        p = jnp.exp(s - m)
        l = jnp.sum(p, axis=-1, keepdims=True)
        o = _bdot(p.astype(BF16), v_h) / l
        for g in range(A_GROUP):
            o_ref[:, base + g * d:base + (g + 1) * d] = o[g * tq:(g + 1) * tq].astype(o_ref.dtype)


def gqa_attention(q, kv, *, nb, lq, lk, tq):
    nq = lq // tq
    return pl.pallas_call(
        functools.partial(_gqa_kernel, tq=tq),
        grid=(nb, nq),
        in_specs=[
            pl.BlockSpec((tq, q.shape[1]), lambda b, i: (b * nq + i, 0)),
            pl.BlockSpec((lk, kv.shape[1]), lambda b, i: (b, 0)),
        ],
        out_specs=pl.BlockSpec((tq, q.shape[1]), lambda b, i: (b * nq + i, 0)),
        out_shape=jax.ShapeDtypeStruct(q.shape, BF16),
        compiler_params=_cparams("parallel", "arbitrary"),
        name="gqa_attention",
    )(q, kv)


def _diff_kernel(q_ref, kv_ref, lam_ref, sg_ref, o_ref, *, lam_init):
    d = B_HEAD_DIM
    kw = B_HEADS * 2 * d
    lf = lam_ref[...]
    lam = (jnp.exp(jnp.sum(lf[0:1] * lf[1:2], axis=-1, keepdims=True))
           - jnp.exp(jnp.sum(lf[2:3] * lf[3:4], axis=-1, keepdims=True)) + lam_init)
    dn = (((1,), (1,)), ((), ()))
    for h in range(B_HEADS):
        c0 = h * 2 * d
        s1 = lax.dot_general(q_ref[:, c0:c0 + d], kv_ref[:, c0:c0 + d], dn, preferred_element_type=F32)
        s2 = lax.dot_general(q_ref[:, c0 + d:c0 + 2 * d], kv_ref[:, c0 + d:c0 + 2 * d], dn, preferred_element_type=F32)
        p1 = jnp.exp(s1 - jnp.max(s1, axis=-1, keepdims=True))
        p2 = jnp.exp(s2 - jnp.max(s2, axis=-1, keepdims=True))
        r1 = 1.0 / jnp.sum(p1, axis=-1, keepdims=True)
        r2 = lam / jnp.sum(p2, axis=-1, keepdims=True)
        a = (p1 * r1 - p2 * r2).astype(BF16)
        o = _bdot(a, kv_ref[:, kw + c0:kw + c0 + 2 * d])
        ms = jnp.mean(o * o, axis=-1, keepdims=True)
        o = (o * lax.rsqrt(ms + B_SUBLN_EPS)) * sg_ref[...]
        o_ref[:, c0:c0 + 2 * d] = (o * (1.0 - lam_init)).astype(o_ref.dtype)


def diff_attention(q, kv, lam, subln_g, *, nb, lq, lk, tq, lam_init):
    nq = lq // tq
    return pl.pallas_call(
        functools.partial(_diff_kernel, lam_init=lam_init),
        grid=(nb, nq),
        in_specs=[
            pl.BlockSpec((tq, q.shape[1]), lambda b, i: (b * nq + i, 0)),
            pl.BlockSpec((lk, kv.shape[1]), lambda b, i: (b, 0)),
            pl.BlockSpec(lam.shape, lambda b, i: (0, 0)),
            pl.BlockSpec((1, 2 * B_HEAD_DIM), lambda b, i: (0, 0)),
        ],
        out_specs=pl.BlockSpec((tq, q.shape[1]), lambda b, i: (b * nq + i, 0)),
        out_shape=jax.ShapeDtypeStruct(q.shape, BF16),
        compiler_params=_cparams("parallel", "arbitrary"),
        name="diff_attention",
    )(q, kv, lam, subln_g.reshape(1, 2 * B_HEAD_DIM))


def _decay(x, ld):
    return jnp.exp(-jnp.abs(x * ld))


def _ret_kernel(ld_ref, q_ref, k_ref, v_ref, *rest, nc, has_s0, cross):
    if has_s0:
        s0_ref, o_ref, st_ref, s_ref = rest
    else:
        o_ref, st_ref, s_ref = rest
    c_len = RET_CHUNK
    h = pl.program_id(1)
    s = pl.program_id(2)
    ld_f = ld_ref[0, h]
    ld_b = ld_ref[1, h]
    qb = q_ref[...].astype(BF16)
    k = k_ref[...] * (C_KEY_DIM ** -0.5)
    vb = v_ref[...].astype(BF16)
    idx = lax.broadcasted_iota(jnp.int32, (c_len, 1), 0).astype(F32)
    full = jnp.full((1, 1), float(c_len), F32)
    tdn = (((0,), (0,)), ((), ()))

    def init_state(d):
        if has_s0:
            s_ref[...] = s0_ref[d]
        else:
            s_ref[...] = jnp.zeros_like(s_ref)

    @pl.when(s < nc)
    def _forward():
        @pl.when(s == 0)
        def _():
            init_state(0)

        row0 = pl.multiple_of(s * c_len, c_len)
        rel = (lax.broadcasted_iota(jnp.int32, (c_len, c_len), 0)
               - lax.broadcasted_iota(jnp.int32, (c_len, c_len), 1)).astype(F32)
        dmat = (jnp.where(rel >= 0, _decay(jnp.maximum(rel, 0.0), ld_f), 0.0)
                + jnp.where(rel <= 0, _decay(jnp.maximum(-rel, 0.0), ld_b), 0.0))
        a = lax.dot_general(qb, k.astype(BF16), (((1,), (1,)), ((), ())), preferred_element_type=F32)
        o = _bdot((a * dmat).astype(BF16), vb)
        if cross:
            o = o + _bdot(qb, s_ref[...].astype(BF16)) * _decay(idx + 1.0, ld_f)
        o_ref[pl.ds(row0, c_len), :] = o
        kd = (k * _decay(c_len - 1.0 - idx, ld_f)).astype(BF16)
        s_ref[...] = s_ref[...] * _decay(full, ld_f) + lax.dot_general(kd, vb, tdn, preferred_element_type=F32)

        @pl.when(s == nc - 1)
        def _():
            st_ref[0] = s_ref[...]

    @pl.when(s >= nc)
    def _backward():
        @pl.when(s == nc)
        def _():
            init_state(1)

        row0 = pl.multiple_of((2 * nc - 1 - s) * c_len, c_len)
        if cross:
            o_ref[pl.ds(row0, c_len), :] += _bdot(qb, s_ref[...].astype(BF16)) * _decay(c_len - idx, ld_b)
        kd = (k * _decay(idx, ld_b)).astype(BF16)
        s_ref[...] = s_ref[...] * _decay(full, ld_b) + lax.dot_general(kd, vb, tdn, preferred_element_type=F32)

        @pl.when(s == 2 * nc - 1)
        def _():
            st_ref[1] = s_ref[...]


def retention(proj, log_decay, s0, *, row0, nb, seq):
    c_len = RET_CHUNK
    nc = seq // c_len
    r0 = row0 // c_len
    kblk = (C_HEADS * C_KEY_DIM) // C_KEY_DIM
    vblk = (2 * C_HEADS * C_KEY_DIM) // C_VAL_DIM

    def chunk(b, s):
        return r0 + b * nc + jnp.where(s < nc, s, 2 * nc - 1 - s)

    in_specs = [
        pl.BlockSpec(memory_space=pltpu.SMEM),
        pl.BlockSpec((c_len, C_KEY_DIM), lambda b, h, s: (chunk(b, s), h)),
        pl.BlockSpec((c_len, C_KEY_DIM), lambda b, h, s: (chunk(b, s), kblk + h)),
        pl.BlockSpec((c_len, C_VAL_DIM), lambda b, h, s: (chunk(b, s), vblk + h)),
    ]
    args = [log_decay, proj, proj, proj]
    if s0 is not None:
        in_specs.append(pl.BlockSpec((None, 2, None, C_KEY_DIM, C_VAL_DIM), lambda b, h, s: (b, 0, h, 0, 0)))
        args.append(s0)
    return pl.pallas_call(
        functools.partial(_ret_kernel, nc=nc, has_s0=s0 is not None, cross=(s0 is not None) or nc > 1),
        grid=(nb, C_HEADS, 2 * nc),
        in_specs=in_specs,
        out_specs=[
            pl.BlockSpec((seq, C_VAL_DIM), lambda b, h, s: (b, h)),
            pl.BlockSpec((None, 2, None, C_KEY_DIM, C_VAL_DIM), lambda b, h, s: (b, 0, h, 0, 0)),
        ],
        out_shape=[
            jax.ShapeDtypeStruct((nb * seq, C_HEADS * C_VAL_DIM), F32),
            jax.ShapeDtypeStruct((nb, 2, C_HEADS, C_KEY_DIM, C_VAL_DIM), F32),
        ],
        scratch_shapes=[pltpu.VMEM((C_KEY_DIM, C_VAL_DIM), F32)],
        compiler_params=_cparams("parallel", "parallel", "arbitrary"),
        name="retention",
    )(*args)


def _ret_out_kernel(oc_ref, ol_ref, g_ref, gn_ref, w_ref, x_ref, gate_ref, o_ref, *, tm):
    o = _pick_group(oc_ref, ol_ref, tm)
    parts = []
    for h in range(C_HEADS):
        oh = o[:, h * C_VAL_DIM:(h + 1) * C_VAL_DIM]
        ms = jnp.mean(oh * oh, axis=-1, keepdims=True)
        parts.append((oh * lax.rsqrt(ms + NORM_EPS)) * gn_ref[:, h * C_VAL_DIM:(h + 1) * C_VAL_DIM])
    a = (_silu(g_ref[...]) * jnp.concatenate(parts, axis=1)).astype(BF16)
    o_ref[...] = x_ref[...] + gate_ref[...] * _bdot(a, w_ref[...])


def retention_out(o_ctx, o_lat, proj, gn_g, w, x, gate, *, tm=512):
    t, d = x.shape
    vd = C_HEADS * C_VAL_DIM
    gblk = proj.shape[1] // vd - 1
    seg = lambda i: (_seg_of_tile(i, tm), 0, 0)
    return pl.pallas_call(
        functools.partial(_ret_out_kernel, tm=tm),
        grid=(t // tm,),
        in_specs=_group_specs(o_ctx, o_lat, tm, vd) + [
            pl.BlockSpec((tm, vd), lambda i: (i, gblk)),
            pl.BlockSpec((1, vd), lambda i: (0, 0)),
            pl.BlockSpec((vd, d), lambda i: (0, 0)),
            pl.BlockSpec((tm, d), lambda i: (i, 0)),
            pl.BlockSpec((None, 1, d), seg),
        ],
        out_specs=pl.BlockSpec((tm, d), lambda i: (i, 0)),
        out_shape=jax.ShapeDtypeStruct((t, d), F32),
        compiler_params=_cparams("parallel"),
        name="retention_out",
    )(o_ctx, o_lat, proj, gn_g.reshape(1, vd), w, x, gate)


def _filter_rows(t, seq, band_ref, w1_ref, b1_ref, w2_ref, b2_ref, w3_ref, fr_ref, delta_ref):
    t_norm = t / max(seq - 1, 1)
    lane = lax.broadcasted_iota(jnp.int32, (1, 128), 1)
    ang = (2.0 * math.pi * t) * band_ref[...] / seq
    feat = jnp.where(lane == 0, t_norm,
                     jnp.where(lane <= D_BANDS, jnp.cos(ang), jnp.where(lane <= 2 * D_BANDS, -jnp.sin(ang), 0.0)))
    a = jnp.sin(fr_ref[0:1] * (_bdot(feat.astype(BF16), w1_ref[...].astype(BF16)) + b1_ref[...]))
    a = jnp.sin(fr_ref[1:2] * (_bdot(a.astype(BF16), w2_ref[...].astype(BF16)) + b2_ref[...]))
    f = _bdot(a.astype(BF16), w3_ref[...].astype(BF16))
    window = jnp.exp(-t_norm * delta_ref[...]) + D_MOD_SHIFT
    return f * jnp.concatenate([window] * 4, axis=1)


def _filter_sum_kernel(band_ref, w1_ref, b1_ref, w2_ref, b2_ref, w3_ref, fr_ref, delta_ref, o_ref, *, seq, tm):
    i = pl.program_id(0)
    t = (i * tm + lax.broadcasted_iota(jnp.int32, (tm, 1), 0)).astype(F32)
    f = _filter_rows(t, seq, band_ref, w1_ref, b1_ref, w2_ref, b2_ref, w3_ref, fr_ref, delta_ref)
    part = jnp.sum(jnp.abs(f), axis=0, keepdims=True)

    @pl.when(i == 0)
    def _():
        o_ref[...] = part

    @pl.when(i > 0)
    def _():
        o_ref[...] += part


def _filter_gen_kernel(band_ref, w1_ref, b1_ref, w2_ref, b2_ref, w3_ref, fr_ref, delta_ref, sum_ref, o_ref, *, seq, tm):
    i = pl.program_id(0)
    r = i * tm + lax.broadcasted_iota(jnp.int32, (tm, 1), 0)
    t = jnp.where(r < seq, r, 2 * seq - r).astype(F32)
    f = _filter_rows(t, seq, band_ref, w1_ref, b1_ref, w2_ref, b2_ref, w3_ref, fr_ref, delta_ref)
    f = f / (sum_ref[...] + 1e-6)
    half = 2 * D_MODEL
    fwd, bwd = f[:, :half], f[:, half:]
    o_ref[...] = jnp.where(r < seq, fwd, jnp.where(r > seq, bwd, 0.0)) + jnp.where(r == 0, bwd, 0.0)


def hyena_circular_kernel(seq, band, w1p, b1, w2, b2, w3, freq, delta, *, tm):
    consts = [band, w1p, b1, w2, b2, w3, freq, delta]
    cspecs = [pl.BlockSpec(a.shape, lambda i: (0, 0)) for a in consts]
    nf = w3.shape[1]
    sums = pl.pallas_call(
        functools.partial(_filter_sum_kernel, seq=seq, tm=tm),
        grid=(seq // tm,),
        in_specs=cspecs,
        out_specs=pl.BlockSpec((1, nf), lambda i: (0, 0)),
        out_shape=jax.ShapeDtypeStruct((1, nf), F32),
        compiler_params=_cparams("arbitrary"),
        name="hyena_filter_sum",
    )(*consts)
    return pl.pallas_call(
        functools.partial(_filter_gen_kernel, seq=seq, tm=tm),
        grid=(2 * seq // tm,),
        in_specs=cspecs + [pl.BlockSpec((1, nf), lambda i: (0, 0))],
        out_specs=pl.BlockSpec((tm, nf // 2), lambda i: (i, 0)),
        out_shape=jax.ShapeDtypeStruct((2 * seq, nf // 2), F32),
        compiler_params=_cparams("parallel"),
        name="hyena_filter_gen",
    )(*consts, sums)


def _dft_cs(rows, cols, n):
    m = np.outer(np.arange(rows), np.arange(cols)) % n
    ang = 2.0 * np.pi * m / n
    return np.cos(ang), np.sin(ang)


def _stack_fwd(c, s):
    return np.block([[c, s], [-s, c]])


def _stack_inv(c, s):
    return np.block([[c, -s], [s, c]])


def _dft_mats():
    as_bf16 = lambda a: jnp.asarray(a, F32).astype(BF16)
    c1, s1 = _dft_cs(LAT_N1, LAT_N1, LAT_N1)
    c2, s2 = _dft_cs(LAT_N2, LAT_N2, LAT_N2)
    cc, sc = _dft_cs(CTX_N, CTX_N, CTX_N)
    h1, hc = LAT_N1 // 2, CTX_N // 2
    return dict(
        lat_g1=as_bf16(_stack_fwd(c1[:, :h1], s1[:, :h1])),
        lat_g1r=as_bf16(np.concatenate([c1, -s1], axis=0)),
        lat_g2=as_bf16(_stack_fwd(c2, s2)),
        lat_g2i=as_bf16(_stack_inv(c2, s2)),
        lat_g1i=as_bf16(_stack_inv(c1[:h1], s1[:h1]) / LAT_N),
        ctx_g=as_bf16(_stack_fwd(cc[:, :hc], sc[:, :hc])),
        ctx_gr=as_bf16(np.concatenate([cc, -sc], axis=0)),
        ctx_gi=as_bf16(_stack_inv(cc[:hc], sc[:hc]) / CTX_N),
    )


def _lmul_kernel(g_ref, x_ref, o_ref):
    o_ref[...] = _bdot(g_ref[...], x_ref[...].astype(BF16))


def left_matmul(g, x, *, row_blk=0, tc):
    m, k = g.shape
    n = x.shape[1]
    return pl.pallas_call(
        _lmul_kernel,
        grid=(n // tc,),
        in_specs=[pl.BlockSpec((m, k), lambda j: (0, 0)), pl.BlockSpec((k, tc), lambda j: (row_blk, j))],
        out_specs=pl.BlockSpec((m, tc), lambda j: (0, j)),
        out_shape=jax.ShapeDtypeStruct((m, n), F32),
        compiler_params=_cparams("parallel"),
        name="left_matmul",
    )(g, x)


def _gated_skip(conv, gate, y, skip_ref):
    reps = conv.shape[1] // D_MODEL
    skip = skip_ref[...] if reps == 1 else jnp.concatenate([skip_ref[...]] * reps, axis=1)
    return gate * (conv + skip * y)


def _lat_last_kernel(g_ref, b_ref, gate_ref, y_ref, skip_ref, o_ref):
    conv = _bdot(g_ref[...], b_ref[...].astype(BF16))
    o_ref[...] = _gated_skip(conv, gate_ref[...], y_ref[...], skip_ref).astype(o_ref.dtype)


def lat_last_stage(g, b, gate, gate_blk, y, y_blk, skip, *, tc, out_dtype):
    m, k = g.shape
    n = b.shape[1]
    return pl.pallas_call(
        _lat_last_kernel,
        grid=(n // tc,),
        in_specs=[
            pl.BlockSpec((m, k), lambda j: (0, 0)),
            pl.BlockSpec((k, tc), lambda j: (0, j)),
            pl.BlockSpec((m, tc), lambda j: (gate_blk, j)),
            pl.BlockSpec((m, tc), lambda j: (y_blk, j)),
            pl.BlockSpec((1, D_MODEL), lambda j: (0, 0)),
        ],
        out_specs=pl.BlockSpec((m, tc), lambda j: (0, j)),
        out_shape=jax.ShapeDtypeStruct((m, n), out_dtype),
        compiler_params=_cparams("parallel"),
        name="hyena_lat_last",
    )(g, b, gate, y, skip)


def _lat_mid_kernel(a_ref, *rest, conv):
    if conv:
        h_ref, g2_ref, g2i_ref, o_ref = rest
    else:
        g2_ref, o_ref = rest
    k1 = pl.program_id(0)
    n2 = lax.broadcasted_iota(jnp.int32, (LAT_N2, 1), 0)
    ang = (k1 * n2).astype(F32) * (2.0 * math.pi / LAT_N)
    c, s = jnp.cos(ang), jnp.sin(ang)
    ar, ai = a_ref[0], a_ref[1]
    t = jnp.concatenate([ar * c + ai * s, ai * c - ar * s], axis=0).astype(BF16)
    x = _bdot(g2_ref[...], t)
    xr, xi = x[:LAT_N2], x[LAT_N2:]
    if not conv:
        o_ref[0] = xr
        o_ref[1] = xi
        return
    hr, hi = h_ref[0], h_ref[1]
    y = jnp.concatenate([xr * hr - xi * hi, xr * hi + xi * hr], axis=0).astype(BF16)
    b = _bdot(g2i_ref[...], y)
    br, bi = b[:LAT_N2], b[LAT_N2:]
    o_ref[0] = br * c - bi * s
    o_ref[1] = bi * c + br * s


def lat_mid_stage(a, mats, h=None, order=0):
    w = a.shape[-1]
    blk = lambda width, col: pl.BlockSpec((2, None, LAT_N2, width), lambda k1, j: (0, k1, 0, col(j)))
    gspec = pl.BlockSpec((2 * LAT_N2, 2 * LAT_N2), lambda k1, j: (0, 0))
    if h is None:
        in_specs = [blk(D_MODEL, lambda j: j), gspec]
        args = [a, mats["lat_g2"]]
    else:
        in_specs = [blk(D_MODEL, lambda j: j), blk(D_MODEL, lambda j: order), gspec, gspec]
        args = [a, h, mats["lat_g2"], mats["lat_g2i"]]
    return pl.pallas_call(
        functools.partial(_lat_mid_kernel, conv=h is not None),
        grid=(LAT_N1, w // D_MODEL),
        in_specs=in_specs,
        out_specs=blk(D_MODEL, lambda j: j),
        out_shape=jax.ShapeDtypeStruct(a.shape, F32),
        compiler_params=_cparams("parallel", "arbitrary"),
        name="hyena_lat_mid",
    )(*args)


def _ctx_conv_kernel(y_ref, gate_ref, h_ref, g_ref, gi_ref, skip_ref, o_ref):
    n = CTX_N
    y = y_ref[...]
    z = _bdot(g_ref[...], y.astype(BF16))
    zr, zi = z[:n], z[n:]
    hr, hi = h_ref[0:n], h_ref[n:]
    w = jnp.concatenate([zr * hr - zi * hi, zr * hi + zi * hr], axis=0).astype(BF16)
    conv = _bdot(gi_ref[...], w)
    o_ref[...] = _gated_skip(conv, gate_ref[...], y, skip_ref).astype(o_ref.dtype)


def ctx_conv(y, y_plane, gate, gate_plane, h, order, mats, skip, *, out_dtype):
    rows = 2 * SEQ
    return pl.pallas_call(
        _ctx_conv_kernel,
        grid=(BATCH // 2,),
        in_specs=[
            pl.BlockSpec((None, rows, D_MODEL), lambda p: (y_plane, p, 0)),
            pl.BlockSpec((None, rows, D_MODEL), lambda p: (gate_plane, p, 0)),
            pl.BlockSpec((2 * CTX_N, D_MODEL), lambda p: (0, order)),
            pl.BlockSpec((2 * CTX_N, rows), lambda p: (0, 0)),
            pl.BlockSpec((rows, 2 * CTX_N), lambda p: (0, 0)),
            pl.BlockSpec((1, D_MODEL), lambda p: (0, 0)),
        ],
        out_specs=pl.BlockSpec((rows, D_MODEL), lambda p: (p, 0)),
        out_shape=jax.ShapeDtypeStruct((T_CTX, D_MODEL), out_dtype),
        compiler_params=_cparams("parallel"),
        name="hyena_ctx_conv",
    )(y, gate, h, mats["ctx_g"], mats["ctx_gi"], skip)


def hyena_core(z3, f_w1, f_b1, f_w2, f_b2, f_w3, f_freq, f_skip):
    mats = _dft_mats()
    bands = jnp.linspace(1e-4, D_BANDS - 1, D_BANDS, dtype=F32)
    band = jnp.zeros((1, 128), F32).at[0, 1:1 + D_BANDS].set(bands).at[0, 1 + D_BANDS:1 + 2 * D_BANDS].set(bands)
    w1p = jnp.zeros((128, D_FILTER_HIDDEN), F32).at[:D_EMB].set(f_w1)
    max_decay = math.log(D_DECAY_TARGET) / D_FAST_DECAY_PCT
    min_decay = math.log(D_DECAY_TARGET) / D_SLOW_DECAY_PCT
    delta = jnp.abs(jnp.linspace(min_decay, max_decay, D_MODEL, dtype=F32)).reshape(1, D_MODEL)
    fargs = (band, w1p, f_b1.reshape(1, -1), f_w2, f_b2.reshape(1, -1), f_w3, f_freq, delta)

    kc_ctx = hyena_circular_kernel(SEQ, *fargs, tm=SEQ)
    h_ctx = left_matmul(mats["ctx_gr"], kc_ctx, tc=D_MODEL)
    kc_lat = hyena_circular_kernel(DEC_SEQ, *fargs, tm=512)
    wide = LAT_N2 * 2 * D_MODEL
    a = left_matmul(mats["lat_g1r"], kc_lat.reshape(LAT_N1, wide), tc=8192)
    h_lat = lat_mid_stage(a.reshape(2, LAT_N1, LAT_N2, 2 * D_MODEL), mats)

    y1 = ctx_conv(z3, 2, z3, 0, h_ctx, 0, mats, f_skip[0:1], out_dtype=F32)
    y_ctx = ctx_conv(y1[None], 0, z3, 1, h_ctx, 1, mats, f_skip[1:2], out_dtype=BF16)

    cols = LAT_N2 * D_MODEL
    rows = DEC_BATCH * LAT_N1 // 2
    zw = z3.reshape(3 * T_ALL * D_MODEL // cols, cols)
    plane_blks = T_ALL * D_MODEL // cols // rows
    lat_blk = lambda plane: plane * plane_blks + T_CTX * D_MODEL // cols // rows
    y, y_blk = zw, lat_blk(2)
    for n in range(2):
        a = left_matmul(mats["lat_g1"], y, row_blk=y_blk, tc=8192)
        b = lat_mid_stage(a.reshape(2, LAT_N1, LAT_N2, D_MODEL), mats, h_lat, n)
        y = lat_last_stage(mats["lat_g1i"], b.reshape(2 * LAT_N1, cols), zw, lat_blk(n), y, y_blk, f_skip[n:n + 1],
                           tc=8192, out_dtype=F32 if n == 0 else BF16)
        y_blk = 0
    return y_ctx, y.reshape(T_LAT, D_MODEL)


def kernel(x_prompt, x_sample, cache_attn_k, cache_attn_v, cache_diff_k, cache_diff_v, state_ret, c, c_ctx, w_mod, b_mod, norm1_g, norm2_g, final_g, attn_w_qkv, attn_q_g, attn_k_g, attn_w_o, diff_w_qkv, diff_lambda, diff_subln_g, diff_w_o, ret_w_in, ret_log_decay, ret_gn_g, ret_w_o, hyena_w_in, hyena_sc_w, hyena_sc_b, hyena_f_w1, hyena_f_b1, hyena_f_w2, hyena_f_b2, hyena_f_w3, hyena_f_freq, hyena_f_skip, hyena_w_o, ffn_w_up, ffn_conv_w, ffn_conv_b, ffn_w_down):
    d = D_MODEL
    x = jnp.concatenate([x_prompt.reshape(T_CTX, d), x_sample.reshape(T_LAT, d)], axis=0)
    cond = jnp.zeros((SEG_ROWS, d), F32).at[0].set(c_ctx).at[1:N_SEG].set(c)
    mod = modulation(cond, w_mod, b_mod)
    mod = mod.reshape(DEPTH, SEG_ROWS, 6, d).transpose(0, 2, 1, 3).reshape(DEPTH, 6, SEG_ROWS, 1, d)
    rope = rope_tables()
    bf = lambda w: w.astype(BF16)
    out = {}

    for l in range(DEPTH):
        m, j = l % 4, l // 4
        sh1, sc1, g1, sh2, sc2, g2 = (mod[l, i] for i in range(6))
        if m == 0:
            nq, nk = A_HEADS * A_HEAD_DIM, A_KV_HEADS * A_HEAD_DIM
            scale = A_HEAD_DIM ** -0.5
            qkv = norm_proj(x, norm1_g[l], sh1, sc1, bf(attn_w_qkv[j]))
            q_c, k_c, kv_c = qk_prep(qkv, row0=0, rows=T_CTX, nq=nq, nk=nk, q_g=attn_q_g[j], k_g=attn_k_g[j],
                                     scale=scale)
            q_l, _, kv_l = qk_prep(qkv, row0=T_CTX, rows=T_LAT, nq=nq, nk=nk, q_g=attn_q_g[j], k_g=attn_k_g[j],
                                   rope=rope, scale=scale)
            out["attn_k"] = k_c.reshape(BATCH, 1, SEQ, A_KV_HEADS, A_HEAD_DIM)
            out["attn_v"] = qkv[:T_CTX, nq + nk:].reshape(BATCH, 1, SEQ, A_KV_HEADS, A_HEAD_DIM)
            cache = jnp.concatenate([cache_attn_k[:, j].reshape(DEC_BATCH, PAST_LEN, nk),
                                     cache_attn_v[:, j].reshape(DEC_BATCH, PAST_LEN, nk)], axis=-1).astype(BF16)
            lk = PAST_LEN + DEC_SEQ
            kv_all = jnp.concatenate([cache, kv_l.reshape(DEC_BATCH, DEC_SEQ, 2 * nk)], axis=1).reshape(DEC_BATCH * lk, 2 * nk)
            o_c = gqa_attention(q_c, kv_c, nb=BATCH, lq=SEQ, lk=SEQ, tq=SEQ)
            o_l = gqa_attention(q_l, kv_all, nb=DEC_BATCH, lq=DEC_SEQ, lk=lk, tq=128)
            x = out_proj_residual(o_c, o_l, bf(attn_w_o[j]), x, g1)
        elif m == 1:
            nq = nk = B_HEADS * 2 * B_HEAD_DIM
            scale = B_HEAD_DIM ** -0.5
            lam_init = 0.8 - 0.6 * math.exp(-0.3 * l)
            qkv = norm_proj(x, norm1_g[l], sh1, sc1, bf(diff_w_qkv[j]))
            q_c, k_c, kv_c = qk_prep(qkv, row0=0, rows=T_CTX, nq=nq, nk=nk, scale=scale)
            q_l, _, kv_l = qk_prep(qkv, row0=T_CTX, rows=T_LAT, nq=nq, nk=nk, rope=rope, scale=scale)
            out["diff_k"] = k_c.reshape(BATCH, 1, SEQ, B_HEADS, 2 * B_HEAD_DIM)
            out["diff_v"] = qkv[:T_CTX, nq + nk:].reshape(BATCH, 1, SEQ, B_HEADS, 2 * B_HEAD_DIM)
            cache = jnp.concatenate([cache_diff_k[:, j].reshape(DEC_BATCH, PAST_LEN, nk),
                                     cache_diff_v[:, j].reshape(DEC_BATCH, PAST_LEN, nk)], axis=-1).astype(BF16)
            lk = PAST_LEN + DEC_SEQ
            kv_all = jnp.concatenate([cache, kv_l.reshape(DEC_BATCH, DEC_SEQ, 2 * nk)], axis=1).reshape(DEC_BATCH * lk, 2 * nk)
            dargs = (diff_lambda[j], diff_subln_g[j])
            o_c = diff_attention(q_c, kv_c, *dargs, nb=BATCH, lq=SEQ, lk=SEQ, tq=SEQ, lam_init=lam_init)
            o_l = diff_attention(q_l, kv_all, *dargs, nb=DEC_BATCH, lq=DEC_SEQ, lk=lk, tq=256, lam_init=lam_init)
            x = out_proj_residual(o_c, o_l, bf(diff_w_o[j]), x, g1)
        elif m == 2:
            proj = norm_proj(x, norm1_g[l], sh1, sc1, bf(ret_w_in[j]))
            o_c, st = retention(proj, ret_log_decay[j], None, row0=0, nb=BATCH, seq=SEQ)
            o_l, _ = retention(proj, ret_log_decay[j], state_ret[:, j], row0=T_CTX, nb=DEC_BATCH, seq=DEC_SEQ)
            out["ret_s"] = st.reshape(BATCH, 1, 2, C_HEADS, C_KEY_DIM, C_VAL_DIM)
            x = retention_out(o_c, o_l, proj, ret_gn_g[j], bf(ret_w_o[j]), x, g1)
        else:
            z3 = norm_proj_conv(x, norm1_g[l], sh1, sc1, bf(hyena_w_in[j]), hyena_sc_w[j], hyena_sc_b[j])
            y_c, y_l = hyena_core(z3, hyena_f_w1[j], hyena_f_b1[j], hyena_f_w2[j], hyena_f_b2[j], hyena_f_w3[j],
                                  hyena_f_freq[j], hyena_f_skip[j])
            x = out_proj_residual(y_c, y_l, bf(hyena_w_o[j]), x, g1)
        a = norm_proj_glu(x, norm2_g[l], sh2, sc2, bf(ffn_w_up[l]), ffn_conv_w[l], ffn_conv_b[l])
        x = out_proj_residual(a, a, bf(ffn_w_down[l]), x, g2, final_g if l == DEPTH - 1 else None)

    y_prompt = x[:T_CTX].reshape(BATCH, SEQ, d)
    y_sample = x[T_CTX:].reshape(DEC_BATCH, DEC_SEQ, d)
    return (y_prompt, y_sample, out["attn_k"], out["attn_v"], out["diff_k"], out["diff_v"], out["ret_s"])
```

```python
import functools
import math

import jax
import jax.numpy as jnp
import numpy as np
from jax import lax
from jax.experimental import pallas as pl
from jax.experimental.pallas import tpu as pltpu

F32 = jnp.float32
BF16 = jnp.bfloat16

D_MODEL = 1024
BATCH = 32
SEQ = 256
DEPTH = 4
DEC_BATCH = 2
DEC_SEQ = 4096
PAST_LEN = 256
GRID_W = 64
ROPE_THETA = 10000.0
NORM_EPS = 1e-6
A_HEADS = 16
A_KV_HEADS = 4
A_HEAD_DIM = 64
A_GROUP = A_HEADS // A_KV_HEADS
B_HEADS = 8
B_HEAD_DIM = 64
B_SUBLN_EPS = 1e-5
C_HEADS = 4
C_KEY_DIM = 256
C_VAL_DIM = 512
D_BANDS = 16
D_EMB = 1 + 2 * D_BANDS
D_FILTER_HIDDEN = 64
D_FAST_DECAY_PCT = 0.3
D_SLOW_DECAY_PCT = 1.5
D_DECAY_TARGET = 1e-2
D_MOD_SHIFT = 0.05
FFN_DIM = 2816

T_CTX = BATCH * SEQ
T_LAT = DEC_BATCH * DEC_SEQ
T_ALL = T_CTX + T_LAT
N_SEG = 1 + DEC_BATCH
SEG_ROWS = 8

HALO = 16
RET_CHUNK = 256
VMEM_LIMIT = 56 * 1024 * 1024

LAT_N = 2 * DEC_SEQ
LAT_N1 = 64
LAT_N2 = LAT_N // LAT_N1
CTX_N = 2 * SEQ


def _cparams(*sem):
    return pltpu.CompilerParams(dimension_semantics=sem, vmem_limit_bytes=VMEM_LIMIT)


def _seg_of_tile(i, tm):
    start = i * tm
    return jnp.where(start < T_CTX, 0, 1 + (start - T_CTX) // DEC_SEQ)


def _silu(x):
    return x * jax.nn.sigmoid(x)


def _bdot(a, b):
    return jnp.dot(a, b, preferred_element_type=F32)


def _norm_mod(x, g, sh, sc):
    ms = jnp.mean(x * x, axis=-1, keepdims=True)
    y = (x * lax.rsqrt(ms + NORM_EPS)) * g
    return y * (1.0 + sc) + sh


def _mod_kernel(c_ref, w_ref, b_ref, o_ref):
    s = _silu(c_ref[...]).astype(BF16)
    o_ref[...] = _bdot(s, w_ref[...].astype(BF16)) + b_ref[...]


def modulation(cond, w_mod, b_mod):
    tn = 1536
    n = w_mod.shape[-1]
    return pl.pallas_call(
        _mod_kernel,
        grid=(DEPTH, n // tn),
        in_specs=[
            pl.BlockSpec((SEG_ROWS, D_MODEL), lambda l, j: (0, 0)),
            pl.BlockSpec((None, D_MODEL, tn), lambda l, j: (l, 0, j)),
            pl.BlockSpec((None, 1, tn), lambda l, j: (l, 0, j)),
        ],
        out_specs=pl.BlockSpec((None, SEG_ROWS, tn), lambda l, j: (l, 0, j)),
        out_shape=jax.ShapeDtypeStruct((DEPTH, SEG_ROWS, n), F32),
        compiler_params=_cparams("arbitrary", "arbitrary"),
        name="modulation",
    )(cond, w_mod, b_mod.reshape(DEPTH, 1, n))


def _proj_kernel(x_ref, g_ref, sh_ref, sc_ref, w_ref, o_ref, h_ref):
    @pl.when(pl.program_id(1) == 0)
    def _():
        h_ref[...] = _norm_mod(x_ref[...], g_ref[...], sh_ref[...], sc_ref[...]).astype(BF16)

    o_ref[...] = _bdot(h_ref[...], w_ref[...]).astype(o_ref.dtype)


def norm_proj(x, g, sh, sc, w, *, tm=1024, tn=512, out_dtype=F32):
    t, d = x.shape
    n = w.shape[1]
    seg = lambda i, j: (_seg_of_tile(i, tm), 0, 0)
    return pl.pallas_call(
        _proj_kernel,
        grid=(t // tm, n // tn),
        in_specs=[
            pl.BlockSpec((tm, d), lambda i, j: (i, 0)),
            pl.BlockSpec((1, d), lambda i, j: (0, 0)),
            pl.BlockSpec((None, 1, d), seg),
            pl.BlockSpec((None, 1, d), seg),
            pl.BlockSpec((d, tn), lambda i, j: (0, j)),
        ],
        out_specs=pl.BlockSpec((tm, tn), lambda i, j: (i, j)),
        out_shape=jax.ShapeDtypeStruct((t, n), out_dtype),
        scratch_shapes=[pltpu.VMEM((tm, d), BF16)],
        compiler_params=_cparams("parallel", "arbitrary"),
        name="norm_proj",
    )(x, g.reshape(1, d), sh, sc, w)


def _conv3(u, cw, cb, i, tm):
    rows = u.shape[0]
    up = pltpu.roll(u, 1, 0)[HALO:HALO + tm]
    uc = u[HALO:HALO + tm]
    un = pltpu.roll(u, rows - 1, 0)[HALO:HALO + tm]
    sub = lax.broadcasted_iota(jnp.int32, (8, 1), 0)
    is_ctx = i * tm < T_CTX
    ups, uns = [], []
    for r in range(0, tm, SEQ):
        start = i * tm + r
        first = jnp.logical_or(is_ctx, (start & (DEC_SEQ - 1)) == 0)
        last = jnp.logical_or(is_ctx, ((start + SEQ) & (DEC_SEQ - 1)) == 0)
        ups += [jnp.where(jnp.logical_and(sub == 0, first), 0.0, up[r:r + 8]), up[r + 8:r + SEQ]]
        uns += [un[r:r + SEQ - 8], jnp.where(jnp.logical_and(sub == 7, last), 0.0, un[r + SEQ - 8:r + SEQ])]
    up = jnp.concatenate(ups, axis=0)
    un = jnp.concatenate(uns, axis=0)
    return up * cw[0:1] + uc * cw[1:2] + un * cw[2:3] + cb


def _fill_h(h_ref, xp_ref, x_ref, xn_ref, g_ref, sh_ref, sc_ref, tm):
    g, sh, sc = g_ref[...], sh_ref[...], sc_ref[...]
    h_ref[0:HALO] = _norm_mod(xp_ref[...], g, sh, sc).astype(BF16)
    h_ref[HALO:HALO + tm] = _norm_mod(x_ref[...], g, sh, sc).astype(BF16)
    h_ref[HALO + tm:] = _norm_mod(xn_ref[...], g, sh, sc).astype(BF16)


def _proj_conv_kernel(xp_ref, x_ref, xn_ref, g_ref, sh_ref, sc_ref, w_ref, cw_ref, cb_ref, o_ref, h_ref, *, tm):
    i = pl.program_id(0)

    @pl.when(pl.program_id(1) == 0)
    def _():
        _fill_h(h_ref, xp_ref, x_ref, xn_ref, g_ref, sh_ref, sc_ref, tm)

    u = _bdot(h_ref[...], w_ref[...])
    o_ref[...] = _conv3(u, cw_ref[...], cb_ref[...], i, tm)


def _ffn_kernel(xp_ref, x_ref, xn_ref, g_ref, sh_ref, sc_ref, gate_ref, wa_ref, wb_ref, cwa_ref, cwb_ref, cba_ref,
                cbb_ref, wd_ref, *rest, tm, final):
    if final:
        fg_ref, o_ref, h_ref = rest
    else:
        o_ref, h_ref = rest
    i = pl.program_id(0)
    j = pl.program_id(1)

    @pl.when(j == 0)
    def _():
        _fill_h(h_ref, xp_ref, x_ref, xn_ref, g_ref, sh_ref, sc_ref, tm)

    h = h_ref[...]
    a = _conv3(_bdot(h, wa_ref[...]), cwa_ref[...], cba_ref[...], i, tm)
    b = _conv3(_bdot(h, wb_ref[...]), cwb_ref[...], cbb_ref[...], i, tm)
    part = _bdot((_silu(a) * b).astype(BF16), wd_ref[...])

    @pl.when(j == 0)
    def _():
        o_ref[...] = part

    @pl.when(j > 0)
    def _():
        o_ref[...] += part

    @pl.when(j == pl.num_programs(1) - 1)
    def _():
        y = x_ref[...] + gate_ref[...] * o_ref[...]
        if final:
            ms = jnp.mean(y * y, axis=-1, keepdims=True)
            y = (y * lax.rsqrt(ms + NORM_EPS)) * fg_ref[...]
        o_ref[...] = y


def _halo_specs(t, tm, d):
    per = tm // HALO
    last_blk = t // HALO - 1
    return [
        pl.BlockSpec((HALO, d), lambda i, j: (jnp.maximum(i * per - 1, 0), 0)),
        pl.BlockSpec((tm, d), lambda i, j: (i, 0)),
        pl.BlockSpec((HALO, d), lambda i, j: (jnp.minimum((i + 1) * per, last_blk), 0)),
    ]


def norm_proj_conv(x, g, sh, sc, w, cw, cb, *, tm=1024, tn=512):
    t, d = x.shape
    n = w.shape[1]
    per_group = d // tn
    seg = lambda i, j: (_seg_of_tile(i, tm), 0, 0)
    return pl.pallas_call(
        functools.partial(_proj_conv_kernel, tm=tm),
        grid=(t // tm, n // tn),
        in_specs=_halo_specs(t, tm, d) + [
            pl.BlockSpec((1, d), lambda i, j: (0, 0)),
            pl.BlockSpec((None, 1, d), seg),
            pl.BlockSpec((None, 1, d), seg),
            pl.BlockSpec((d, tn), lambda i, j: (0, j)),
            pl.BlockSpec((3, tn), lambda i, j: (0, j)),
            pl.BlockSpec((1, tn), lambda i, j: (0, j)),
        ],
        out_specs=pl.BlockSpec((None, tm, tn), lambda i, j: (j // per_group, i, j % per_group)),
        out_shape=jax.ShapeDtypeStruct((n // d, t, d), F32),
        scratch_shapes=[pltpu.VMEM((tm + 2 * HALO, d), BF16)],
        compiler_params=_cparams("parallel", "arbitrary"),
        name="norm_proj_conv",
    )(x, x, x, g.reshape(1, d), sh, sc, w, cw, cb.reshape(1, n))


def conv_ffn(x, g, sh, sc, gate, w_up, cw, cb, w_down, final_g=None, *, tm=1024, tn=256):
    t, d = x.shape
    f = w_down.shape[0]
    nb = f // tn
    cb = cb.reshape(1, 2 * f)
    seg = lambda i, j: (_seg_of_tile(i, tm), 0, 0)
    in_specs = _halo_specs(t, tm, d) + [
        pl.BlockSpec((1, d), lambda i, j: (0, 0)),
        pl.BlockSpec((None, 1, d), seg),
        pl.BlockSpec((None, 1, d), seg),
        pl.BlockSpec((None, 1, d), seg),
        pl.BlockSpec((d, tn), lambda i, j: (0, j)),
        pl.BlockSpec((d, tn), lambda i, j: (0, j + nb)),
        pl.BlockSpec((3, tn), lambda i, j: (0, j)),
        pl.BlockSpec((3, tn), lambda i, j: (0, j + nb)),
        pl.BlockSpec((1, tn), lambda i, j: (0, j)),
        pl.BlockSpec((1, tn), lambda i, j: (0, j + nb)),
        pl.BlockSpec((tn, d), lambda i, j: (j, 0)),
    ]
    args = [x, x, x, g.reshape(1, d), sh, sc, gate, w_up, w_up, cw, cw, cb, cb, w_down]
    if final_g is not None:
        in_specs.append(pl.BlockSpec((1, d), lambda i, j: (0, 0)))
        args.append(final_g.reshape(1, d))
    return pl.pallas_call(
        functools.partial(_ffn_kernel, tm=tm, final=final_g is not None),
        grid=(t // tm, nb),
        in_specs=in_specs,
        out_specs=pl.BlockSpec((tm, d), lambda i, j: (i, 0)),
        out_shape=jax.ShapeDtypeStruct((t, d), F32),
        scratch_shapes=[pltpu.VMEM((tm + 2 * HALO, d), BF16)],
        compiler_params=_cparams("parallel", "arbitrary"),
        name="conv_ffn",
    )(*args)


def _group_specs(a_ctx, a_lat, tm, width):
    nctx = T_CTX // tm
    off = nctx if a_lat.shape[0] == T_ALL else 0
    return [
        pl.BlockSpec((tm, width), lambda i: (jnp.minimum(i, nctx - 1), 0)),
        pl.BlockSpec((tm, width), lambda i: (off + jnp.maximum(i - nctx, 0), 0)),
    ]


def _pick_group(ac_ref, al_ref, tm):
    return jnp.where(pl.program_id(0) < T_CTX // tm, ac_ref[...], al_ref[...])


def _out_proj_kernel(ac_ref, al_ref, w_ref, x_ref, gate_ref, o_ref, *, tm):
    a = _pick_group(ac_ref, al_ref, tm).astype(BF16)
    o_ref[...] = x_ref[...] + gate_ref[...] * _bdot(a, w_ref[...])


def out_proj_residual(a_ctx, a_lat, w, x, gate, *, tm=512):
    t, d = x.shape
    k = w.shape[0]
    seg = lambda i: (_seg_of_tile(i, tm), 0, 0)
    return pl.pallas_call(
        functools.partial(_out_proj_kernel, tm=tm),
        grid=(t // tm,),
        in_specs=_group_specs(a_ctx, a_lat, tm, k) + [
            pl.BlockSpec((k, d), lambda i: (0, 0)),
            pl.BlockSpec((tm, d), lambda i: (i, 0)),
            pl.BlockSpec((None, 1, d), seg),
        ],
        out_specs=pl.BlockSpec((tm, d), lambda i: (i, 0)),
        out_shape=jax.ShapeDtypeStruct((t, d), F32),
        compiler_params=_cparams("parallel"),
        name="out_proj_residual",
    )(a_ctx, a_lat, w, x, gate)


def _head_mean_sq(x, head_dim):
    n = x.shape[1]
    x2 = x * x
    hi = x2.astype(BF16)
    lo = (x2 - hi.astype(F32)).astype(BF16)
    blk = 256
    r = lax.broadcasted_iota(jnp.int32, (blk, blk), 0) // head_dim
    c = lax.broadcasted_iota(jnp.int32, (blk, blk), 1) // head_dim
    ones = (r == c).astype(BF16)
    parts = []
    for s in range(0, n, blk):
        parts.append(_bdot(hi[:, s:s + blk], ones) + _bdot(lo[:, s:s + blk], ones))
    ss = parts[0] if len(parts) == 1 else jnp.concatenate(parts, axis=1)
    return ss * (1.0 / head_dim)


def _rope(x, cos, sin):
    n = x.shape[1]
    lane = lax.broadcasted_iota(jnp.int32, (1, 128), 1)
    lower = (lane & 31) < 16
    outs = []
    for s in range(0, n, 128):
        xs = x[:, s:s + 128]
        partner = jnp.where(lower, pltpu.roll(xs, 128 - 16, 1), pltpu.roll(xs, 16, 1))
        outs.append(xs * cos + partner * sin)
    return jnp.concatenate(outs, axis=1)


def _qk_prep_kernel(*refs, nq, nk, norm, rope, scale):
    it = iter(refs)
    q_ref, k_ref, v_ref = next(it), next(it), next(it)
    if norm:
        qg_ref, kg_ref = next(it), next(it)
    if rope:
        cos_ref, sin_ref = next(it), next(it)
    qo_ref, ko_ref, kvo_ref = next(it), next(it), next(it)
    q = q_ref[...]
    k = k_ref[...]
    if norm:
        q = (q * lax.rsqrt(_head_mean_sq(q, A_HEAD_DIM) + NORM_EPS)) * qg_ref[...]
        k = (k * lax.rsqrt(_head_mean_sq(k, A_HEAD_DIM) + NORM_EPS)) * kg_ref[...]
    if rope:
        cos, sin = cos_ref[...], sin_ref[...]
        q = _rope(q, cos, sin)
        k = _rope(k, cos, sin)
    qo_ref[...] = (q * scale).astype(BF16)
    ko_ref[...] = k
    kvo_ref[:, 0:nk] = k.astype(BF16)
    kvo_ref[:, nk:] = v_ref[...].astype(BF16)


def qk_prep(qkv, *, row0, rows, nq, nk, q_g=None, k_g=None, rope=None, scale, tm=512):
    norm = q_g is not None
    r0 = row0 // tm
    qb = nq // nk
    in_specs = [
        pl.BlockSpec((tm, nq), lambda i: (i + r0, 0)),
        pl.BlockSpec((tm, nk), lambda i: (i + r0, qb)),
        pl.BlockSpec((tm, nk), lambda i: (i + r0, qb + 1)),
    ]
    args = [qkv, qkv, qkv]
    if norm:
        in_specs += [pl.BlockSpec((1, nq), lambda i: (0, 0)), pl.BlockSpec((1, nk), lambda i: (0, 0))]
        args += [jnp.tile(q_g, nq // q_g.shape[0]).reshape(1, nq), jnp.tile(k_g, nk // k_g.shape[0]).reshape(1, nk)]
    if rope is not None:
        per = DEC_SEQ // tm
        in_specs += [pl.BlockSpec((tm, 128), lambda i: (i % per, 0))] * 2
        args += list(rope)
    return pl.pallas_call(
        functools.partial(_qk_prep_kernel, nq=nq, nk=nk, norm=norm, rope=rope is not None, scale=scale),
        grid=(rows // tm,),
        in_specs=in_specs,
        out_specs=[
            pl.BlockSpec((tm, nq), lambda i: (i, 0)),
            pl.BlockSpec((tm, nk), lambda i: (i, 0)),
            pl.BlockSpec((tm, 2 * nk), lambda i: (i, 0)),
        ],
        out_shape=[
            jax.ShapeDtypeStruct((rows, nq), BF16),
            jax.ShapeDtypeStruct((rows, nk), F32),
            jax.ShapeDtypeStruct((rows, 2 * nk), BF16),
        ],
        compiler_params=_cparams("parallel"),
        name="qk_prep",
    )(*args)


def rope_tables():
    t = jnp.arange(DEC_SEQ)
    row = (t // GRID_W).astype(F32)
    col = (t % GRID_W).astype(F32)
    half = A_HEAD_DIM // 4
    inv_freq = ROPE_THETA ** (-jnp.arange(half, dtype=F32) / half)
    ar = row[:, None] * inv_freq[None, :]
    ac = col[:, None] * inv_freq[None, :]
    cos = jnp.concatenate([jnp.cos(ar), jnp.cos(ar), jnp.cos(ac), jnp.cos(ac)], axis=1)
    sin = jnp.concatenate([-jnp.sin(ar), jnp.sin(ar), -jnp.sin(ac), jnp.sin(ac)], axis=1)
    return jnp.tile(cos, (1, 2)), jnp.tile(sin, (1, 2))


_NT = (((1,), (1,)), ((), ()))


def _fill_vt(vt_ref, kv_ref, kw):
    @pl.when(pl.program_id(1) == 0)
    def _():
        vt_ref[...] = kv_ref[:, kw:].astype(F32).T.astype(BF16)


def _softmax_t(st):
    p = jnp.exp(st - jnp.max(st, axis=0, keepdims=True))
    return p, jnp.sum(p, axis=0, keepdims=True)


def _gqa_kernel(q_ref, kv_ref, o_ref, vt_ref, *, tq):
    d = A_HEAD_DIM
    kvw = A_KV_HEADS * d
    _fill_vt(vt_ref, kv_ref, kvw)
    blocks = []
    for kh in range(A_KV_HEADS):
        base = kh * A_GROUP * d
        qs = jnp.concatenate([q_ref[:, base + g * d:base + (g + 1) * d] for g in range(A_GROUP)], axis=0)
        st = lax.dot_general(kv_ref[:, kh * d:(kh + 1) * d], qs, _NT, preferred_element_type=F32)
        p, l = _softmax_t(st)
        ot = _bdot(vt_ref[kh * d:(kh + 1) * d, :], p.astype(BF16)) / l
        blocks += [ot[:, g * tq:(g + 1) * tq] for g in range(A_GROUP)]
    o_ref[...] = jnp.concatenate(blocks, axis=0).T.astype(o_ref.dtype)


def gqa_attention(q, kv, *, nb, lq, lk, tq):
    nq = lq // tq
    kvw = kv.shape[1] // 2
    return pl.pallas_call(
        functools.partial(_gqa_kernel, tq=tq),
        grid=(nb, nq),
        in_specs=[
            pl.BlockSpec((tq, q.shape[1]), lambda b, i: (b * nq + i, 0)),
            pl.BlockSpec((lk, kv.shape[1]), lambda b, i: (b, 0), pipeline_mode=pl.Buffered(1)),
        ],
        out_specs=pl.BlockSpec((tq, q.shape[1]), lambda b, i: (b * nq + i, 0)),
        out_shape=jax.ShapeDtypeStruct(q.shape, BF16),
        scratch_shapes=[pltpu.VMEM((kvw, lk), BF16)],
        compiler_params=_cparams("parallel", "arbitrary"),
        name="gqa_attention",
    )(q, kv)


def _diff_kernel(q_ref, kv_ref, lam_ref, sg_ref, o_ref, vt_ref, *, lam_init):
    d = B_HEAD_DIM
    kw = B_HEADS * 2 * d
    _fill_vt(vt_ref, kv_ref, kw)
    lf = lam_ref[...]
    lam = (jnp.exp(jnp.sum(lf[0:1] * lf[1:2], axis=-1, keepdims=True))
           - jnp.exp(jnp.sum(lf[2:3] * lf[3:4], axis=-1, keepdims=True)) + lam_init)
    blocks = []
    for h in range(B_HEADS):
        c0 = h * 2 * d
        s1 = lax.dot_general(kv_ref[:, c0:c0 + d], q_ref[:, c0:c0 + d], _NT, preferred_element_type=F32)
        s2 = lax.dot_general(kv_ref[:, c0 + d:c0 + 2 * d], q_ref[:, c0 + d:c0 + 2 * d], _NT, preferred_element_type=F32)
        p1, l1 = _softmax_t(s1)
        p2, l2 = _softmax_t(s2)
        vt = vt_ref[c0:c0 + 2 * d, :]
        ot = _bdot(vt, p1.astype(BF16)) * (1.0 / l1) - _bdot(vt, p2.astype(BF16)) * (lam / l2)
        ms = jnp.mean(ot * ot, axis=0, keepdims=True)
        blocks.append(((ot * lax.rsqrt(ms + B_SUBLN_EPS)) * sg_ref[...]) * (1.0 - lam_init))
    o_ref[...] = jnp.concatenate(blocks, axis=0).T.astype(o_ref.dtype)


def diff_attention(q, kv, lam, subln_g, *, nb, lq, lk, tq, lam_init):
    nq = lq // tq
    kw = kv.shape[1] // 2
    return pl.pallas_call(
        functools.partial(_diff_kernel, lam_init=lam_init),
        grid=(nb, nq),
        in_specs=[
            pl.BlockSpec((tq, q.shape[1]), lambda b, i: (b * nq + i, 0)),
            pl.BlockSpec((lk, kv.shape[1]), lambda b, i: (b, 0), pipeline_mode=pl.Buffered(1)),
            pl.BlockSpec(lam.shape, lambda b, i: (0, 0)),
            pl.BlockSpec((2 * B_HEAD_DIM, 1), lambda b, i: (0, 0)),
        ],
        out_specs=pl.BlockSpec((tq, q.shape[1]), lambda b, i: (b * nq + i, 0)),
        out_shape=jax.ShapeDtypeStruct(q.shape, BF16),
        scratch_shapes=[pltpu.VMEM((kw, lk), BF16)],
        compiler_params=_cparams("parallel", "arbitrary"),
        name="diff_attention",
    )(q, kv, lam, subln_g.reshape(2 * B_HEAD_DIM, 1))


def _decay(x, ld):
    return jnp.exp(-jnp.abs(x * ld))


def _ret_kernel(ld_ref, q_ref, k_ref, v_ref, *rest, nc, has_s0, cross):
    if has_s0:
        s0_ref, o_ref, st_ref, s_ref = rest
    else:
        o_ref, st_ref, s_ref = rest
    c_len = RET_CHUNK
    h = pl.program_id(1)
    s = pl.program_id(2)
    ld_f = ld_ref[0, h]
    ld_b = ld_ref[1, h]
    qb = q_ref[...].astype(BF16)
    k = k_ref[...] * (C_KEY_DIM ** -0.5)
    vb = v_ref[...].astype(BF16)
    idx = lax.broadcasted_iota(jnp.int32, (c_len, 1), 0).astype(F32)
    full = jnp.full((1, 1), float(c_len), F32)
    tdn = (((0,), (0,)), ((), ()))

    def init_state(d):
        if has_s0:
            s_ref[...] = s0_ref[d]
        else:
            s_ref[...] = jnp.zeros_like(s_ref)

    @pl.when(s < nc)
    def _forward():
        @pl.when(s == 0)
        def _():
            init_state(0)

        row0 = pl.multiple_of(s * c_len, c_len)
        rel = (lax.broadcasted_iota(jnp.int32, (c_len, c_len), 0)
               - lax.broadcasted_iota(jnp.int32, (c_len, c_len), 1)).astype(F32)
        dmat = (jnp.where(rel >= 0, _decay(jnp.maximum(rel, 0.0), ld_f), 0.0)
                + jnp.where(rel <= 0, _decay(jnp.maximum(-rel, 0.0), ld_b), 0.0))
        a = lax.dot_general(qb, k.astype(BF16), (((1,), (1,)), ((), ())), preferred_element_type=F32)
        o = _bdot((a * dmat).astype(BF16), vb)
        if cross:
            o = o + _bdot(qb, s_ref[...].astype(BF16)) * _decay(idx + 1.0, ld_f)
        o_ref[pl.ds(row0, c_len), :] = o
        kd = (k * _decay(c_len - 1.0 - idx, ld_f)).astype(BF16)
        s_ref[...] = s_ref[...] * _decay(full, ld_f) + lax.dot_general(kd, vb, tdn, preferred_element_type=F32)

        @pl.when(s == nc - 1)
        def _():
            st_ref[0] = s_ref[...]

    @pl.when(s >= nc)
    def _backward():
        @pl.when(s == nc)
        def _():
            init_state(1)

        row0 = pl.multiple_of((2 * nc - 1 - s) * c_len, c_len)
        if cross:
            o_ref[pl.ds(row0, c_len), :] += _bdot(qb, s_ref[...].astype(BF16)) * _decay(c_len - idx, ld_b)
        kd = (k * _decay(idx, ld_b)).astype(BF16)
        s_ref[...] = s_ref[...] * _decay(full, ld_b) + lax.dot_general(kd, vb, tdn, preferred_element_type=F32)

        @pl.when(s == 2 * nc - 1)
        def _():
            st_ref[1] = s_ref[...]


def retention(proj, log_decay, s0, *, row0, nb, seq):
    c_len = RET_CHUNK
    nc = seq // c_len
    r0 = row0 // c_len
    kblk = (C_HEADS * C_KEY_DIM) // C_KEY_DIM
    vblk = (2 * C_HEADS * C_KEY_DIM) // C_VAL_DIM

    def chunk(b, s):
        return r0 + b * nc + jnp.where(s < nc, s, 2 * nc - 1 - s)

    in_specs = [
        pl.BlockSpec(memory_space=pltpu.SMEM),
        pl.BlockSpec((c_len, C_KEY_DIM), lambda b, h, s: (chunk(b, s), h)),
        pl.BlockSpec((c_len, C_KEY_DIM), lambda b, h, s: (chunk(b, s), kblk + h)),
        pl.BlockSpec((c_len, C_VAL_DIM), lambda b, h, s: (chunk(b, s), vblk + h)),
    ]
    args = [log_decay, proj, proj, proj]
    if s0 is not None:
        in_specs.append(pl.BlockSpec((None, 2, None, C_KEY_DIM, C_VAL_DIM), lambda b, h, s: (b, 0, h, 0, 0)))
        args.append(s0)
    return pl.pallas_call(
        functools.partial(_ret_kernel, nc=nc, has_s0=s0 is not None, cross=(s0 is not None) or nc > 1),
        grid=(nb, C_HEADS, 2 * nc),
        in_specs=in_specs,
        out_specs=[
            pl.BlockSpec((seq, C_VAL_DIM), lambda b, h, s: (b, h)),
            pl.BlockSpec((None, 2, None, C_KEY_DIM, C_VAL_DIM), lambda b, h, s: (b, 0, h, 0, 0)),
        ],
        out_shape=[
            jax.ShapeDtypeStruct((nb * seq, C_HEADS * C_VAL_DIM), F32),
            jax.ShapeDtypeStruct((nb, 2, C_HEADS, C_KEY_DIM, C_VAL_DIM), F32),
        ],
        scratch_shapes=[pltpu.VMEM((C_KEY_DIM, C_VAL_DIM), F32)],
        compiler_params=_cparams("parallel", "parallel", "arbitrary"),
        name="retention",
    )(*args)


def _ret_out_kernel(oc_ref, ol_ref, g_ref, gn_ref, w_ref, x_ref, gate_ref, o_ref, *, tm):
    o = _pick_group(oc_ref, ol_ref, tm)
    parts = []
    for h in range(C_HEADS):
        oh = o[:, h * C_VAL_DIM:(h + 1) * C_VAL_DIM]
        ms = jnp.mean(oh * oh, axis=-1, keepdims=True)
        parts.append((oh * lax.rsqrt(ms + NORM_EPS)) * gn_ref[:, h * C_VAL_DIM:(h + 1) * C_VAL_DIM])
    a = (_silu(g_ref[...]) * jnp.concatenate(parts, axis=1)).astype(BF16)
    o_ref[...] = x_ref[...] + gate_ref[...] * _bdot(a, w_ref[...])


def retention_out(o_ctx, o_lat, proj, gn_g, w, x, gate, *, tm=512):
    t, d = x.shape
    vd = C_HEADS * C_VAL_DIM
    gblk = proj.shape[1] // vd - 1
    seg = lambda i: (_seg_of_tile(i, tm), 0, 0)
    return pl.pallas_call(
        functools.partial(_ret_out_kernel, tm=tm),
        grid=(t // tm,),
        in_specs=_group_specs(o_ctx, o_lat, tm, vd) + [
            pl.BlockSpec((tm, vd), lambda i: (i, gblk)),
            pl.BlockSpec((1, vd), lambda i: (0, 0)),
            pl.BlockSpec((vd, d), lambda i: (0, 0)),
            pl.BlockSpec((tm, d), lambda i: (i, 0)),
            pl.BlockSpec((None, 1, d), seg),
        ],
        out_specs=pl.BlockSpec((tm, d), lambda i: (i, 0)),
        out_shape=jax.ShapeDtypeStruct((t, d), F32),
        compiler_params=_cparams("parallel"),
        name="retention_out",
    )(o_ctx, o_lat, proj, gn_g.reshape(1, vd), w, x, gate)


def _filter_rows(t, seq, band_ref, w1_ref, b1_ref, w2_ref, b2_ref, w3_ref, fr_ref, delta_ref):
    t_norm = t / max(seq - 1, 1)
    lane = lax.broadcasted_iota(jnp.int32, (1, 128), 1)
    ang = (2.0 * math.pi * t) * band_ref[...] / seq
    feat = jnp.where(lane == 0, t_norm,
                     jnp.where(lane <= D_BANDS, jnp.cos(ang), jnp.where(lane <= 2 * D_BANDS, -jnp.sin(ang), 0.0)))
    a = jnp.sin(fr_ref[0:1] * (_bdot(feat.astype(BF16), w1_ref[...].astype(BF16)) + b1_ref[...]))
    a = jnp.sin(fr_ref[1:2] * (_bdot(a.astype(BF16), w2_ref[...].astype(BF16)) + b2_ref[...]))
    f = _bdot(a.astype(BF16), w3_ref[...].astype(BF16))
    window = jnp.exp(-t_norm * delta_ref[...]) + D_MOD_SHIFT
    return f * jnp.concatenate([window] * 4, axis=1)


def _filter_sum_kernel(band_ref, w1_ref, b1_ref, w2_ref, b2_ref, w3_ref, fr_ref, delta_ref, o_ref, *, seq, tm):
    i = pl.program_id(0)
    t = (i * tm + lax.broadcasted_iota(jnp.int32, (tm, 1), 0)).astype(F32)
    f = _filter_rows(t, seq, band_ref, w1_ref, b1_ref, w2_ref, b2_ref, w3_ref, fr_ref, delta_ref)
    part = jnp.sum(jnp.abs(f), axis=0, keepdims=True)

    @pl.when(i == 0)
    def _():
        o_ref[...] = part

    @pl.when(i > 0)
    def _():
        o_ref[...] += part


def _filter_gen_kernel(band_ref, w1_ref, b1_ref, w2_ref, b2_ref, w3_ref, fr_ref, delta_ref, sum_ref, o_ref, *, seq, tm):
    i = pl.program_id(0)
    r = i * tm + lax.broadcasted_iota(jnp.int32, (tm, 1), 0)
    t = jnp.where(r < seq, r, 2 * seq - r).astype(F32)
    f = _filter_rows(t, seq, band_ref, w1_ref, b1_ref, w2_ref, b2_ref, w3_ref, fr_ref, delta_ref)
    f = f / (sum_ref[...] + 1e-6)
    half = 2 * D_MODEL
    fwd, bwd = f[:, :half], f[:, half:]
    o_ref[...] = jnp.where(r < seq, fwd, jnp.where(r > seq, bwd, 0.0)) + jnp.where(r == 0, bwd, 0.0)


def hyena_circular_kernel(seq, band, w1p, b1, w2, b2, w3, freq, delta, *, tm):
    consts = [band, w1p, b1, w2, b2, w3, freq, delta]
    cspecs = [pl.BlockSpec(a.shape, lambda i: (0, 0)) for a in consts]
    nf = w3.shape[1]
    sums = pl.pallas_call(
        functools.partial(_filter_sum_kernel, seq=seq, tm=tm),
        grid=(seq // tm,),
        in_specs=cspecs,
        out_specs=pl.BlockSpec((1, nf), lambda i: (0, 0)),
        out_shape=jax.ShapeDtypeStruct((1, nf), F32),
        compiler_params=_cparams("arbitrary"),
        name="hyena_filter_sum",
    )(*consts)
    return pl.pallas_call(
        functools.partial(_filter_gen_kernel, seq=seq, tm=tm),
        grid=(2 * seq // tm,),
        in_specs=cspecs + [pl.BlockSpec((1, nf), lambda i: (0, 0))],
        out_specs=pl.BlockSpec((tm, nf // 2), lambda i: (i, 0)),
        out_shape=jax.ShapeDtypeStruct((2 * seq, nf // 2), F32),
        compiler_params=_cparams("parallel"),
        name="hyena_filter_gen",
    )(*consts, sums)


def _dft_cs(rows, cols, n):
    m = np.outer(np.arange(rows), np.arange(cols)) % n
    ang = 2.0 * np.pi * m / n
    return np.cos(ang), np.sin(ang)


def _stack_fwd(c, s):
    return np.block([[c, s], [-s, c]])


def _stack_inv(c, s):
    return np.block([[c, -s], [s, c]])


def _dft_mats():
    as_bf16 = lambda a: jnp.asarray(a, F32).astype(BF16)
    c1, s1 = _dft_cs(LAT_N1, LAT_N1, LAT_N1)
    c2, s2 = _dft_cs(LAT_N2, LAT_N2, LAT_N2)
    cc, sc = _dft_cs(CTX_N, CTX_N, CTX_N)
    h1, hc = LAT_N1 // 2, CTX_N // 2
    return dict(
        lat_g1=as_bf16(_stack_fwd(c1[:, :h1], s1[:, :h1])),
        lat_g1r=as_bf16(np.concatenate([c1, -s1], axis=0)),
        lat_g2=as_bf16(_stack_fwd(c2, s2)),
        lat_g2i=as_bf16(_stack_inv(c2, s2)),
        lat_g1i=as_bf16(_stack_inv(c1[:h1], s1[:h1]) / LAT_N),
        ctx_g=as_bf16(_stack_fwd(cc[:, :hc], sc[:, :hc])),
        ctx_gr=as_bf16(np.concatenate([cc, -sc], axis=0)),
        ctx_gi=as_bf16(_stack_inv(cc[:hc], sc[:hc]) / CTX_N),
    )


def _lmul_kernel(g_ref, x_ref, o_ref):
    o_ref[...] = _bdot(g_ref[...], x_ref[...].astype(BF16))


def left_matmul(g, x, *, row_blk=0, tc):
    m, k = g.shape
    n = x.shape[1]
    return pl.pallas_call(
        _lmul_kernel,
        grid=(n // tc,),
        in_specs=[pl.BlockSpec((m, k), lambda j: (0, 0)), pl.BlockSpec((k, tc), lambda j: (row_blk, j))],
        out_specs=pl.BlockSpec((m, tc), lambda j: (0, j)),
        out_shape=jax.ShapeDtypeStruct((m, n), F32),
        compiler_params=_cparams("parallel"),
        name="left_matmul",
    )(g, x)


def _gated_skip(conv, gate, y, skip_ref):
    reps = conv.shape[1] // D_MODEL
    skip = skip_ref[...] if reps == 1 else jnp.concatenate([skip_ref[...]] * reps, axis=1)
    return gate * (conv + skip * y)


def _lat_last_kernel(g_ref, b_ref, gate_ref, y_ref, skip_ref, o_ref):
    conv = _bdot(g_ref[...], b_ref[...].astype(BF16))
    o_ref[...] = _gated_skip(conv, gate_ref[...], y_ref[...], skip_ref).astype(o_ref.dtype)


def lat_last_stage(g, b, gate, gate_blk, y, y_blk, skip, *, tc, out_dtype):
    m, k = g.shape
    n = b.shape[1]
    return pl.pallas_call(
        _lat_last_kernel,
        grid=(n // tc,),
        in_specs=[
            pl.BlockSpec((m, k), lambda j: (0, 0)),
            pl.BlockSpec((k, tc), lambda j: (0, j)),
            pl.BlockSpec((m, tc), lambda j: (gate_blk, j)),
            pl.BlockSpec((m, tc), lambda j: (y_blk, j)),
            pl.BlockSpec((1, D_MODEL), lambda j: (0, 0)),
        ],
        out_specs=pl.BlockSpec((m, tc), lambda j: (0, j)),
        out_shape=jax.ShapeDtypeStruct((m, n), out_dtype),
        compiler_params=_cparams("parallel"),
        name="hyena_lat_last",
    )(g, b, gate, y, skip)


def _lat_mid_kernel(a_ref, *rest, conv):
    if conv:
        h_ref, g2_ref, g2i_ref, o_ref = rest
    else:
        g2_ref, o_ref = rest
    k1 = pl.program_id(0)
    n2 = lax.broadcasted_iota(jnp.int32, (LAT_N2, 1), 0)
    ang = (k1 * n2).astype(F32) * (2.0 * math.pi / LAT_N)
    c, s = jnp.cos(ang), jnp.sin(ang)
    ar, ai = a_ref[0], a_ref[1]
    t = jnp.concatenate([ar * c + ai * s, ai * c - ar * s], axis=0).astype(BF16)
    x = _bdot(g2_ref[...], t)
    xr, xi = x[:LAT_N2], x[LAT_N2:]
    if not conv:
        o_ref[0] = xr
        o_ref[1] = xi
        return
    hr, hi = h_ref[0], h_ref[1]
    y = jnp.concatenate([xr * hr - xi * hi, xr * hi + xi * hr], axis=0).astype(BF16)
    b = _bdot(g2i_ref[...], y)
    br, bi = b[:LAT_N2], b[LAT_N2:]
    o_ref[0] = br * c - bi * s
    o_ref[1] = bi * c + br * s


def lat_mid_stage(a, mats, h=None, order=0):
    w = a.shape[-1]
    blk = lambda width, col: pl.BlockSpec((2, None, LAT_N2, width), lambda k1, j: (0, k1, 0, col(j)))
    gspec = pl.BlockSpec((2 * LAT_N2, 2 * LAT_N2), lambda k1, j: (0, 0))
    if h is None:
        in_specs = [blk(D_MODEL, lambda j: j), gspec]
        args = [a, mats["lat_g2"]]
    else:
        in_specs = [blk(D_MODEL, lambda j: j), blk(D_MODEL, lambda j: order), gspec, gspec]
        args = [a, h, mats["lat_g2"], mats["lat_g2i"]]
    return pl.pallas_call(
        functools.partial(_lat_mid_kernel, conv=h is not None),
        grid=(LAT_N1, w // D_MODEL),
        in_specs=in_specs,
        out_specs=blk(D_MODEL, lambda j: j),
        out_shape=jax.ShapeDtypeStruct(a.shape, F32),
        compiler_params=_cparams("parallel", "arbitrary"),
        name="hyena_lat_mid",
    )(*args)


def _ctx_conv_kernel(y_ref, gate_ref, h_ref, g_ref, gi_ref, skip_ref, o_ref):
    n = CTX_N
    y = y_ref[...]
    z = _bdot(g_ref[...], y.astype(BF16))
    zr, zi = z[:n], z[n:]
    hr, hi = h_ref[0:n], h_ref[n:]
    w = jnp.concatenate([zr * hr - zi * hi, zr * hi + zi * hr], axis=0).astype(BF16)
    conv = _bdot(gi_ref[...], w)
    o_ref[...] = _gated_skip(conv, gate_ref[...], y, skip_ref).astype(o_ref.dtype)


def ctx_conv(y, y_plane, gate, gate_plane, h, order, mats, skip, *, out_dtype):
    rows = 2 * SEQ
    return pl.pallas_call(
        _ctx_conv_kernel,
        grid=(BATCH // 2,),
        in_specs=[
            pl.BlockSpec((None, rows, D_MODEL), lambda p: (y_plane, p, 0)),
            pl.BlockSpec((None, rows, D_MODEL), lambda p: (gate_plane, p, 0)),
            pl.BlockSpec((2 * CTX_N, D_MODEL), lambda p: (0, order)),
            pl.BlockSpec((2 * CTX_N, rows), lambda p: (0, 0)),
            pl.BlockSpec((rows, 2 * CTX_N), lambda p: (0, 0)),
            pl.BlockSpec((1, D_MODEL), lambda p: (0, 0)),
        ],
        out_specs=pl.BlockSpec((rows, D_MODEL), lambda p: (p, 0)),
        out_shape=jax.ShapeDtypeStruct((T_CTX, D_MODEL), out_dtype),
        compiler_params=_cparams("parallel"),
        name="hyena_ctx_conv",
    )(y, gate, h, mats["ctx_g"], mats["ctx_gi"], skip)


def hyena_core(z3, f_w1, f_b1, f_w2, f_b2, f_w3, f_freq, f_skip):
    mats = _dft_mats()
    bands = jnp.linspace(1e-4, D_BANDS - 1, D_BANDS, dtype=F32)
    band = jnp.zeros((1, 128), F32).at[0, 1:1 + D_BANDS].set(bands).at[0, 1 + D_BANDS:1 + 2 * D_BANDS].set(bands)
    w1p = jnp.zeros((128, D_FILTER_HIDDEN), F32).at[:D_EMB].set(f_w1)
    max_decay = math.log(D_DECAY_TARGET) / D_FAST_DECAY_PCT
    min_decay = math.log(D_DECAY_TARGET) / D_SLOW_DECAY_PCT
    delta = jnp.abs(jnp.linspace(min_decay, max_decay, D_MODEL, dtype=F32)).reshape(1, D_MODEL)
    fargs = (band, w1p, f_b1.reshape(1, -1), f_w2, f_b2.reshape(1, -1), f_w3, f_freq, delta)

    kc_ctx = hyena_circular_kernel(SEQ, *fargs, tm=SEQ)
    h_ctx = left_matmul(mats["ctx_gr"], kc_ctx, tc=D_MODEL)
    kc_lat = hyena_circular_kernel(DEC_SEQ, *fargs, tm=512)
    wide = LAT_N2 * 2 * D_MODEL
    a = left_matmul(mats["lat_g1r"], kc_lat.reshape(LAT_N1, wide), tc=8192)
    h_lat = lat_mid_stage(a.reshape(2, LAT_N1, LAT_N2, 2 * D_MODEL), mats)

    y1 = ctx_conv(z3, 2, z3, 0, h_ctx, 0, mats, f_skip[0:1], out_dtype=F32)
    y_ctx = ctx_conv(y1[None], 0, z3, 1, h_ctx, 1, mats, f_skip[1:2], out_dtype=BF16)

    cols = LAT_N2 * D_MODEL
    rows = DEC_BATCH * LAT_N1 // 2
    zw = z3.reshape(3 * T_ALL * D_MODEL // cols, cols)
    plane_blks = T_ALL * D_MODEL // cols // rows
    lat_blk = lambda plane: plane * plane_blks + T_CTX * D_MODEL // cols // rows
    y, y_blk = zw, lat_blk(2)
    for n in range(2):
        a = left_matmul(mats["lat_g1"], y, row_blk=y_blk, tc=8192)
        b = lat_mid_stage(a.reshape(2, LAT_N1, LAT_N2, D_MODEL), mats, h_lat, n)
        y = lat_last_stage(mats["lat_g1i"], b.reshape(2 * LAT_N1, cols), zw, lat_blk(n), y, y_blk, f_skip[n:n + 1],
                           tc=8192, out_dtype=F32 if n == 0 else BF16)
        y_blk = 0
    return y_ctx, y.reshape(T_LAT, D_MODEL)


def kernel(x_prompt, x_sample, cache_attn_k, cache_attn_v, cache_diff_k, cache_diff_v, state_ret, c, c_ctx, w_mod, b_mod, norm1_g, norm2_g, final_g, attn_w_qkv, attn_q_g, attn_k_g, attn_w_o, diff_w_qkv, diff_lambda, diff_subln_g, diff_w_o, ret_w_in, ret_log_decay, ret_gn_g, ret_w_o, hyena_w_in, hyena_sc_w, hyena_sc_b, hyena_f_w1, hyena_f_b1, hyena_f_w2, hyena_f_b2, hyena_f_w3, hyena_f_freq, hyena_f_skip, hyena_w_o, ffn_w_up, ffn_conv_w, ffn_conv_b, ffn_w_down):
    d = D_MODEL
    x = jnp.concatenate([x_prompt.reshape(T_CTX, d), x_sample.reshape(T_LAT, d)], axis=0)
    cond = jnp.zeros((SEG_ROWS, d), F32).at[0].set(c_ctx).at[1:N_SEG].set(c)
    mod = modulation(cond, w_mod, b_mod)
    mod = mod.reshape(DEPTH, SEG_ROWS, 6, d).transpose(0, 2, 1, 3).reshape(DEPTH, 6, SEG_ROWS, 1, d)
    rope = rope_tables()
    bf = lambda w: w.astype(BF16)
    out = {}

    for l in range(DEPTH):
        m, j = l % 4, l // 4
        sh1, sc1, g1, sh2, sc2, g2 = (mod[l, i] for i in range(6))
        if m == 0:
            nq, nk = A_HEADS * A_HEAD_DIM, A_KV_HEADS * A_HEAD_DIM
            scale = A_HEAD_DIM ** -0.5
            qkv = norm_proj(x, norm1_g[l], sh1, sc1, bf(attn_w_qkv[j]))
            q_c, k_c, kv_c = qk_prep(qkv, row0=0, rows=T_CTX, nq=nq, nk=nk, q_g=attn_q_g[j], k_g=attn_k_g[j],
                                     scale=scale)
            q_l, _, kv_l = qk_prep(qkv, row0=T_CTX, rows=T_LAT, nq=nq, nk=nk, q_g=attn_q_g[j], k_g=attn_k_g[j],
                                   rope=rope, scale=scale)
            out["attn_k"] = k_c.reshape(BATCH, 1, SEQ, A_KV_HEADS, A_HEAD_DIM)
            out["attn_v"] = qkv[:T_CTX, nq + nk:].reshape(BATCH, 1, SEQ, A_KV_HEADS, A_HEAD_DIM)
            cache = jnp.concatenate([cache_attn_k[:, j].reshape(DEC_BATCH, PAST_LEN, nk),
                                     cache_attn_v[:, j].reshape(DEC_BATCH, PAST_LEN, nk)], axis=-1).astype(BF16)
            lk = PAST_LEN + DEC_SEQ
            kv_all = jnp.concatenate([cache, kv_l.reshape(DEC_BATCH, DEC_SEQ, 2 * nk)], axis=1).reshape(DEC_BATCH * lk, 2 * nk)
            o_c = gqa_attention(q_c, kv_c, nb=BATCH, lq=SEQ, lk=SEQ, tq=SEQ)
            o_l = gqa_attention(q_l, kv_all, nb=DEC_BATCH, lq=DEC_SEQ, lk=lk, tq=128)
            x = out_proj_residual(o_c, o_l, bf(attn_w_o[j]), x, g1)
        elif m == 1:
            nq = nk = B_HEADS * 2 * B_HEAD_DIM
            scale = B_HEAD_DIM ** -0.5
            lam_init = 0.8 - 0.6 * math.exp(-0.3 * l)
            qkv = norm_proj(x, norm1_g[l], sh1, sc1, bf(diff_w_qkv[j]))
            q_c, k_c, kv_c = qk_prep(qkv, row0=0, rows=T_CTX, nq=nq, nk=nk, scale=scale)
            q_l, _, kv_l = qk_prep(qkv, row0=T_CTX, rows=T_LAT, nq=nq, nk=nk, rope=rope, scale=scale)
            out["diff_k"] = k_c.reshape(BATCH, 1, SEQ, B_HEADS, 2 * B_HEAD_DIM)
            out["diff_v"] = qkv[:T_CTX, nq + nk:].reshape(BATCH, 1, SEQ, B_HEADS, 2 * B_HEAD_DIM)
            cache = jnp.concatenate([cache_diff_k[:, j].reshape(DEC_BATCH, PAST_LEN, nk),
                                     cache_diff_v[:, j].reshape(DEC_BATCH, PAST_LEN, nk)], axis=-1).astype(BF16)
            lk = PAST_LEN + DEC_SEQ
            kv_all = jnp.concatenate([cache, kv_l.reshape(DEC_BATCH, DEC_SEQ, 2 * nk)], axis=1).reshape(DEC_BATCH * lk, 2 * nk)
            dargs = (diff_lambda[j], diff_subln_g[j])
            o_c = diff_attention(q_c, kv_c, *dargs, nb=BATCH, lq=SEQ, lk=SEQ, tq=SEQ, lam_init=lam_init)
            o_l = diff_attention(q_l, kv_all, *dargs, nb=DEC_BATCH, lq=DEC_SEQ, lk=lk, tq=256, lam_init=lam_init)
            x = out_proj_residual(o_c, o_l, bf(diff_w_o[j]), x, g1)
        elif m == 2:
            proj = norm_proj(x, norm1_g[l], sh1, sc1, bf(ret_w_in[j]))
            o_c, st = retention(proj, ret_log_decay[j], None, row0=0, nb=BATCH, seq=SEQ)
            o_l, _ = retention(proj, ret_log_decay[j], state_ret[:, j], row0=T_CTX, nb=DEC_BATCH, seq=DEC_SEQ)
            out["ret_s"] = st.reshape(BATCH, 1, 2, C_HEADS, C_KEY_DIM, C_VAL_DIM)
            x = retention_out(o_c, o_l, proj, ret_gn_g[j], bf(ret_w_o[j]), x, g1)
        else:
            z3 = norm_proj_conv(x, norm1_g[l], sh1, sc1, bf(hyena_w_in[j]), hyena_sc_w[j], hyena_sc_b[j])
            y_c, y_l = hyena_core(z3, hyena_f_w1[j], hyena_f_b1[j], hyena_f_w2[j], hyena_f_b2[j], hyena_f_w3[j],
                                  hyena_f_freq[j], hyena_f_skip[j])
            x = out_proj_residual(y_c, y_l, bf(hyena_w_o[j]), x, g1)
        x = conv_ffn(x, norm2_g[l], sh2, sc2, g2, bf(ffn_w_up[l]), ffn_conv_w[l], ffn_conv_b[l], bf(ffn_w_down[l]),
                     final_g if l == DEPTH - 1 else None)

    y_prompt = x[:T_CTX].reshape(BATCH, SEQ, d)
    y_sample = x[T_CTX:].reshape(DEC_BATCH, DEC_SEQ, d)
    return (y_prompt, y_sample, out["attn_k"], out["attn_v"], out["diff_k"], out["diff_v"], out["ret_s"])
```

```python
import functools
import math

import jax
import jax.numpy as jnp
import numpy as np
from jax import lax
from jax.experimental import pallas as pl
from jax.experimental.pallas import tpu as pltpu

F32 = jnp.float32
BF16 = jnp.bfloat16

D_MODEL = 1024
BATCH = 32
SEQ = 256
DEPTH = 4
DEC_BATCH = 2
DEC_SEQ = 4096
PAST_LEN = 256
GRID_W = 64
ROPE_THETA = 10000.0
NORM_EPS = 1e-6
A_HEADS = 16
A_KV_HEADS = 4
A_HEAD_DIM = 64
A_GROUP = A_HEADS // A_KV_HEADS
B_HEADS = 8
B_HEAD_DIM = 64
B_SUBLN_EPS = 1e-5
C_HEADS = 4
C_KEY_DIM = 256
C_VAL_DIM = 512
D_BANDS = 16
D_EMB = 1 + 2 * D_BANDS
D_FILTER_HIDDEN = 64
D_FAST_DECAY_PCT = 0.3
D_SLOW_DECAY_PCT = 1.5
D_DECAY_TARGET = 1e-2
D_MOD_SHIFT = 0.05
FFN_DIM = 2816

T_CTX = BATCH * SEQ
T_LAT = DEC_BATCH * DEC_SEQ
T_ALL = T_CTX + T_LAT
N_SEG = 1 + DEC_BATCH
SEG_ROWS = 8

HALO = 16
RET_CHUNK = 256
VMEM_LIMIT = 56 * 1024 * 1024

LAT_N = 2 * DEC_SEQ
LAT_N1 = 64
LAT_N2 = LAT_N // LAT_N1
CTX_N = 2 * SEQ


def _cparams(*sem):
    return pltpu.CompilerParams(dimension_semantics=sem, vmem_limit_bytes=VMEM_LIMIT)


def _seg_of_tile(i, tm):
    start = i * tm
    return jnp.where(start < T_CTX, 0, 1 + (start - T_CTX) // DEC_SEQ)


def _silu(x):
    return x * jax.nn.sigmoid(x)


def _bdot(a, b):
    return jnp.dot(a, b, preferred_element_type=F32)


def _norm_mod(x, g, sh, sc):
    ms = jnp.mean(x * x, axis=-1, keepdims=True)
    y = (x * lax.rsqrt(ms + NORM_EPS)) * g
    return y * (1.0 + sc) + sh


def _mod_kernel(c_ref, w_ref, b_ref, o_ref):
    s = _silu(c_ref[...]).astype(BF16)
    o_ref[...] = _bdot(s, w_ref[...].astype(BF16)) + b_ref[...]


def modulation(cond, w_mod, b_mod):
    tn = 1536
    n = w_mod.shape[-1]
    return pl.pallas_call(
        _mod_kernel,
        grid=(DEPTH, n // tn),
        in_specs=[
            pl.BlockSpec((SEG_ROWS, D_MODEL), lambda l, j: (0, 0)),
            pl.BlockSpec((None, D_MODEL, tn), lambda l, j: (l, 0, j)),
            pl.BlockSpec((None, 1, tn), lambda l, j: (l, 0, j)),
        ],
        out_specs=pl.BlockSpec((None, SEG_ROWS, tn), lambda l, j: (l, 0, j)),
        out_shape=jax.ShapeDtypeStruct((DEPTH, SEG_ROWS, n), F32),
        compiler_params=_cparams("arbitrary", "arbitrary"),
        name="modulation",
    )(cond, w_mod, b_mod.reshape(DEPTH, 1, n))


def _proj_kernel(x_ref, g_ref, sh_ref, sc_ref, w_ref, o_ref, h_ref):
    @pl.when(pl.program_id(1) == 0)
    def _():
        h_ref[...] = _norm_mod(x_ref[...], g_ref[...], sh_ref[...], sc_ref[...]).astype(BF16)

    o_ref[...] = _bdot(h_ref[...], w_ref[...]).astype(o_ref.dtype)


def norm_proj(x, g, sh, sc, w, *, tm=1024, tn=1536, out_dtype=F32):
    t, d = x.shape
    n = w.shape[1]
    seg = lambda i, j: (_seg_of_tile(i, tm), 0, 0)
    return pl.pallas_call(
        _proj_kernel,
        grid=(t // tm, n // tn),
        in_specs=[
            pl.BlockSpec((tm, d), lambda i, j: (i, 0)),
            pl.BlockSpec((1, d), lambda i, j: (0, 0)),
            pl.BlockSpec((None, 1, d), seg),
            pl.BlockSpec((None, 1, d), seg),
            pl.BlockSpec((d, tn), lambda i, j: (0, j)),
        ],
        out_specs=pl.BlockSpec((tm, tn), lambda i, j: (i, j)),
        out_shape=jax.ShapeDtypeStruct((t, n), out_dtype),
        scratch_shapes=[pltpu.VMEM((tm, d), BF16)],
        compiler_params=_cparams("parallel", "arbitrary"),
        name="norm_proj",
    )(x, g.reshape(1, d), sh, sc, w)


def _conv3(u, cw, cb, i, tm):
    rows = u.shape[0]
    up = pltpu.roll(u, 1, 0)[HALO:HALO + tm]
    uc = u[HALO:HALO + tm]
    un = pltpu.roll(u, rows - 1, 0)[HALO:HALO + tm]
    sub = lax.broadcasted_iota(jnp.int32, (8, 1), 0)
    is_ctx = i * tm < T_CTX
    ups, uns = [], []
    for r in range(0, tm, SEQ):
        start = i * tm + r
        first = jnp.logical_or(is_ctx, (start & (DEC_SEQ - 1)) == 0)
        last = jnp.logical_or(is_ctx, ((start + SEQ) & (DEC_SEQ - 1)) == 0)
        ups += [jnp.where(jnp.logical_and(sub == 0, first), 0.0, up[r:r + 8]), up[r + 8:r + SEQ]]
        uns += [un[r:r + SEQ - 8], jnp.where(jnp.logical_and(sub == 7, last), 0.0, un[r + SEQ - 8:r + SEQ])]
    up = jnp.concatenate(ups, axis=0)
    un = jnp.concatenate(uns, axis=0)
    return up * cw[0:1] + uc * cw[1:2] + un * cw[2:3] + cb


def _fill_h(h_ref, xp_ref, x_ref, xn_ref, g_ref, sh_ref, sc_ref, tm):
    g, sh, sc = g_ref[...], sh_ref[...], sc_ref[...]
    h_ref[0:HALO] = _norm_mod(xp_ref[...], g, sh, sc).astype(BF16)
    h_ref[HALO:HALO + tm] = _norm_mod(x_ref[...], g, sh, sc).astype(BF16)
    h_ref[HALO + tm:] = _norm_mod(xn_ref[...], g, sh, sc).astype(BF16)


def _proj_conv_kernel(xp_ref, x_ref, xn_ref, g_ref, sh_ref, sc_ref, w_ref, cw_ref, cb_ref, o_ref, h_ref, *, tm):
    i = pl.program_id(0)

    @pl.when(pl.program_id(1) == 0)
    def _():
        _fill_h(h_ref, xp_ref, x_ref, xn_ref, g_ref, sh_ref, sc_ref, tm)

    u = _bdot(h_ref[...], w_ref[...])
    o_ref[...] = _conv3(u, cw_ref[...], cb_ref[...], i, tm)


def _ffn_kernel(xp_ref, x_ref, xn_ref, g_ref, sh_ref, sc_ref, gate_ref, wu_ref, cw_ref, cb_ref, wd_ref, *rest, tm, tn,
                final):
    if final:
        fg_ref, o_ref = rest
    else:
        (o_ref,) = rest
    i = pl.program_id(0)
    g, sh, sc = g_ref[...], sh_ref[...], sc_ref[...]
    x = x_ref[...]
    h = jnp.concatenate([_norm_mod(xp_ref[...], g, sh, sc).astype(BF16), _norm_mod(x, g, sh, sc).astype(BF16),
                         _norm_mod(xn_ref[...], g, sh, sc).astype(BF16)], axis=0)
    f = wd_ref.shape[0]
    acts = []
    for c in range(0, f, tn):
        a = _conv3(_bdot(h, wu_ref[:, c:c + tn]), cw_ref[:, c:c + tn], cb_ref[:, c:c + tn], i, tm)
        b = _conv3(_bdot(h, wu_ref[:, f + c:f + c + tn]), cw_ref[:, f + c:f + c + tn], cb_ref[:, f + c:f + c + tn], i, tm)
        acts.append((_silu(a) * b).astype(BF16))
    y = x + gate_ref[...] * _bdot(jnp.concatenate(acts, axis=1), wd_ref[...])
    if final:
        ms = jnp.mean(y * y, axis=-1, keepdims=True)
        y = (y * lax.rsqrt(ms + NORM_EPS)) * fg_ref[...]
    o_ref[...] = y


def _halo_specs(t, tm, d):
    per = tm // HALO
    last_blk = t // HALO - 1
    return [
        pl.BlockSpec((HALO, d), lambda i, *_: (jnp.maximum(i * per - 1, 0), 0)),
        pl.BlockSpec((tm, d), lambda i, *_: (i, 0)),
        pl.BlockSpec((HALO, d), lambda i, *_: (jnp.minimum((i + 1) * per, last_blk), 0)),
    ]


def norm_proj_conv(x, g, sh, sc, w, cw, cb, *, tm=1024, tn=512):
    t, d = x.shape
    n = w.shape[1]
    per_group = d // tn
    seg = lambda i, j: (_seg_of_tile(i, tm), 0, 0)
    return pl.pallas_call(
        functools.partial(_proj_conv_kernel, tm=tm),
        grid=(t // tm, n // tn),
        in_specs=_halo_specs(t, tm, d) + [
            pl.BlockSpec((1, d), lambda i, j: (0, 0)),
            pl.BlockSpec((None, 1, d), seg),
            pl.BlockSpec((None, 1, d), seg),
            pl.BlockSpec((d, tn), lambda i, j: (0, j)),
            pl.BlockSpec((3, tn), lambda i, j: (0, j)),
            pl.BlockSpec((1, tn), lambda i, j: (0, j)),
        ],
        out_specs=pl.BlockSpec((None, tm, tn), lambda i, j: (j // per_group, i, j % per_group)),
        out_shape=jax.ShapeDtypeStruct((n // d, t, d), F32),
        scratch_shapes=[pltpu.VMEM((tm + 2 * HALO, d), BF16)],
        compiler_params=_cparams("parallel", "arbitrary"),
        name="norm_proj_conv",
    )(x, x, x, g.reshape(1, d), sh, sc, w, cw, cb.reshape(1, n))


def conv_ffn(x, g, sh, sc, gate, w_up, cw, cb, w_down, final_g=None, *, tm=512, tn=256):
    t, d = x.shape
    f = w_down.shape[0]
    seg = lambda i: (_seg_of_tile(i, tm), 0, 0)
    whole = lambda a: pl.BlockSpec(a.shape, lambda i: (0, 0), pipeline_mode=pl.Buffered(1))
    cb = cb.reshape(1, 2 * f)
    in_specs = _halo_specs(t, tm, d) + [
        pl.BlockSpec((1, d), lambda i: (0, 0)),
        pl.BlockSpec((None, 1, d), seg),
        pl.BlockSpec((None, 1, d), seg),
        pl.BlockSpec((None, 1, d), seg),
        whole(w_up), whole(cw), whole(cb), whole(w_down),
    ]
    args = [x, x, x, g.reshape(1, d), sh, sc, gate, w_up, cw, cb, w_down]
    if final_g is not None:
        in_specs.append(pl.BlockSpec((1, d), lambda i: (0, 0)))
        args.append(final_g.reshape(1, d))
    return pl.pallas_call(
        functools.partial(_ffn_kernel, tm=tm, tn=tn, final=final_g is not None),
        grid=(t // tm,),
        in_specs=in_specs,
        out_specs=pl.BlockSpec((tm, d), lambda i: (i, 0)),
        out_shape=jax.ShapeDtypeStruct((t, d), F32),
        compiler_params=_cparams("parallel"),
        name="conv_ffn",
    )(*args)


def _group_specs(a_ctx, a_lat, tm, width):
    nctx = T_CTX // tm
    off = nctx if a_lat.shape[0] == T_ALL else 0
    return [
        pl.BlockSpec((tm, width), lambda i: (jnp.minimum(i, nctx - 1), 0)),
        pl.BlockSpec((tm, width), lambda i: (off + jnp.maximum(i - nctx, 0), 0)),
    ]


def _pick_group(ac_ref, al_ref, tm):
    return jnp.where(pl.program_id(0) < T_CTX // tm, ac_ref[...], al_ref[...])


def _out_proj_kernel(ac_ref, al_ref, w_ref, x_ref, gate_ref, o_ref, *, tm):
    a = _pick_group(ac_ref, al_ref, tm).astype(BF16)
    o_ref[...] = x_ref[...] + gate_ref[...] * _bdot(a, w_ref[...])


def out_proj_residual(a_ctx, a_lat, w, x, gate, *, tm=512):
    t, d = x.shape
    k = w.shape[0]
    seg = lambda i: (_seg_of_tile(i, tm), 0, 0)
    return pl.pallas_call(
        functools.partial(_out_proj_kernel, tm=tm),
        grid=(t // tm,),
        in_specs=_group_specs(a_ctx, a_lat, tm, k) + [
            pl.BlockSpec((k, d), lambda i: (0, 0)),
            pl.BlockSpec((tm, d), lambda i: (i, 0)),
            pl.BlockSpec((None, 1, d), seg),
        ],
        out_specs=pl.BlockSpec((tm, d), lambda i: (i, 0)),
        out_shape=jax.ShapeDtypeStruct((t, d), F32),
        compiler_params=_cparams("parallel"),
        name="out_proj_residual",
    )(a_ctx, a_lat, w, x, gate)


def _head_mean_sq(x, head_dim):
    n = x.shape[1]
    x2 = x * x
    hi = x2.astype(BF16)
    lo = (x2 - hi.astype(F32)).astype(BF16)
    blk = 256
    r = lax.broadcasted_iota(jnp.int32, (blk, blk), 0) // head_dim
    c = lax.broadcasted_iota(jnp.int32, (blk, blk), 1) // head_dim
    ones = (r == c).astype(BF16)
    parts = []
    for s in range(0, n, blk):
        parts.append(_bdot(hi[:, s:s + blk], ones) + _bdot(lo[:, s:s + blk], ones))
    ss = parts[0] if len(parts) == 1 else jnp.concatenate(parts, axis=1)
    return ss * (1.0 / head_dim)


def _rope(x, cos, sin):
    n = x.shape[1]
    lane = lax.broadcasted_iota(jnp.int32, (1, 128), 1)
    lower = (lane & 31) < 16
    outs = []
    for s in range(0, n, 128):
        xs = x[:, s:s + 128]
        partner = jnp.where(lower, pltpu.roll(xs, 128 - 16, 1), pltpu.roll(xs, 16, 1))
        outs.append(xs * cos + partner * sin)
    return jnp.concatenate(outs, axis=1)


def _qk_prep_kernel(*refs, nq, nk, norm, rope, scale):
    it = iter(refs)
    q_ref, k_ref, v_ref = next(it), next(it), next(it)
    if norm:
        qg_ref, kg_ref = next(it), next(it)
    if rope:
        cos_ref, sin_ref = next(it), next(it)
    qo_ref, ko_ref, kvo_ref = next(it), next(it), next(it)
    q = q_ref[...]
    k = k_ref[...]
    if norm:
        q = (q * lax.rsqrt(_head_mean_sq(q, A_HEAD_DIM) + NORM_EPS)) * qg_ref[...]
        k = (k * lax.rsqrt(_head_mean_sq(k, A_HEAD_DIM) + NORM_EPS)) * kg_ref[...]
    if rope:
        cos, sin = cos_ref[...], sin_ref[...]
        q = _rope(q, cos, sin)
        k = _rope(k, cos, sin)
    qo_ref[...] = (q * scale).astype(BF16)
    ko_ref[...] = k
    kvo_ref[:, 0:nk] = k.astype(BF16)
    kvo_ref[:, nk:] = v_ref[...].astype(BF16)


def qk_prep(qkv, *, row0, rows, nq, nk, q_g=None, k_g=None, rope=None, scale, tm=512):
    norm = q_g is not None
    r0 = row0 // tm
    qb = nq // nk
    in_specs = [
        pl.BlockSpec((tm, nq), lambda i: (i + r0, 0)),
        pl.BlockSpec((tm, nk), lambda i: (i + r0, qb)),
        pl.BlockSpec((tm, nk), lambda i: (i + r0, qb + 1)),
    ]
    args = [qkv, qkv, qkv]
    if norm:
        in_specs += [pl.BlockSpec((1, nq), lambda i: (0, 0)), pl.BlockSpec((1, nk), lambda i: (0, 0))]
        args += [jnp.tile(q_g, nq // q_g.shape[0]).reshape(1, nq), jnp.tile(k_g, nk // k_g.shape[0]).reshape(1, nk)]
    if rope is not None:
        per = DEC_SEQ // tm
        in_specs += [pl.BlockSpec((tm, 128), lambda i: (i % per, 0))] * 2
        args += list(rope)
    return pl.pallas_call(
        functools.partial(_qk_prep_kernel, nq=nq, nk=nk, norm=norm, rope=rope is not None, scale=scale),
        grid=(rows // tm,),
        in_specs=in_specs,
        out_specs=[
            pl.BlockSpec((tm, nq), lambda i: (i, 0)),
            pl.BlockSpec((tm, nk), lambda i: (i, 0)),
            pl.BlockSpec((tm, 2 * nk), lambda i: (i, 0)),
        ],
        out_shape=[
            jax.ShapeDtypeStruct((rows, nq), BF16),
            jax.ShapeDtypeStruct((rows, nk), F32),
            jax.ShapeDtypeStruct((rows, 2 * nk), BF16),
        ],
        compiler_params=_cparams("parallel"),
        name="qk_prep",
    )(*args)


def rope_tables():
    t = jnp.arange(DEC_SEQ)
    row = (t // GRID_W).astype(F32)
    col = (t % GRID_W).astype(F32)
    half = A_HEAD_DIM // 4
    inv_freq = ROPE_THETA ** (-jnp.arange(half, dtype=F32) / half)
    ar = row[:, None] * inv_freq[None, :]
    ac = col[:, None] * inv_freq[None, :]
    cos = jnp.concatenate([jnp.cos(ar), jnp.cos(ar), jnp.cos(ac), jnp.cos(ac)], axis=1)
    sin = jnp.concatenate([-jnp.sin(ar), jnp.sin(ar), -jnp.sin(ac), jnp.sin(ac)], axis=1)
    return jnp.tile(cos, (1, 2)), jnp.tile(sin, (1, 2))


LOG2E = math.log2(math.e)


def _transpose_bf16(x):
    return x.astype(F32).T.astype(BF16)


ATT_TK = 256


def _fill_vt(vt_ref, kv_ref, kw):
    @pl.when(pl.program_id(1) == 0)
    def _():
        vt_ref[...] = _transpose_bf16(kv_ref[:, kw:])


def _attend(n, lk, k_piece, qts, vt_piece):
    out = []
    prev_sts = prev_m = None
    for s in range(n + 1):
        sts, m, l, acc = [], None, None, None
        for r0 in range(0, lk, ATT_TK):
            if s < n:
                st = _bdot(k_piece(s, r0), qts[s])
                m_c = jnp.max(st, axis=0, keepdims=True)
                m = m_c if m is None else jnp.maximum(m, m_c)
                sts.append(st)
            if s > 0:
                p = jnp.exp2(prev_sts[r0 // ATT_TK] - prev_m)
                l_c = jnp.sum(p, axis=0, keepdims=True)
                pv = _bdot(vt_piece(s - 1, r0), p.astype(BF16))
                l = l_c if l is None else l + l_c
                acc = pv if acc is None else acc + pv
        if s > 0:
            out.append((acc, l))
        prev_sts, prev_m = sts, m
    return out


def _gqa_kernel(q_ref, kv_ref, o_ref, vt_ref, *, tq):
    d = A_HEAD_DIM
    kvw = A_KV_HEADS * d
    _fill_vt(vt_ref, kv_ref, kvw)
    qt = _transpose_bf16(q_ref[...])
    qts = [jnp.concatenate([qt[(kh * A_GROUP + g) * d:(kh * A_GROUP + g + 1) * d] for g in range(A_GROUP)], axis=1)
           for kh in range(A_KV_HEADS)]

    k_piece = lambda kh, r0: kv_ref[r0:r0 + ATT_TK, kh * d:(kh + 1) * d]
    vt_piece = lambda kh, r0: vt_ref[kh * d:(kh + 1) * d, r0:r0 + ATT_TK]
    blocks = []
    for acc, l in _attend(A_KV_HEADS, kv_ref.shape[0], k_piece, qts, vt_piece):
        ot = acc / l
        blocks += [ot[:, g * tq:(g + 1) * tq] for g in range(A_GROUP)]
    o_ref[...] = jnp.concatenate(blocks, axis=0).T.astype(o_ref.dtype)


def gqa_attention(q, kv, *, nb, lq, lk, tq):
    nq = lq // tq
    kvw = kv.shape[1] // 2
    return pl.pallas_call(
        functools.partial(_gqa_kernel, tq=tq),
        grid=(nb, nq),
        in_specs=[
            pl.BlockSpec((tq, q.shape[1]), lambda b, i: (b * nq + i, 0)),
            pl.BlockSpec((lk, kv.shape[1]), lambda b, i: (b, 0), pipeline_mode=pl.Buffered(1)),
        ],
        out_specs=pl.BlockSpec((tq, q.shape[1]), lambda b, i: (b * nq + i, 0)),
        out_shape=jax.ShapeDtypeStruct(q.shape, BF16),
        scratch_shapes=[pltpu.VMEM((kvw, lk), BF16)],
        compiler_params=_cparams("parallel", "arbitrary"),
        name="gqa_attention",
    )(q, kv)


def _diff_kernel(q_ref, kv_ref, lam_ref, sg_ref, o_ref, vt_ref, *, lam_init):
    d = B_HEAD_DIM
    kw = B_HEADS * 2 * d
    _fill_vt(vt_ref, kv_ref, kw)
    lf = lam_ref[...]
    lam = (jnp.exp(jnp.sum(lf[0:1] * lf[1:2], axis=-1, keepdims=True))
           - jnp.exp(jnp.sum(lf[2:3] * lf[3:4], axis=-1, keepdims=True)) + lam_init)
    qt = _transpose_bf16(q_ref[...])

    qts = [qt[r * d:(r + 1) * d] for r in range(2 * B_HEADS)]
    k_piece = lambda r, r0: kv_ref[r0:r0 + ATT_TK, r * d:(r + 1) * d]
    vt_piece = lambda r, r0: vt_ref[(r // 2) * 2 * d:(r // 2 + 1) * 2 * d, r0:r0 + ATT_TK]
    res = _attend(2 * B_HEADS, kv_ref.shape[0], k_piece, qts, vt_piece)
    blocks = []
    for h in range(B_HEADS):
        (o1, l1), (o2, l2) = res[2 * h], res[2 * h + 1]
        ot = o1 * (1.0 / l1) - o2 * (lam / l2)
        ms = jnp.mean(ot * ot, axis=0, keepdims=True)
        blocks.append(((ot * lax.rsqrt(ms + B_SUBLN_EPS)) * sg_ref[...]) * (1.0 - lam_init))
    o_ref[...] = jnp.concatenate(blocks, axis=0).T.astype(o_ref.dtype)


def diff_attention(q, kv, lam, subln_g, *, nb, lq, lk, tq, lam_init):
    nq = lq // tq
    kw = kv.shape[1] // 2
    return pl.pallas_call(
        functools.partial(_diff_kernel, lam_init=lam_init),
        grid=(nb, nq),
        in_specs=[
            pl.BlockSpec((tq, q.shape[1]), lambda b, i: (b * nq + i, 0)),
            pl.BlockSpec((lk, kv.shape[1]), lambda b, i: (b, 0), pipeline_mode=pl.Buffered(1)),
            pl.BlockSpec(lam.shape, lambda b, i: (0, 0)),
            pl.BlockSpec((2 * B_HEAD_DIM, 1), lambda b, i: (0, 0)),
        ],
        out_specs=pl.BlockSpec((tq, q.shape[1]), lambda b, i: (b * nq + i, 0)),
        out_shape=jax.ShapeDtypeStruct(q.shape, BF16),
        scratch_shapes=[pltpu.VMEM((kw, lk), BF16)],
        compiler_params=_cparams("parallel", "arbitrary"),
        name="diff_attention",
    )(q, kv, lam, subln_g.reshape(2 * B_HEAD_DIM, 1))


def _decay(x, ld):
    return jnp.exp(-jnp.abs(x * ld))


def _ret_kernel(ld_ref, q_ref, k_ref, v_ref, *rest, nc, has_s0, cross):
    if has_s0:
        s0_ref, o_ref, st_ref, s_ref = rest
    else:
        o_ref, st_ref, s_ref = rest
    c_len = RET_CHUNK
    h = pl.program_id(1)
    s = pl.program_id(2)
    ld_f = ld_ref[0, h]
    ld_b = ld_ref[1, h]
    qb = q_ref[...].astype(BF16)
    k = k_ref[...] * (C_KEY_DIM ** -0.5)
    vb = v_ref[...].astype(BF16)
    idx = lax.broadcasted_iota(jnp.int32, (c_len, 1), 0).astype(F32)
    full = jnp.full((1, 1), float(c_len), F32)
    tdn = (((0,), (0,)), ((), ()))

    def init_state(d):
        if has_s0:
            s_ref[...] = s0_ref[d]
        else:
            s_ref[...] = jnp.zeros_like(s_ref)

    @pl.when(s < nc)
    def _forward():
        @pl.when(s == 0)
        def _():
            init_state(0)

        row0 = pl.multiple_of(s * c_len, c_len)
        rel = (lax.broadcasted_iota(jnp.int32, (c_len, c_len), 0)
               - lax.broadcasted_iota(jnp.int32, (c_len, c_len), 1)).astype(F32)
        dmat = (jnp.where(rel >= 0, _decay(jnp.maximum(rel, 0.0), ld_f), 0.0)
                + jnp.where(rel <= 0, _decay(jnp.maximum(-rel, 0.0), ld_b), 0.0))
        a = lax.dot_general(qb, k.astype(BF16), (((1,), (1,)), ((), ())), preferred_element_type=F32)
        o = _bdot((a * dmat).astype(BF16), vb)
        if cross:
            o = o + _bdot(qb, s_ref[...].astype(BF16)) * _decay(idx + 1.0, ld_f)
        o_ref[pl.ds(row0, c_len), :] = o
        kd = (k * _decay(c_len - 1.0 - idx, ld_f)).astype(BF16)
        s_ref[...] = s_ref[...] * _decay(full, ld_f) + lax.dot_general(kd, vb, tdn, preferred_element_type=F32)

        @pl.when(s == nc - 1)
        def _():
            st_ref[0] = s_ref[...]

    @pl.when(s >= nc)
    def _backward():
        @pl.when(s == nc)
        def _():
            init_state(1)

        row0 = pl.multiple_of((2 * nc - 1 - s) * c_len, c_len)
        if cross:
            o_ref[pl.ds(row0, c_len), :] += _bdot(qb, s_ref[...].astype(BF16)) * _decay(c_len - idx, ld_b)
        kd = (k * _decay(idx, ld_b)).astype(BF16)
        s_ref[...] = s_ref[...] * _decay(full, ld_b) + lax.dot_general(kd, vb, tdn, preferred_element_type=F32)

        @pl.when(s == 2 * nc - 1)
        def _():
            st_ref[1] = s_ref[...]


def retention(proj, log_decay, s0, *, row0, nb, seq):
    c_len = RET_CHUNK
    nc = seq // c_len
    r0 = row0 // c_len
    kblk = (C_HEADS * C_KEY_DIM) // C_KEY_DIM
    vblk = (2 * C_HEADS * C_KEY_DIM) // C_VAL_DIM

    def chunk(b, s):
        return r0 + b * nc + jnp.where(s < nc, s, 2 * nc - 1 - s)

    in_specs = [
        pl.BlockSpec(memory_space=pltpu.SMEM),
        pl.BlockSpec((c_len, C_KEY_DIM), lambda b, h, s: (chunk(b, s), h)),
        pl.BlockSpec((c_len, C_KEY_DIM), lambda b, h, s: (chunk(b, s), kblk + h)),
        pl.BlockSpec((c_len, C_VAL_DIM), lambda b, h, s: (chunk(b, s), vblk + h)),
    ]
    args = [log_decay, proj, proj, proj]
    if s0 is not None:
        in_specs.append(pl.BlockSpec((None, 2, None, C_KEY_DIM, C_VAL_DIM), lambda b, h, s: (b, 0, h, 0, 0)))
        args.append(s0)
    return pl.pallas_call(
        functools.partial(_ret_kernel, nc=nc, has_s0=s0 is not None, cross=(s0 is not None) or nc > 1),
        grid=(nb, C_HEADS, 2 * nc),
        in_specs=in_specs,
        out_specs=[
            pl.BlockSpec((seq, C_VAL_DIM), lambda b, h, s: (b, h)),
            pl.BlockSpec((None, 2, None, C_KEY_DIM, C_VAL_DIM), lambda b, h, s: (b, 0, h, 0, 0)),
        ],
        out_shape=[
            jax.ShapeDtypeStruct((nb * seq, C_HEADS * C_VAL_DIM), F32),
            jax.ShapeDtypeStruct((nb, 2, C_HEADS, C_KEY_DIM, C_VAL_DIM), F32),
        ],
        scratch_shapes=[pltpu.VMEM((C_KEY_DIM, C_VAL_DIM), F32)],
        compiler_params=_cparams("parallel", "parallel", "arbitrary"),
        name="retention",
    )(*args)


def _ret_out_kernel(oc_ref, ol_ref, g_ref, gn_ref, w_ref, x_ref, gate_ref, o_ref, *, tm):
    o = _pick_group(oc_ref, ol_ref, tm)
    parts = []
    for h in range(C_HEADS):
        oh = o[:, h * C_VAL_DIM:(h + 1) * C_VAL_DIM]
        ms = jnp.mean(oh * oh, axis=-1, keepdims=True)
        parts.append((oh * lax.rsqrt(ms + NORM_EPS)) * gn_ref[:, h * C_VAL_DIM:(h + 1) * C_VAL_DIM])
    a = (_silu(g_ref[...]) * jnp.concatenate(parts, axis=1)).astype(BF16)
    o_ref[...] = x_ref[...] + gate_ref[...] * _bdot(a, w_ref[...])


def retention_out(o_ctx, o_lat, proj, gn_g, w, x, gate, *, tm=512):
    t, d = x.shape
    vd = C_HEADS * C_VAL_DIM
    gblk = proj.shape[1] // vd - 1
    seg = lambda i: (_seg_of_tile(i, tm), 0, 0)
    return pl.pallas_call(
        functools.partial(_ret_out_kernel, tm=tm),
        grid=(t // tm,),
        in_specs=_group_specs(o_ctx, o_lat, tm, vd) + [
            pl.BlockSpec((tm, vd), lambda i: (i, gblk)),
            pl.BlockSpec((1, vd), lambda i: (0, 0)),
            pl.BlockSpec((vd, d), lambda i: (0, 0)),
            pl.BlockSpec((tm, d), lambda i: (i, 0)),
            pl.BlockSpec((None, 1, d), seg),
        ],
        out_specs=pl.BlockSpec((tm, d), lambda i: (i, 0)),
        out_shape=jax.ShapeDtypeStruct((t, d), F32),
        compiler_params=_cparams("parallel"),
        name="retention_out",
    )(o_ctx, o_lat, proj, gn_g.reshape(1, vd), w, x, gate)


def _filter_rows(t, seq, band_ref, w1_ref, b1_ref, w2_ref, b2_ref, w3_ref, fr_ref, delta_ref):
    t_norm = t / max(seq - 1, 1)
    lane = lax.broadcasted_iota(jnp.int32, (1, 128), 1)
    ang = (2.0 * math.pi * t) * band_ref[...] / seq
    feat = jnp.where(lane == 0, t_norm,
                     jnp.where(lane <= D_BANDS, jnp.cos(ang), jnp.where(lane <= 2 * D_BANDS, -jnp.sin(ang), 0.0)))
    a = jnp.sin(fr_ref[0:1] * (_bdot(feat.astype(BF16), w1_ref[...].astype(BF16)) + b1_ref[...]))
    a = jnp.sin(fr_ref[1:2] * (_bdot(a.astype(BF16), w2_ref[...].astype(BF16)) + b2_ref[...]))
    f = _bdot(a.astype(BF16), w3_ref[...].astype(BF16))
    window = jnp.exp(-t_norm * delta_ref[...]) + D_MOD_SHIFT
    return f * jnp.concatenate([window] * 4, axis=1)


def _filter_sum_kernel(band_ref, w1_ref, b1_ref, w2_ref, b2_ref, w3_ref, fr_ref, delta_ref, o_ref, *, seq, tm):
    i = pl.program_id(0)
    t = (i * tm + lax.broadcasted_iota(jnp.int32, (tm, 1), 0)).astype(F32)
    f = _filter_rows(t, seq, band_ref, w1_ref, b1_ref, w2_ref, b2_ref, w3_ref, fr_ref, delta_ref)
    part = jnp.sum(jnp.abs(f), axis=0, keepdims=True)

    @pl.when(i == 0)
    def _():
        o_ref[...] = part

    @pl.when(i > 0)
    def _():
        o_ref[...] += part


def _filter_gen_kernel(band_ref, w1_ref, b1_ref, w2_ref, b2_ref, w3_ref, fr_ref, delta_ref, sum_ref, o_ref, *, seq, tm):
    i = pl.program_id(0)
    r = i * tm + lax.broadcasted_iota(jnp.int32, (tm, 1), 0)
    t = jnp.where(r < seq, r, 2 * seq - r).astype(F32)
    f = _filter_rows(t, seq, band_ref, w1_ref, b1_ref, w2_ref, b2_ref, w3_ref, fr_ref, delta_ref)
    f = f / (sum_ref[...] + 1e-6)
    half = 2 * D_MODEL
    fwd, bwd = f[:, :half], f[:, half:]
    o_ref[...] = jnp.where(r < seq, fwd, jnp.where(r > seq, bwd, 0.0)) + jnp.where(r == 0, bwd, 0.0)


def hyena_circular_kernel(seq, band, w1p, b1, w2, b2, w3, freq, delta, *, tm):
    consts = [band, w1p, b1, w2, b2, w3, freq, delta]
    cspecs = [pl.BlockSpec(a.shape, lambda i: (0, 0)) for a in consts]
    nf = w3.shape[1]
    sums = pl.pallas_call(
        functools.partial(_filter_sum_kernel, seq=seq, tm=tm),
        grid=(seq // tm,),
        in_specs=cspecs,
        out_specs=pl.BlockSpec((1, nf), lambda i: (0, 0)),
        out_shape=jax.ShapeDtypeStruct((1, nf), F32),
        compiler_params=_cparams("arbitrary"),
        name="hyena_filter_sum",
    )(*consts)
    return pl.pallas_call(
        functools.partial(_filter_gen_kernel, seq=seq, tm=tm),
        grid=(2 * seq // tm,),
        in_specs=cspecs + [pl.BlockSpec((1, nf), lambda i: (0, 0))],
        out_specs=pl.BlockSpec((tm, nf // 2), lambda i: (i, 0)),
        out_shape=jax.ShapeDtypeStruct((2 * seq, nf // 2), F32),
        compiler_params=_cparams("parallel"),
        name="hyena_filter_gen",
    )(*consts, sums)


def _dft_cs(rows, cols, n):
    m = np.outer(np.arange(rows), np.arange(cols)) % n
    ang = 2.0 * np.pi * m / n
    return np.cos(ang), np.sin(ang)


def _stack_fwd(c, s):
    return np.block([[c, s], [-s, c]])


def _stack_inv(c, s):
    return np.block([[c, -s], [s, c]])


def _dft_mats():
    as_bf16 = lambda a: jnp.asarray(a, F32).astype(BF16)
    c1, s1 = _dft_cs(LAT_N1, LAT_N1, LAT_N1)
    c2, s2 = _dft_cs(LAT_N2, LAT_N2, LAT_N2)
    cc, sc = _dft_cs(CTX_N, CTX_N, CTX_N)
    h1, hc = LAT_N1 // 2, CTX_N // 2
    return dict(
        lat_g1=as_bf16(_stack_fwd(c1[:, :h1], s1[:, :h1])),
        lat_g1r=as_bf16(np.concatenate([c1, -s1], axis=0)),
        lat_g2=as_bf16(_stack_fwd(c2, s2)),
        lat_g2i=as_bf16(_stack_inv(c2, s2)),
        lat_g1i=as_bf16(_stack_inv(c1[:h1], s1[:h1]) / LAT_N),
        ctx_g=as_bf16(_stack_fwd(cc[:, :hc], sc[:, :hc])),
        ctx_gr=as_bf16(np.concatenate([cc, -sc], axis=0)),
        ctx_gi=as_bf16(_stack_inv(cc[:hc], sc[:hc]) / CTX_N),
    )


def _lmul_kernel(g_ref, x_ref, o_ref):
    o_ref[...] = _bdot(g_ref[...], x_ref[...].astype(BF16))


def left_matmul(g, x, *, row_blk=0, tc):
    m, k = g.shape
    n = x.shape[1]
    return pl.pallas_call(
        _lmul_kernel,
        grid=(n // tc,),
        in_specs=[pl.BlockSpec((m, k), lambda j: (0, 0)), pl.BlockSpec((k, tc), lambda j: (row_blk, j))],
        out_specs=pl.BlockSpec((m, tc), lambda j: (0, j)),
        out_shape=jax.ShapeDtypeStruct((m, n), F32),
        compiler_params=_cparams("parallel"),
        name="left_matmul",
    )(g, x)


def _gated_skip(conv, gate, y, skip_ref):
    reps = conv.shape[1] // D_MODEL
    skip = skip_ref[...] if reps == 1 else jnp.concatenate([skip_ref[...]] * reps, axis=1)
    return gate * (conv + skip * y)


def _lat_last_kernel(g_ref, b_ref, gate_ref, y_ref, skip_ref, o_ref):
    conv = _bdot(g_ref[...], b_ref[...].astype(BF16))
    o_ref[...] = _gated_skip(conv, gate_ref[...], y_ref[...], skip_ref).astype(o_ref.dtype)


def lat_last_stage(g, b, gate, gate_blk, y, y_blk, skip, *, tc, out_dtype):
    m, k = g.shape
    n = b.shape[1]
    return pl.pallas_call(
        _lat_last_kernel,
        grid=(n // tc,),
        in_specs=[
            pl.BlockSpec((m, k), lambda j: (0, 0)),
            pl.BlockSpec((k, tc), lambda j: (0, j)),
            pl.BlockSpec((m, tc), lambda j: (gate_blk, j)),
            pl.BlockSpec((m, tc), lambda j: (y_blk, j)),
            pl.BlockSpec((1, D_MODEL), lambda j: (0, 0)),
        ],
        out_specs=pl.BlockSpec((m, tc), lambda j: (0, j)),
        out_shape=jax.ShapeDtypeStruct((m, n), out_dtype),
        compiler_params=_cparams("parallel"),
        name="hyena_lat_last",
    )(g, b, gate, y, skip)


def _lat_mid_kernel(a_ref, *rest, conv):
    if conv:
        h_ref, g2_ref, g2i_ref, o_ref = rest
    else:
        g2_ref, o_ref = rest
    k1 = pl.program_id(0)
    n2 = lax.broadcasted_iota(jnp.int32, (LAT_N2, 1), 0)
    ang = (k1 * n2).astype(F32) * (2.0 * math.pi / LAT_N)
    c, s = jnp.cos(ang), jnp.sin(ang)
    ar, ai = a_ref[0], a_ref[1]
    t = jnp.concatenate([ar * c + ai * s, ai * c - ar * s], axis=0).astype(BF16)
    x = _bdot(g2_ref[...], t)
    xr, xi = x[:LAT_N2], x[LAT_N2:]
    if not conv:
        o_ref[0] = xr
        o_ref[1] = xi
        return
    hr, hi = h_ref[0], h_ref[1]
    y = jnp.concatenate([xr * hr - xi * hi, xr * hi + xi * hr], axis=0).astype(BF16)
    b = _bdot(g2i_ref[...], y)
    br, bi = b[:LAT_N2], b[LAT_N2:]
    o_ref[0] = br * c - bi * s
    o_ref[1] = bi * c + br * s


def lat_mid_stage(a, mats, h=None, order=0):
    w = a.shape[-1]
    blk = lambda width, col: pl.BlockSpec((2, None, LAT_N2, width), lambda k1, j: (0, k1, 0, col(j)))
    gspec = pl.BlockSpec((2 * LAT_N2, 2 * LAT_N2), lambda k1, j: (0, 0))
    if h is None:
        in_specs = [blk(D_MODEL, lambda j: j), gspec]
        args = [a, mats["lat_g2"]]
    else:
        in_specs = [blk(D_MODEL, lambda j: j), blk(D_MODEL, lambda j: order), gspec, gspec]
        args = [a, h, mats["lat_g2"], mats["lat_g2i"]]
    return pl.pallas_call(
        functools.partial(_lat_mid_kernel, conv=h is not None),
        grid=(LAT_N1, w // D_MODEL),
        in_specs=in_specs,
        out_specs=blk(D_MODEL, lambda j: j),
        out_shape=jax.ShapeDtypeStruct(a.shape, F32),
        compiler_params=_cparams("parallel", "arbitrary"),
        name="hyena_lat_mid",
    )(*args)


def _ctx_conv_kernel(y_ref, gate_ref, h_ref, g_ref, gi_ref, skip_ref, o_ref):
    n = CTX_N
    y = y_ref[...]
    z = _bdot(g_ref[...], y.astype(BF16))
    zr, zi = z[:n], z[n:]
    hr, hi = h_ref[0:n], h_ref[n:]
    w = jnp.concatenate([zr * hr - zi * hi, zr * hi + zi * hr], axis=0).astype(BF16)
    conv = _bdot(gi_ref[...], w)
    o_ref[...] = _gated_skip(conv, gate_ref[...], y, skip_ref).astype(o_ref.dtype)


def ctx_conv(y, y_plane, gate, gate_plane, h, order, mats, skip, *, out_dtype):
    rows = 2 * SEQ
    return pl.pallas_call(
        _ctx_conv_kernel,
        grid=(BATCH // 2,),
        in_specs=[
            pl.BlockSpec((None, rows, D_MODEL), lambda p: (y_plane, p, 0)),
            pl.BlockSpec((None, rows, D_MODEL), lambda p: (gate_plane, p, 0)),
            pl.BlockSpec((2 * CTX_N, D_MODEL), lambda p: (0, order)),
            pl.BlockSpec((2 * CTX_N, rows), lambda p: (0, 0)),
            pl.BlockSpec((rows, 2 * CTX_N), lambda p: (0, 0)),
            pl.BlockSpec((1, D_MODEL), lambda p: (0, 0)),
        ],
        out_specs=pl.BlockSpec((rows, D_MODEL), lambda p: (p, 0)),
        out_shape=jax.ShapeDtypeStruct((T_CTX, D_MODEL), out_dtype),
        compiler_params=_cparams("parallel"),
        name="hyena_ctx_conv",
    )(y, gate, h, mats["ctx_g"], mats["ctx_gi"], skip)


def hyena_core(z3, f_w1, f_b1, f_w2, f_b2, f_w3, f_freq, f_skip):
    mats = _dft_mats()
    bands = jnp.linspace(1e-4, D_BANDS - 1, D_BANDS, dtype=F32)
    band = jnp.zeros((1, 128), F32).at[0, 1:1 + D_BANDS].set(bands).at[0, 1 + D_BANDS:1 + 2 * D_BANDS].set(bands)
    w1p = jnp.zeros((128, D_FILTER_HIDDEN), F32).at[:D_EMB].set(f_w1)
    max_decay = math.log(D_DECAY_TARGET) / D_FAST_DECAY_PCT
    min_decay = math.log(D_DECAY_TARGET) / D_SLOW_DECAY_PCT
    delta = jnp.abs(jnp.linspace(min_decay, max_decay, D_MODEL, dtype=F32)).reshape(1, D_MODEL)
    fargs = (band, w1p, f_b1.reshape(1, -1), f_w2, f_b2.reshape(1, -1), f_w3, f_freq, delta)

    kc_ctx = hyena_circular_kernel(SEQ, *fargs, tm=SEQ)
    h_ctx = left_matmul(mats["ctx_gr"], kc_ctx, tc=D_MODEL)
    kc_lat = hyena_circular_kernel(DEC_SEQ, *fargs, tm=512)
    wide = LAT_N2 * 2 * D_MODEL
    a = left_matmul(mats["lat_g1r"], kc_lat.reshape(LAT_N1, wide), tc=8192)
    h_lat = lat_mid_stage(a.reshape(2, LAT_N1, LAT_N2, 2 * D_MODEL), mats)

    y1 = ctx_conv(z3, 2, z3, 0, h_ctx, 0, mats, f_skip[0:1], out_dtype=F32)
    y_ctx = ctx_conv(y1[None], 0, z3, 1, h_ctx, 1, mats, f_skip[1:2], out_dtype=BF16)

    cols = LAT_N2 * D_MODEL
    rows = DEC_BATCH * LAT_N1 // 2
    zw = z3.reshape(3 * T_ALL * D_MODEL // cols, cols)
    plane_blks = T_ALL * D_MODEL // cols // rows
    lat_blk = lambda plane: plane * plane_blks + T_CTX * D_MODEL // cols // rows
    y, y_blk = zw, lat_blk(2)
    for n in range(2):
        a = left_matmul(mats["lat_g1"], y, row_blk=y_blk, tc=8192)
        b = lat_mid_stage(a.reshape(2, LAT_N1, LAT_N2, D_MODEL), mats, h_lat, n)
        y = lat_last_stage(mats["lat_g1i"], b.reshape(2 * LAT_N1, cols), zw, lat_blk(n), y, y_blk, f_skip[n:n + 1],
                           tc=8192, out_dtype=F32 if n == 0 else BF16)
        y_blk = 0
    return y_ctx, y.reshape(T_LAT, D_MODEL)


def kernel(x_prompt, x_sample, cache_attn_k, cache_attn_v, cache_diff_k, cache_diff_v, state_ret, c, c_ctx, w_mod, b_mod, norm1_g, norm2_g, final_g, attn_w_qkv, attn_q_g, attn_k_g, attn_w_o, diff_w_qkv, diff_lambda, diff_subln_g, diff_w_o, ret_w_in, ret_log_decay, ret_gn_g, ret_w_o, hyena_w_in, hyena_sc_w, hyena_sc_b, hyena_f_w1, hyena_f_b1, hyena_f_w2, hyena_f_b2, hyena_f_w3, hyena_f_freq, hyena_f_skip, hyena_w_o, ffn_w_up, ffn_conv_w, ffn_conv_b, ffn_w_down):
    d = D_MODEL
    x = jnp.concatenate([x_prompt.reshape(T_CTX, d), x_sample.reshape(T_LAT, d)], axis=0)
    cond = jnp.zeros((SEG_ROWS, d), F32).at[0].set(c_ctx).at[1:N_SEG].set(c)
    mod = modulation(cond, w_mod, b_mod)
    mod = mod.reshape(DEPTH, SEG_ROWS, 6, d).transpose(0, 2, 1, 3).reshape(DEPTH, 6, SEG_ROWS, 1, d)
    rope = rope_tables()
    bf = lambda w: w.astype(BF16)
    out = {}

    for l in range(DEPTH):
        m, j = l % 4, l // 4
        sh1, sc1, g1, sh2, sc2, g2 = (mod[l, i] for i in range(6))
        if m == 0:
            nq, nk = A_HEADS * A_HEAD_DIM, A_KV_HEADS * A_HEAD_DIM
            scale = A_HEAD_DIM ** -0.5 * LOG2E
            qkv = norm_proj(x, norm1_g[l], sh1, sc1, bf(attn_w_qkv[j]))
            q_c, k_c, kv_c = qk_prep(qkv, row0=0, rows=T_CTX, nq=nq, nk=nk, q_g=attn_q_g[j], k_g=attn_k_g[j],
                                     scale=scale)
            q_l, _, kv_l = qk_prep(qkv, row0=T_CTX, rows=T_LAT, nq=nq, nk=nk, q_g=attn_q_g[j], k_g=attn_k_g[j],
                                   rope=rope, scale=scale)
            out["attn_k"] = k_c.reshape(BATCH, 1, SEQ, A_KV_HEADS, A_HEAD_DIM)
            out["attn_v"] = qkv[:T_CTX, nq + nk:].reshape(BATCH, 1, SEQ, A_KV_HEADS, A_HEAD_DIM)
            cache = jnp.concatenate([cache_attn_k[:, j].reshape(DEC_BATCH, PAST_LEN, nk),
                                     cache_attn_v[:, j].reshape(DEC_BATCH, PAST_LEN, nk)], axis=-1).astype(BF16)
            lk = PAST_LEN + DEC_SEQ
            kv_all = jnp.concatenate([cache, kv_l.reshape(DEC_BATCH, DEC_SEQ, 2 * nk)], axis=1).reshape(DEC_BATCH * lk, 2 * nk)
            o_c = gqa_attention(q_c, kv_c, nb=BATCH, lq=SEQ, lk=SEQ, tq=SEQ)
            o_l = gqa_attention(q_l, kv_all, nb=DEC_BATCH, lq=DEC_SEQ, lk=lk, tq=128)
            x = out_proj_residual(o_c, o_l, bf(attn_w_o[j]), x, g1)
        elif m == 1:
            nq = nk = B_HEADS * 2 * B_HEAD_DIM
            scale = B_HEAD_DIM ** -0.5 * LOG2E
            lam_init = 0.8 - 0.6 * math.exp(-0.3 * l)
            qkv = norm_proj(x, norm1_g[l], sh1, sc1, bf(diff_w_qkv[j]))
            q_c, k_c, kv_c = qk_prep(qkv, row0=0, rows=T_CTX, nq=nq, nk=nk, scale=scale)
            q_l, _, kv_l = qk_prep(qkv, row0=T_CTX, rows=T_LAT, nq=nq, nk=nk, rope=rope, scale=scale)
            out["diff_k"] = k_c.reshape(BATCH, 1, SEQ, B_HEADS, 2 * B_HEAD_DIM)
            out["diff_v"] = qkv[:T_CTX, nq + nk:].reshape(BATCH, 1, SEQ, B_HEADS, 2 * B_HEAD_DIM)
            cache = jnp.concatenate([cache_diff_k[:, j].reshape(DEC_BATCH, PAST_LEN, nk),
                                     cache_diff_v[:, j].reshape(DEC_BATCH, PAST_LEN, nk)], axis=-1).astype(BF16)
            lk = PAST_LEN + DEC_SEQ
            kv_all = jnp.concatenate([cache, kv_l.reshape(DEC_BATCH, DEC_SEQ, 2 * nk)], axis=1).reshape(DEC_BATCH * lk, 2 * nk)
            dargs = (diff_lambda[j], diff_subln_g[j])
            o_c = diff_attention(q_c, kv_c, *dargs, nb=BATCH, lq=SEQ, lk=SEQ, tq=SEQ, lam_init=lam_init)
            o_l = diff_attention(q_l, kv_all, *dargs, nb=DEC_BATCH, lq=DEC_SEQ, lk=lk, tq=256, lam_init=lam_init)
            x = out_proj_residual(o_c, o_l, bf(diff_w_o[j]), x, g1)
        elif m == 2:
            proj = norm_proj(x, norm1_g[l], sh1, sc1, bf(ret_w_in[j]))
            o_c, st = retention(proj, ret_log_decay[j], None, row0=0, nb=BATCH, seq=SEQ)
            o_l, _ = retention(proj, ret_log_decay[j], state_ret[:, j], row0=T_CTX, nb=DEC_BATCH, seq=DEC_SEQ)
            out["ret_s"] = st.reshape(BATCH, 1, 2, C_HEADS, C_KEY_DIM, C_VAL_DIM)
            x = retention_out(o_c, o_l, proj, ret_gn_g[j], bf(ret_w_o[j]), x, g1)
        else:
            z3 = norm_proj_conv(x, norm1_g[l], sh1, sc1, bf(hyena_w_in[j]), hyena_sc_w[j], hyena_sc_b[j])
            y_c, y_l = hyena_core(z3, hyena_f_w1[j], hyena_f_b1[j], hyena_f_w2[j], hyena_f_b2[j], hyena_f_w3[j],
                                  hyena_f_freq[j], hyena_f_skip[j])
            x = out_proj_residual(y_c, y_l, bf(hyena_w_o[j]), x, g1)
        x = conv_ffn(x, norm2_g[l], sh2, sc2, g2, bf(ffn_w_up[l]), ffn_conv_w[l], ffn_conv_b[l], bf(ffn_w_down[l]),
                     final_g if l == DEPTH - 1 else None)

    y_prompt = x[:T_CTX].reshape(BATCH, SEQ, d)
    y_sample = x[T_CTX:].reshape(DEC_BATCH, DEC_SEQ, d)
    return (y_prompt, y_sample, out["attn_k"], out["attn_v"], out["diff_k"], out["diff_v"], out["ret_s"])
```

```python
import functools
import math

import jax
import jax.numpy as jnp
import numpy as np
from jax import lax
from jax.experimental import pallas as pl
from jax.experimental.pallas import tpu as pltpu

F32 = jnp.float32
BF16 = jnp.bfloat16

D_MODEL = 1024
BATCH = 32
SEQ = 256
DEPTH = 4
DEC_BATCH = 2
DEC_SEQ = 4096
PAST_LEN = 256
GRID_W = 64
ROPE_THETA = 10000.0
NORM_EPS = 1e-6
A_HEADS = 16
A_KV_HEADS = 4
A_HEAD_DIM = 64
A_GROUP = A_HEADS // A_KV_HEADS
B_HEADS = 8
B_HEAD_DIM = 64
B_SUBLN_EPS = 1e-5
C_HEADS = 4
C_KEY_DIM = 256
C_VAL_DIM = 512
D_BANDS = 16
D_EMB = 1 + 2 * D_BANDS
D_FILTER_HIDDEN = 64
D_FAST_DECAY_PCT = 0.3
D_SLOW_DECAY_PCT = 1.5
D_DECAY_TARGET = 1e-2
D_MOD_SHIFT = 0.05
FFN_DIM = 2816

T_CTX = BATCH * SEQ
T_LAT = DEC_BATCH * DEC_SEQ
T_ALL = T_CTX + T_LAT
N_SEG = 1 + DEC_BATCH
SEG_ROWS = 8

HALO = 16
RET_CHUNK = 256
VMEM_LIMIT = 56 * 1024 * 1024

LAT_N = 2 * DEC_SEQ
LAT_N1 = 64
LAT_N2 = LAT_N // LAT_N1
CTX_N = 2 * SEQ


def _cparams(*sem):
    return pltpu.CompilerParams(dimension_semantics=sem, vmem_limit_bytes=VMEM_LIMIT)


def _seg_of_tile(i, tm):
    start = i * tm
    return jnp.where(start < T_CTX, 0, 1 + (start - T_CTX) // DEC_SEQ)


def _silu(x):
    return x * jax.nn.sigmoid(x)


def _bdot(a, b):
    return jnp.dot(a, b, preferred_element_type=F32)


def _norm_mod(x, g, sh, sc):
    ms = jnp.mean(x * x, axis=-1, keepdims=True)
    y = (x * lax.rsqrt(ms + NORM_EPS)) * g
    return y * (1.0 + sc) + sh


def _mod_kernel(c_ref, w_ref, b_ref, o_ref):
    s = _silu(c_ref[...]).astype(BF16)
    o_ref[...] = _bdot(s, w_ref[...].astype(BF16)) + b_ref[...]


def modulation(cond, w_mod, b_mod):
    tn = 1536
    n = w_mod.shape[-1]
    return pl.pallas_call(
        _mod_kernel,
        grid=(DEPTH, n // tn),
        in_specs=[
            pl.BlockSpec((SEG_ROWS, D_MODEL), lambda l, j: (0, 0)),
            pl.BlockSpec((None, D_MODEL, tn), lambda l, j: (l, 0, j)),
            pl.BlockSpec((None, 1, tn), lambda l, j: (l, 0, j)),
        ],
        out_specs=pl.BlockSpec((None, SEG_ROWS, tn), lambda l, j: (l, 0, j)),
        out_shape=jax.ShapeDtypeStruct((DEPTH, SEG_ROWS, n), F32),
        compiler_params=_cparams("arbitrary", "arbitrary"),
        name="modulation",
    )(cond, w_mod, b_mod.reshape(DEPTH, 1, n))


def _proj_kernel(x_ref, g_ref, sh_ref, sc_ref, w_ref, o_ref, h_ref):
    @pl.when(pl.program_id(1) == 0)
    def _():
        h_ref[...] = _norm_mod(x_ref[...], g_ref[...], sh_ref[...], sc_ref[...]).astype(BF16)

    o_ref[...] = _bdot(h_ref[...], w_ref[...]).astype(o_ref.dtype)


def norm_proj(x, g, sh, sc, w, *, tm=1024, tn=1536, out_dtype=F32):
    t, d = x.shape
    n = w.shape[1]
    seg = lambda i, j: (_seg_of_tile(i, tm), 0, 0)
    return pl.pallas_call(
        _proj_kernel,
        grid=(t // tm, n // tn),
        in_specs=[
            pl.BlockSpec((tm, d), lambda i, j: (i, 0)),
            pl.BlockSpec((1, d), lambda i, j: (0, 0)),
            pl.BlockSpec((None, 1, d), seg),
            pl.BlockSpec((None, 1, d), seg),
            pl.BlockSpec((d, tn), lambda i, j: (0, j)),
        ],
        out_specs=pl.BlockSpec((tm, tn), lambda i, j: (i, j)),
        out_shape=jax.ShapeDtypeStruct((t, n), out_dtype),
        scratch_shapes=[pltpu.VMEM((tm, d), BF16)],
        compiler_params=_cparams("parallel", "arbitrary"),
        name="norm_proj",
    )(x, g.reshape(1, d), sh, sc, w)


def _conv3(u, cw, cb, i, tm):
    rows = u.shape[0]
    up = pltpu.roll(u, 1, 0)[HALO:HALO + tm]
    uc = u[HALO:HALO + tm]
    un = pltpu.roll(u, rows - 1, 0)[HALO:HALO + tm]
    sub = lax.broadcasted_iota(jnp.int32, (8, 1), 0)
    is_ctx = i * tm < T_CTX
    ups, uns = [], []
    for r in range(0, tm, SEQ):
        start = i * tm + r
        first = jnp.logical_or(is_ctx, (start & (DEC_SEQ - 1)) == 0)
        last = jnp.logical_or(is_ctx, ((start + SEQ) & (DEC_SEQ - 1)) == 0)
        ups += [jnp.where(jnp.logical_and(sub == 0, first), 0.0, up[r:r + 8]), up[r + 8:r + SEQ]]
        uns += [un[r:r + SEQ - 8], jnp.where(jnp.logical_and(sub == 7, last), 0.0, un[r + SEQ - 8:r + SEQ])]
    up = jnp.concatenate(ups, axis=0)
    un = jnp.concatenate(uns, axis=0)
    return up * cw[0:1] + uc * cw[1:2] + un * cw[2:3] + cb


def _fill_h(h_ref, xp_ref, x_ref, xn_ref, g_ref, sh_ref, sc_ref, tm):
    g, sh, sc = g_ref[...], sh_ref[...], sc_ref[...]
    h_ref[0:HALO] = _norm_mod(xp_ref[...], g, sh, sc).astype(BF16)
    h_ref[HALO:HALO + tm] = _norm_mod(x_ref[...], g, sh, sc).astype(BF16)
    h_ref[HALO + tm:] = _norm_mod(xn_ref[...], g, sh, sc).astype(BF16)


def _proj_conv_kernel(xp_ref, x_ref, xn_ref, g_ref, sh_ref, sc_ref, w_ref, cw_ref, cb_ref, o_ref, h_ref, *, tm):
    i = pl.program_id(0)

    @pl.when(pl.program_id(1) == 0)
    def _():
        _fill_h(h_ref, xp_ref, x_ref, xn_ref, g_ref, sh_ref, sc_ref, tm)

    u = _bdot(h_ref[...], w_ref[...])
    o_ref[...] = _conv3(u, cw_ref[...], cb_ref[...], i, tm)


def _ffn_kernel(xp_ref, x_ref, xn_ref, g_ref, sh_ref, sc_ref, gate_ref, wu_ref, cw_ref, cb_ref, wd_ref, *rest, tm, tn,
                final):
    if final:
        fg_ref, o_ref = rest
    else:
        (o_ref,) = rest
    i = pl.program_id(0)
    g, sh, sc = g_ref[...], sh_ref[...], sc_ref[...]
    x = x_ref[...]
    h = jnp.concatenate([_norm_mod(xp_ref[...], g, sh, sc).astype(BF16), _norm_mod(x, g, sh, sc).astype(BF16),
                         _norm_mod(xn_ref[...], g, sh, sc).astype(BF16)], axis=0)
    f = wd_ref.shape[0]
    acts = []
    for c in range(0, f, tn):
        a = _conv3(_bdot(h, wu_ref[:, c:c + tn]), cw_ref[:, c:c + tn], cb_ref[:, c:c + tn], i, tm)
        b = _conv3(_bdot(h, wu_ref[:, f + c:f + c + tn]), cw_ref[:, f + c:f + c + tn], cb_ref[:, f + c:f + c + tn], i, tm)
        acts.append((_silu(a) * b).astype(BF16))
    y = x + gate_ref[...] * _bdot(jnp.concatenate(acts, axis=1), wd_ref[...])
    if final:
        ms = jnp.mean(y * y, axis=-1, keepdims=True)
        y = (y * lax.rsqrt(ms + NORM_EPS)) * fg_ref[...]
    o_ref[...] = y


def _halo_specs(t, tm, d):
    per = tm // HALO
    last_blk = t // HALO - 1
    return [
        pl.BlockSpec((HALO, d), lambda i, *_: (jnp.maximum(i * per - 1, 0), 0)),
        pl.BlockSpec((tm, d), lambda i, *_: (i, 0)),
        pl.BlockSpec((HALO, d), lambda i, *_: (jnp.minimum((i + 1) * per, last_blk), 0)),
    ]


def norm_proj_conv(x, g, sh, sc, w, cw, cb, *, tm=1024, tn=512):
    t, d = x.shape
    n = w.shape[1]
    per_group = d // tn
    seg = lambda i, j: (_seg_of_tile(i, tm), 0, 0)
    return pl.pallas_call(
        functools.partial(_proj_conv_kernel, tm=tm),
        grid=(t // tm, n // tn),
        in_specs=_halo_specs(t, tm, d) + [
            pl.BlockSpec((1, d), lambda i, j: (0, 0)),
            pl.BlockSpec((None, 1, d), seg),
            pl.BlockSpec((None, 1, d), seg),
            pl.BlockSpec((d, tn), lambda i, j: (0, j)),
            pl.BlockSpec((3, tn), lambda i, j: (0, j)),
            pl.BlockSpec((1, tn), lambda i, j: (0, j)),
        ],
        out_specs=pl.BlockSpec((None, tm, tn), lambda i, j: (j // per_group, i, j % per_group)),
        out_shape=jax.ShapeDtypeStruct((n // d, t, d), F32),
        scratch_shapes=[pltpu.VMEM((tm + 2 * HALO, d), BF16)],
        compiler_params=_cparams("parallel", "arbitrary"),
        name="norm_proj_conv",
    )(x, x, x, g.reshape(1, d), sh, sc, w, cw, cb.reshape(1, n))


def conv_ffn(x, g, sh, sc, gate, w_up, cw, cb, w_down, final_g=None, *, tm=512, tn=256):
    t, d = x.shape
    f = w_down.shape[0]
    seg = lambda i: (_seg_of_tile(i, tm), 0, 0)
    whole = lambda a: pl.BlockSpec(a.shape, lambda i: (0, 0), pipeline_mode=pl.Buffered(1))
    cb = cb.reshape(1, 2 * f)
    in_specs = _halo_specs(t, tm, d) + [
        pl.BlockSpec((1, d), lambda i: (0, 0)),
        pl.BlockSpec((None, 1, d), seg),
        pl.BlockSpec((None, 1, d), seg),
        pl.BlockSpec((None, 1, d), seg),
        whole(w_up), whole(cw), whole(cb), whole(w_down),
    ]
    args = [x, x, x, g.reshape(1, d), sh, sc, gate, w_up, cw, cb, w_down]
    if final_g is not None:
        in_specs.append(pl.BlockSpec((1, d), lambda i: (0, 0)))
        args.append(final_g.reshape(1, d))
    return pl.pallas_call(
        functools.partial(_ffn_kernel, tm=tm, tn=tn, final=final_g is not None),
        grid=(t // tm,),
        in_specs=in_specs,
        out_specs=pl.BlockSpec((tm, d), lambda i: (i, 0)),
        out_shape=jax.ShapeDtypeStruct((t, d), F32),
        compiler_params=_cparams("parallel"),
        name="conv_ffn",
    )(*args)


def _group_specs(a_ctx, a_lat, tm, width):
    nctx = T_CTX // tm
    off = nctx if a_lat.shape[0] == T_ALL else 0
    return [
        pl.BlockSpec((tm, width), lambda i: (jnp.minimum(i, nctx - 1), 0)),
        pl.BlockSpec((tm, width), lambda i: (off + jnp.maximum(i - nctx, 0), 0)),
    ]


def _pick_group(ac_ref, al_ref, tm):
    return jnp.where(pl.program_id(0) < T_CTX // tm, ac_ref[...], al_ref[...])


def _out_proj_kernel(ac_ref, al_ref, w_ref, x_ref, gate_ref, o_ref, *, tm):
    a = _pick_group(ac_ref, al_ref, tm).astype(BF16)
    o_ref[...] = x_ref[...] + gate_ref[...] * _bdot(a, w_ref[...])


def out_proj_residual(a_ctx, a_lat, w, x, gate, *, tm=512):
    t, d = x.shape
    k = w.shape[0]
    seg = lambda i: (_seg_of_tile(i, tm), 0, 0)
    return pl.pallas_call(
        functools.partial(_out_proj_kernel, tm=tm),
        grid=(t // tm,),
        in_specs=_group_specs(a_ctx, a_lat, tm, k) + [
            pl.BlockSpec((k, d), lambda i: (0, 0)),
            pl.BlockSpec((tm, d), lambda i: (i, 0)),
            pl.BlockSpec((None, 1, d), seg),
        ],
        out_specs=pl.BlockSpec((tm, d), lambda i: (i, 0)),
        out_shape=jax.ShapeDtypeStruct((t, d), F32),
        compiler_params=_cparams("parallel"),
        name="out_proj_residual",
    )(a_ctx, a_lat, w, x, gate)


def _head_mean_sq(x, head_dim):
    n = x.shape[1]
    x2 = x * x
    hi = x2.astype(BF16)
    lo = (x2 - hi.astype(F32)).astype(BF16)
    blk = 256
    r = lax.broadcasted_iota(jnp.int32, (blk, blk), 0) // head_dim
    c = lax.broadcasted_iota(jnp.int32, (blk, blk), 1) // head_dim
    ones = (r == c).astype(BF16)
    parts = []
    for s in range(0, n, blk):
        parts.append(_bdot(hi[:, s:s + blk], ones) + _bdot(lo[:, s:s + blk], ones))
    ss = parts[0] if len(parts) == 1 else jnp.concatenate(parts, axis=1)
    return ss * (1.0 / head_dim)


def _rope(x, cos, sin):
    n = x.shape[1]
    lane = lax.broadcasted_iota(jnp.int32, (1, 128), 1)
    lower = (lane & 31) < 16
    outs = []
    for s in range(0, n, 128):
        xs = x[:, s:s + 128]
        partner = jnp.where(lower, pltpu.roll(xs, 128 - 16, 1), pltpu.roll(xs, 16, 1))
        outs.append(xs * cos + partner * sin)
    return jnp.concatenate(outs, axis=1)


def _store_cache(o_ref, x, cache):
    if cache == "seq_minor":
        for b in range(o_ref.shape[0]):
            o_ref[b] = x[b * SEQ:(b + 1) * SEQ].T
    else:
        for h in range(o_ref.shape[1]):
            o_ref[:, h, :] = x[:, h * 128:(h + 1) * 128]


def _qk_prep_kernel(*refs, nq, nk, norm, rope, scale, cache):
    it = iter(refs)
    q_ref, k_ref, v_ref = next(it), next(it), next(it)
    if norm:
        qg_ref, kg_ref = next(it), next(it)
    if rope:
        cos_ref, sin_ref = next(it), next(it)
    qo_ref, kvo_ref = next(it), next(it)
    q = q_ref[...]
    k = k_ref[...]
    v = v_ref[...]
    if norm:
        q = (q * lax.rsqrt(_head_mean_sq(q, A_HEAD_DIM) + NORM_EPS)) * qg_ref[...]
        k = (k * lax.rsqrt(_head_mean_sq(k, A_HEAD_DIM) + NORM_EPS)) * kg_ref[...]
    if rope:
        cos, sin = cos_ref[...], sin_ref[...]
        q = _rope(q, cos, sin)
        k = _rope(k, cos, sin)
    qo_ref[...] = (q * scale).astype(BF16)
    kvo_ref[:, 0:nk] = k.astype(BF16)
    kvo_ref[:, nk:] = v.astype(BF16)
    if cache is not None:
        _store_cache(next(it), k, cache)
        _store_cache(next(it), v, cache)


def qk_prep(qkv, *, row0, rows, nq, nk, q_g=None, k_g=None, rope=None, scale, cache=None, tm=512):
    norm = q_g is not None
    r0 = row0 // tm
    qb = nq // nk
    in_specs = [
        pl.BlockSpec((tm, nq), lambda i: (i + r0, 0)),
        pl.BlockSpec((tm, nk), lambda i: (i + r0, qb)),
        pl.BlockSpec((tm, nk), lambda i: (i + r0, qb + 1)),
    ]
    args = [qkv, qkv, qkv]
    if norm:
        in_specs += [pl.BlockSpec((1, nq), lambda i: (0, 0)), pl.BlockSpec((1, nk), lambda i: (0, 0))]
        args += [jnp.tile(q_g, nq // q_g.shape[0]).reshape(1, nq), jnp.tile(k_g, nk // k_g.shape[0]).reshape(1, nk)]
    if rope is not None:
        per = DEC_SEQ // tm
        in_specs += [pl.BlockSpec((tm, 128), lambda i: (i % per, 0))] * 2
        args += list(rope)
    out_specs = [pl.BlockSpec((tm, nq), lambda i: (i, 0)), pl.BlockSpec((tm, 2 * nk), lambda i: (i, 0))]
    out_shape = [jax.ShapeDtypeStruct((rows, nq), BF16), jax.ShapeDtypeStruct((rows, 2 * nk), BF16)]
    if cache == "seq_minor":
        out_specs += [pl.BlockSpec((tm // SEQ, nk, SEQ), lambda i: (i, 0, 0))] * 2
        out_shape += [jax.ShapeDtypeStruct((rows // SEQ, nk, SEQ), F32)] * 2
    elif cache == "head_tile":
        out_specs += [pl.BlockSpec((tm, nk // 128, 128), lambda i: (i, 0, 0))] * 2
        out_shape += [jax.ShapeDtypeStruct((rows, nk // 128, 128), F32)] * 2
    return pl.pallas_call(
        functools.partial(_qk_prep_kernel, nq=nq, nk=nk, norm=norm, rope=rope is not None, scale=scale, cache=cache),
        grid=(rows // tm,),
        in_specs=in_specs,
        out_specs=out_specs,
        out_shape=out_shape,
        compiler_params=_cparams("parallel"),
        name="qk_prep",
    )(*args)


def rope_tables():
    t = jnp.arange(DEC_SEQ)
    row = (t // GRID_W).astype(F32)
    col = (t % GRID_W).astype(F32)
    half = A_HEAD_DIM // 4
    inv_freq = ROPE_THETA ** (-jnp.arange(half, dtype=F32) / half)
    ar = row[:, None] * inv_freq[None, :]
    ac = col[:, None] * inv_freq[None, :]
    cos = jnp.concatenate([jnp.cos(ar), jnp.cos(ar), jnp.cos(ac), jnp.cos(ac)], axis=1)
    sin = jnp.concatenate([-jnp.sin(ar), jnp.sin(ar), -jnp.sin(ac), jnp.sin(ac)], axis=1)
    return jnp.tile(cos, (1, 2)), jnp.tile(sin, (1, 2))


LOG2E = math.log2(math.e)


def _transpose_bf16(x):
    return x.astype(F32).T.astype(BF16)


ATT_TK = 256


def _fill_vt(vt_ref, kv_ref, kw):
    @pl.when(pl.program_id(1) == 0)
    def _():
        vt_ref[...] = _transpose_bf16(kv_ref[:, kw:])


def _attend(n, lk, k_piece, qts, vt_piece):
    out = []
    prev_sts = prev_m = None
    for s in range(n + 1):
        sts, m, l, acc = [], None, None, None
        for r0 in range(0, lk, ATT_TK):
            if s < n:
                st = _bdot(k_piece(s, r0), qts[s])
                m_c = jnp.max(st, axis=0, keepdims=True)
                m = m_c if m is None else jnp.maximum(m, m_c)
                sts.append(st)
            if s > 0:
                p = jnp.exp2(prev_sts[r0 // ATT_TK] - prev_m)
                l_c = jnp.sum(p, axis=0, keepdims=True)
                pv = _bdot(vt_piece(s - 1, r0), p.astype(BF16))
                l = l_c if l is None else l + l_c
                acc = pv if acc is None else acc + pv
        if s > 0:
            out.append((acc, l))
        prev_sts, prev_m = sts, m
    return out


def _gqa_kernel(q_ref, kv_ref, o_ref, vt_ref, *, tq):
    d = A_HEAD_DIM
    kvw = A_KV_HEADS * d
    _fill_vt(vt_ref, kv_ref, kvw)
    qt = _transpose_bf16(q_ref[...])
    qts = [jnp.concatenate([qt[(kh * A_GROUP + g) * d:(kh * A_GROUP + g + 1) * d] for g in range(A_GROUP)], axis=1)
           for kh in range(A_KV_HEADS)]

    k_piece = lambda kh, r0: kv_ref[r0:r0 + ATT_TK, kh * d:(kh + 1) * d]
    vt_piece = lambda kh, r0: vt_ref[kh * d:(kh + 1) * d, r0:r0 + ATT_TK]
    blocks = []
    for acc, l in _attend(A_KV_HEADS, kv_ref.shape[0], k_piece, qts, vt_piece):
        ot = acc / l
        blocks += [ot[:, g * tq:(g + 1) * tq] for g in range(A_GROUP)]
    o_ref[...] = jnp.concatenate(blocks, axis=0).T.astype(o_ref.dtype)


def gqa_attention(q, kv, *, nb, lq, lk, tq):
    nq = lq // tq
    kvw = kv.shape[1] // 2
    return pl.pallas_call(
        functools.partial(_gqa_kernel, tq=tq),
        grid=(nb, nq),
        in_specs=[
            pl.BlockSpec((tq, q.shape[1]), lambda b, i: (b * nq + i, 0)),
            pl.BlockSpec((lk, kv.shape[1]), lambda b, i: (b, 0), pipeline_mode=pl.Buffered(1)),
        ],
        out_specs=pl.BlockSpec((tq, q.shape[1]), lambda b, i: (b * nq + i, 0)),
        out_shape=jax.ShapeDtypeStruct(q.shape, BF16),
        scratch_shapes=[pltpu.VMEM((kvw, lk), BF16)],
        compiler_params=_cparams("parallel", "arbitrary"),
        name="gqa_attention",
    )(q, kv)


def _diff_kernel(q_ref, kv_ref, lam_ref, sg_ref, o_ref, vt_ref, *, lam_init):
    d = B_HEAD_DIM
    kw = B_HEADS * 2 * d
    _fill_vt(vt_ref, kv_ref, kw)
    lf = lam_ref[...]
    lam = (jnp.exp(jnp.sum(lf[0:1] * lf[1:2], axis=-1, keepdims=True))
           - jnp.exp(jnp.sum(lf[2:3] * lf[3:4], axis=-1, keepdims=True)) + lam_init)
    qt = _transpose_bf16(q_ref[...])

    qts = [qt[r * d:(r + 1) * d] for r in range(2 * B_HEADS)]
    k_piece = lambda r, r0: kv_ref[r0:r0 + ATT_TK, r * d:(r + 1) * d]
    vt_piece = lambda r, r0: vt_ref[(r // 2) * 2 * d:(r // 2 + 1) * 2 * d, r0:r0 + ATT_TK]
    res = _attend(2 * B_HEADS, kv_ref.shape[0], k_piece, qts, vt_piece)
    blocks = []
    for h in range(B_HEADS):
        (o1, l1), (o2, l2) = res[2 * h], res[2 * h + 1]
        ot = o1 * (1.0 / l1) - o2 * (lam / l2)
        ms = jnp.mean(ot * ot, axis=0, keepdims=True)
        blocks.append(((ot * lax.rsqrt(ms + B_SUBLN_EPS)) * sg_ref[...]) * (1.0 - lam_init))
    o_ref[...] = jnp.concatenate(blocks, axis=0).T.astype(o_ref.dtype)


def diff_attention(q, kv, lam, subln_g, *, nb, lq, lk, tq, lam_init):
    nq = lq // tq
    kw = kv.shape[1] // 2
    return pl.pallas_call(
        functools.partial(_diff_kernel, lam_init=lam_init),
        grid=(nb, nq),
        in_specs=[
            pl.BlockSpec((tq, q.shape[1]), lambda b, i: (b * nq + i, 0)),
            pl.BlockSpec((lk, kv.shape[1]), lambda b, i: (b, 0), pipeline_mode=pl.Buffered(1)),
            pl.BlockSpec(lam.shape, lambda b, i: (0, 0)),
            pl.BlockSpec((2 * B_HEAD_DIM, 1), lambda b, i: (0, 0)),
        ],
        out_specs=pl.BlockSpec((tq, q.shape[1]), lambda b, i: (b * nq + i, 0)),
        out_shape=jax.ShapeDtypeStruct(q.shape, BF16),
        scratch_shapes=[pltpu.VMEM((kw, lk), BF16)],
        compiler_params=_cparams("parallel", "arbitrary"),
        name="diff_attention",
    )(q, kv, lam, subln_g.reshape(2 * B_HEAD_DIM, 1))


def _decay(x, ld):
    return jnp.exp(-jnp.abs(x * ld))


def _ret_kernel(ld_ref, q_ref, k_ref, v_ref, *rest, nc, has_s0, cross):
    if has_s0:
        s0_ref, o_ref, st_ref, s_ref = rest
    else:
        o_ref, st_ref, s_ref = rest
    c_len = RET_CHUNK
    h = pl.program_id(1)
    s = pl.program_id(2)
    ld_f = ld_ref[0, h]
    ld_b = ld_ref[1, h]
    qb = q_ref[...].astype(BF16)
    k = k_ref[...] * (C_KEY_DIM ** -0.5)
    vb = v_ref[...].astype(BF16)
    idx = lax.broadcasted_iota(jnp.int32, (c_len, 1), 0).astype(F32)
    full = jnp.full((1, 1), float(c_len), F32)
    tdn = (((0,), (0,)), ((), ()))

    def init_state(d):
        if has_s0:
            s_ref[...] = s0_ref[d]
        else:
            s_ref[...] = jnp.zeros_like(s_ref)

    @pl.when(s < nc)
    def _forward():
        @pl.when(s == 0)
        def _():
            init_state(0)

        row0 = pl.multiple_of(s * c_len, c_len)
        rel = (lax.broadcasted_iota(jnp.int32, (c_len, c_len), 0)
               - lax.broadcasted_iota(jnp.int32, (c_len, c_len), 1)).astype(F32)
        dmat = (jnp.where(rel >= 0, _decay(jnp.maximum(rel, 0.0), ld_f), 0.0)
                + jnp.where(rel <= 0, _decay(jnp.maximum(-rel, 0.0), ld_b), 0.0))
        a = lax.dot_general(qb, k.astype(BF16), (((1,), (1,)), ((), ())), preferred_element_type=F32)
        o = _bdot((a * dmat).astype(BF16), vb)
        if cross:
            o = o + _bdot(qb, s_ref[...].astype(BF16)) * _decay(idx + 1.0, ld_f)
        o_ref[pl.ds(row0, c_len), :] = o
        kd = (k * _decay(c_len - 1.0 - idx, ld_f)).astype(BF16)
        s_ref[...] = s_ref[...] * _decay(full, ld_f) + lax.dot_general(kd, vb, tdn, preferred_element_type=F32)

        @pl.when(s == nc - 1)
        def _():
            st_ref[0] = s_ref[...]

    @pl.when(s >= nc)
    def _backward():
        @pl.when(s == nc)
        def _():
            init_state(1)

        row0 = pl.multiple_of((2 * nc - 1 - s) * c_len, c_len)
        if cross:
            o_ref[pl.ds(row0, c_len), :] += _bdot(qb, s_ref[...].astype(BF16)) * _decay(c_len - idx, ld_b)
        kd = (k * _decay(idx, ld_b)).astype(BF16)
        s_ref[...] = s_ref[...] * _decay(full, ld_b) + lax.dot_general(kd, vb, tdn, preferred_element_type=F32)

        @pl.when(s == 2 * nc - 1)
        def _():
            st_ref[1] = s_ref[...]


def retention(proj, log_decay, s0, *, row0, nb, seq):
    c_len = RET_CHUNK
    nc = seq // c_len
    r0 = row0 // c_len
    kblk = (C_HEADS * C_KEY_DIM) // C_KEY_DIM
    vblk = (2 * C_HEADS * C_KEY_DIM) // C_VAL_DIM

    def chunk(b, s):
        return r0 + b * nc + jnp.where(s < nc, s, 2 * nc - 1 - s)

    in_specs = [
        pl.BlockSpec(memory_space=pltpu.SMEM),
        pl.BlockSpec((c_len, C_KEY_DIM), lambda b, h, s: (chunk(b, s), h)),
        pl.BlockSpec((c_len, C_KEY_DIM), lambda b, h, s: (chunk(b, s), kblk + h)),
        pl.BlockSpec((c_len, C_VAL_DIM), lambda b, h, s: (chunk(b, s), vblk + h)),
    ]
    args = [log_decay, proj, proj, proj]
    if s0 is not None:
        in_specs.append(pl.BlockSpec((None, 2, None, C_KEY_DIM, C_VAL_DIM), lambda b, h, s: (b, 0, h, 0, 0)))
        args.append(s0)
    return pl.pallas_call(
        functools.partial(_ret_kernel, nc=nc, has_s0=s0 is not None, cross=(s0 is not None) or nc > 1),
        grid=(nb, C_HEADS, 2 * nc),
        in_specs=in_specs,
        out_specs=[
            pl.BlockSpec((seq, C_VAL_DIM), lambda b, h, s: (b, h)),
            pl.BlockSpec((None, 2, None, C_KEY_DIM, C_VAL_DIM), lambda b, h, s: (b, 0, h, 0, 0)),
        ],
        out_shape=[
            jax.ShapeDtypeStruct((nb * seq, C_HEADS * C_VAL_DIM), F32),
            jax.ShapeDtypeStruct((nb, 2, C_HEADS, C_KEY_DIM, C_VAL_DIM), F32),
        ],
        scratch_shapes=[pltpu.VMEM((C_KEY_DIM, C_VAL_DIM), F32)],
        compiler_params=_cparams("parallel", "parallel", "arbitrary"),
        name="retention",
    )(*args)


def _ret_out_kernel(oc_ref, ol_ref, g_ref, gn_ref, w_ref, x_ref, gate_ref, o_ref, *, tm):
    o = _pick_group(oc_ref, ol_ref, tm)
    parts = []
    for h in range(C_HEADS):
        oh = o[:, h * C_VAL_DIM:(h + 1) * C_VAL_DIM]
        ms = jnp.mean(oh * oh, axis=-1, keepdims=True)
        parts.append((oh * lax.rsqrt(ms + NORM_EPS)) * gn_ref[:, h * C_VAL_DIM:(h + 1) * C_VAL_DIM])
    a = (_silu(g_ref[...]) * jnp.concatenate(parts, axis=1)).astype(BF16)
    o_ref[...] = x_ref[...] + gate_ref[...] * _bdot(a, w_ref[...])


def retention_out(o_ctx, o_lat, proj, gn_g, w, x, gate, *, tm=512):
    t, d = x.shape
    vd = C_HEADS * C_VAL_DIM
    gblk = proj.shape[1] // vd - 1
    seg = lambda i: (_seg_of_tile(i, tm), 0, 0)
    return pl.pallas_call(
        functools.partial(_ret_out_kernel, tm=tm),
        grid=(t // tm,),
        in_specs=_group_specs(o_ctx, o_lat, tm, vd) + [
            pl.BlockSpec((tm, vd), lambda i: (i, gblk)),
            pl.BlockSpec((1, vd), lambda i: (0, 0)),
            pl.BlockSpec((vd, d), lambda i: (0, 0)),
            pl.BlockSpec((tm, d), lambda i: (i, 0)),
            pl.BlockSpec((None, 1, d), seg),
        ],
        out_specs=pl.BlockSpec((tm, d), lambda i: (i, 0)),
        out_shape=jax.ShapeDtypeStruct((t, d), F32),
        compiler_params=_cparams("parallel"),
        name="retention_out",
    )(o_ctx, o_lat, proj, gn_g.reshape(1, vd), w, x, gate)


def _filter_rows(t, seq, band_ref, w1_ref, b1_ref, w2_ref, b2_ref, w3_ref, fr_ref, delta_ref):
    t_norm = t / max(seq - 1, 1)
    lane = lax.broadcasted_iota(jnp.int32, (1, 128), 1)
    ang = (2.0 * math.pi * t) * band_ref[...] / seq
    feat = jnp.where(lane == 0, t_norm,
                     jnp.where(lane <= D_BANDS, jnp.cos(ang), jnp.where(lane <= 2 * D_BANDS, -jnp.sin(ang), 0.0)))
    a = jnp.sin(fr_ref[0:1] * (_bdot(feat.astype(BF16), w1_ref[...].astype(BF16)) + b1_ref[...]))
    a = jnp.sin(fr_ref[1:2] * (_bdot(a.astype(BF16), w2_ref[...].astype(BF16)) + b2_ref[...]))
    f = _bdot(a.astype(BF16), w3_ref[...].astype(BF16))
    window = jnp.exp(-t_norm * delta_ref[...]) + D_MOD_SHIFT
    return f * jnp.concatenate([window] * 4, axis=1)


def _filter_sum_kernel(band_ref, w1_ref, b1_ref, w2_ref, b2_ref, w3_ref, fr_ref, delta_ref, o_ref, *, seq, tm):
    i = pl.program_id(0)
    t = (i * tm + lax.broadcasted_iota(jnp.int32, (tm, 1), 0)).astype(F32)
    f = _filter_rows(t, seq, band_ref, w1_ref, b1_ref, w2_ref, b2_ref, w3_ref, fr_ref, delta_ref)
    part = jnp.sum(jnp.abs(f), axis=0, keepdims=True)

    @pl.when(i == 0)
    def _():
        o_ref[...] = part

    @pl.when(i > 0)
    def _():
        o_ref[...] += part


def _filter_gen_kernel(band_ref, w1_ref, b1_ref, w2_ref, b2_ref, w3_ref, fr_ref, delta_ref, sum_ref, o_ref, *, seq, tm):
    i = pl.program_id(0)
    r = i * tm + lax.broadcasted_iota(jnp.int32, (tm, 1), 0)
    t = jnp.where(r < seq, r, 2 * seq - r).astype(F32)
    f = _filter_rows(t, seq, band_ref, w1_ref, b1_ref, w2_ref, b2_ref, w3_ref, fr_ref, delta_ref)
    f = f / (sum_ref[...] + 1e-6)
    half = 2 * D_MODEL
    fwd, bwd = f[:, :half], f[:, half:]
    o_ref[...] = jnp.where(r < seq, fwd, jnp.where(r > seq, bwd, 0.0)) + jnp.where(r == 0, bwd, 0.0)


def hyena_circular_kernel(seq, band, w1p, b1, w2, b2, w3, freq, delta, *, tm):
    consts = [band, w1p, b1, w2, b2, w3, freq, delta]
    cspecs = [pl.BlockSpec(a.shape, lambda i: (0, 0)) for a in consts]
    nf = w3.shape[1]
    sums = pl.pallas_call(
        functools.partial(_filter_sum_kernel, seq=seq, tm=tm),
        grid=(seq // tm,),
        in_specs=cspecs,
        out_specs=pl.BlockSpec((1, nf), lambda i: (0, 0)),
        out_shape=jax.ShapeDtypeStruct((1, nf), F32),
        compiler_params=_cparams("arbitrary"),
        name="hyena_filter_sum",
    )(*consts)
    return pl.pallas_call(
        functools.partial(_filter_gen_kernel, seq=seq, tm=tm),
        grid=(2 * seq // tm,),
        in_specs=cspecs + [pl.BlockSpec((1, nf), lambda i: (0, 0))],
        out_specs=pl.BlockSpec((tm, nf // 2), lambda i: (i, 0)),
        out_shape=jax.ShapeDtypeStruct((2 * seq, nf // 2), F32),
        compiler_params=_cparams("parallel"),
        name="hyena_filter_gen",
    )(*consts, sums)


def _dft_cs(rows, cols, n):
    m = np.outer(np.arange(rows), np.arange(cols)) % n
    ang = 2.0 * np.pi * m / n
    return np.cos(ang), np.sin(ang)


def _stack_fwd(c, s):
    return np.block([[c, s], [-s, c]])


def _stack_inv(c, s):
    return np.block([[c, -s], [s, c]])


def _dft_mats():
    as_bf16 = lambda a: jnp.asarray(a, F32).astype(BF16)
    c1, s1 = _dft_cs(LAT_N1, LAT_N1, LAT_N1)
    c2, s2 = _dft_cs(LAT_N2, LAT_N2, LAT_N2)
    cc, sc = _dft_cs(CTX_N, CTX_N, CTX_N)
    h1, hc = LAT_N1 // 2, CTX_N // 2
    return dict(
        lat_g1=as_bf16(_kron_rows(_stack_fwd(c1[:, :h1], s1[:, :h1]))),
        lat_g1r=as_bf16(_kron_rows(np.concatenate([c1, -s1], axis=0))),
        lat_g2=as_bf16(_stack_fwd(c2, s2)),
        lat_g2i=as_bf16(_stack_inv(c2, s2)),
        lat_g1i=as_bf16(_kron_rows(_stack_inv(c1[:h1], s1[:h1]) / LAT_N)),
        ctx_g=as_bf16(_stack_fwd(cc[:, :hc], sc[:, :hc])),
        ctx_gr=as_bf16(np.concatenate([cc, -sc], axis=0)),
        ctx_gi=as_bf16(_stack_inv(cc[:hc], sc[:hc]) / CTX_N),
    )


def _lmul_kernel(g_ref, x_ref, o_ref):
    o_ref[...] = _bdot(g_ref[...], x_ref[...].astype(BF16))


def left_matmul(g, x, *, row_blk=0, tc):
    m, k = g.shape
    n = x.shape[1]
    return pl.pallas_call(
        _lmul_kernel,
        grid=(n // tc,),
        in_specs=[pl.BlockSpec((m, k), lambda j: (0, 0)), pl.BlockSpec((k, tc), lambda j: (row_blk, j))],
        out_specs=pl.BlockSpec((m, tc), lambda j: (0, j)),
        out_shape=jax.ShapeDtypeStruct((m, n), F32),
        compiler_params=_cparams("parallel"),
        name="left_matmul",
    )(g, x)


SLAB_ROWS = 8


def _kron_rows(g):
    return np.kron(g, np.eye(SLAB_ROWS))


def _slab_dot(g_ref, x_ref):
    k, r, w = x_ref.shape
    return _bdot(g_ref[...], x_ref[...].reshape(k * r, w).astype(BF16))


def _lmul_slab_kernel(g_ref, x_ref, o_ref):
    o_ref[...] = _slab_dot(g_ref, x_ref).reshape(o_ref.shape)


def left_matmul_slabs(g8, x3, *, row_blk=0):
    m, k = g8.shape[0] // SLAB_ROWS, g8.shape[1] // SLAB_ROWS
    _, s, w = x3.shape
    return pl.pallas_call(
        _lmul_slab_kernel,
        grid=(s // SLAB_ROWS,),
        in_specs=[pl.BlockSpec(g8.shape, lambda j: (0, 0)),
                  pl.BlockSpec((k, SLAB_ROWS, w), lambda j: (row_blk, j, 0))],
        out_specs=pl.BlockSpec((m, SLAB_ROWS, w), lambda j: (0, j, 0)),
        out_shape=jax.ShapeDtypeStruct((m, s, w), F32),
        compiler_params=_cparams("parallel"),
        name="left_matmul_slabs",
    )(g8, x3)


def _gated_skip(conv, gate, y, skip_ref):
    return gate * (conv + skip_ref[...] * y)


def _lat_last_kernel(g_ref, b_ref, gate_ref, y_ref, skip_ref, o_ref):
    conv = _slab_dot(g_ref, b_ref).reshape(o_ref.shape)
    o_ref[...] = _gated_skip(conv, gate_ref[...], y_ref[...], skip_ref).astype(o_ref.dtype)


def lat_last_stage(g, b3, gate3, gate_blk, y3, y_blk, skip, *, out_dtype):
    m, k = g.shape[0] // SLAB_ROWS, g.shape[1] // SLAB_ROWS
    _, s, w = b3.shape
    return pl.pallas_call(
        _lat_last_kernel,
        grid=(s // SLAB_ROWS,),
        in_specs=[
            pl.BlockSpec(g.shape, lambda j: (0, 0)),
            pl.BlockSpec((k, SLAB_ROWS, w), lambda j: (0, j, 0)),
            pl.BlockSpec((m, SLAB_ROWS, w), lambda j: (gate_blk, j, 0)),
            pl.BlockSpec((m, SLAB_ROWS, w), lambda j: (y_blk, j, 0)),
            pl.BlockSpec((1, w), lambda j: (0, 0)),
        ],
        out_specs=pl.BlockSpec((m, SLAB_ROWS, w), lambda j: (0, j, 0)),
        out_shape=jax.ShapeDtypeStruct((m, s, w), out_dtype),
        compiler_params=_cparams("parallel"),
        name="hyena_lat_last",
    )(g, b3, gate3, y3, skip)


def _lat_mid_kernel(a_ref, *rest, conv):
    if conv:
        h_ref, g2_ref, g2i_ref, o_ref = rest
    else:
        g2_ref, o_ref = rest
    k1 = pl.program_id(0)
    n2 = lax.broadcasted_iota(jnp.int32, (LAT_N2, 1), 0)
    ang = (k1 * n2).astype(F32) * (2.0 * math.pi / LAT_N)
    c, s = jnp.cos(ang), jnp.sin(ang)
    ar, ai = a_ref[0], a_ref[1]
    t = jnp.concatenate([ar * c + ai * s, ai * c - ar * s], axis=0).astype(BF16)
    x = _bdot(g2_ref[...], t)
    xr, xi = x[:LAT_N2], x[LAT_N2:]
    if not conv:
        o_ref[0] = xr
        o_ref[1] = xi
        return
    hr, hi = h_ref[0], h_ref[1]
    y = jnp.concatenate([xr * hr - xi * hi, xr * hi + xi * hr], axis=0).astype(BF16)
    b = _bdot(g2i_ref[...], y)
    br, bi = b[:LAT_N2], b[LAT_N2:]
    o_ref[0] = br * c - bi * s
    o_ref[1] = bi * c + br * s


def lat_mid_stage(a, mats, h=None, order=0):
    w = a.shape[-1]
    blk = lambda width, col: pl.BlockSpec((2, None, LAT_N2, width), lambda k1, j: (0, k1, 0, col(j)))
    gspec = pl.BlockSpec((2 * LAT_N2, 2 * LAT_N2), lambda k1, j: (0, 0))
    if h is None:
        in_specs = [blk(D_MODEL, lambda j: j), gspec]
        args = [a, mats["lat_g2"]]
    else:
        in_specs = [blk(D_MODEL, lambda j: j), blk(D_MODEL, lambda j: order), gspec, gspec]
        args = [a, h, mats["lat_g2"], mats["lat_g2i"]]
    return pl.pallas_call(
        functools.partial(_lat_mid_kernel, conv=h is not None),
        grid=(LAT_N1, w // D_MODEL),
        in_specs=in_specs,
        out_specs=blk(D_MODEL, lambda j: j),
        out_shape=jax.ShapeDtypeStruct(a.shape, F32),
        compiler_params=_cparams("parallel", "arbitrary"),
        name="hyena_lat_mid",
    )(*args)


def _ctx_conv_kernel(y_ref, gate_ref, h_ref, g_ref, gi_ref, skip_ref, o_ref):
    n = CTX_N
    y = y_ref[...]
    z = _bdot(g_ref[...], y.astype(BF16))
    zr, zi = z[:n], z[n:]
    hr, hi = h_ref[0:n], h_ref[n:]
    w = jnp.concatenate([zr * hr - zi * hi, zr * hi + zi * hr], axis=0).astype(BF16)
    conv = _bdot(gi_ref[...], w)
    o_ref[...] = _gated_skip(conv, gate_ref[...], y, skip_ref).astype(o_ref.dtype)


def ctx_conv(y, y_plane, gate, gate_plane, h, order, mats, skip, *, out_dtype):
    rows = 2 * SEQ
    return pl.pallas_call(
        _ctx_conv_kernel,
        grid=(BATCH // 2,),
        in_specs=[
            pl.BlockSpec((None, rows, D_MODEL), lambda p: (y_plane, p, 0)),
            pl.BlockSpec((None, rows, D_MODEL), lambda p: (gate_plane, p, 0)),
            pl.BlockSpec((2 * CTX_N, D_MODEL), lambda p: (0, order)),
            pl.BlockSpec((2 * CTX_N, rows), lambda p: (0, 0)),
            pl.BlockSpec((rows, 2 * CTX_N), lambda p: (0, 0)),
            pl.BlockSpec((1, D_MODEL), lambda p: (0, 0)),
        ],
        out_specs=pl.BlockSpec((rows, D_MODEL), lambda p: (p, 0)),
        out_shape=jax.ShapeDtypeStruct((T_CTX, D_MODEL), out_dtype),
        compiler_params=_cparams("parallel"),
        name="hyena_ctx_conv",
    )(y, gate, h, mats["ctx_g"], mats["ctx_gi"], skip)


def hyena_core(z3, f_w1, f_b1, f_w2, f_b2, f_w3, f_freq, f_skip):
    mats = _dft_mats()
    bands = jnp.linspace(1e-4, D_BANDS - 1, D_BANDS, dtype=F32)
    band = jnp.zeros((1, 128), F32).at[0, 1:1 + D_BANDS].set(bands).at[0, 1 + D_BANDS:1 + 2 * D_BANDS].set(bands)
    w1p = jnp.zeros((128, D_FILTER_HIDDEN), F32).at[:D_EMB].set(f_w1)
    max_decay = math.log(D_DECAY_TARGET) / D_FAST_DECAY_PCT
    min_decay = math.log(D_DECAY_TARGET) / D_SLOW_DECAY_PCT
    delta = jnp.abs(jnp.linspace(min_decay, max_decay, D_MODEL, dtype=F32)).reshape(1, D_MODEL)
    fargs = (band, w1p, f_b1.reshape(1, -1), f_w2, f_b2.reshape(1, -1), f_w3, f_freq, delta)

    kc_ctx = hyena_circular_kernel(SEQ, *fargs, tm=SEQ)
    h_ctx = left_matmul(mats["ctx_gr"], kc_ctx, tc=D_MODEL)
    kc_lat = hyena_circular_kernel(DEC_SEQ, *fargs, tm=512)
    a = left_matmul_slabs(mats["lat_g1r"], kc_lat.reshape(LAT_N1, LAT_N2, 2 * D_MODEL))
    h_lat = lat_mid_stage(a.reshape(2, LAT_N1, LAT_N2, 2 * D_MODEL), mats)

    y1 = ctx_conv(z3, 2, z3, 0, h_ctx, 0, mats, f_skip[0:1], out_dtype=F32)
    y_ctx = ctx_conv(y1[None], 0, z3, 1, h_ctx, 1, mats, f_skip[1:2], out_dtype=BF16)

    slabs = DEC_BATCH * LAT_N1 // 2
    z_slabs = z3.reshape(3 * T_ALL // LAT_N2, LAT_N2, D_MODEL)
    plane_blks = T_ALL // LAT_N2 // slabs
    lat_blk = lambda plane: plane * plane_blks + T_CTX // LAT_N2 // slabs
    y, y_blk = z_slabs, lat_blk(2)
    for n in range(2):
        a = left_matmul_slabs(mats["lat_g1"], y, row_blk=y_blk)
        b = lat_mid_stage(a.reshape(2, LAT_N1, LAT_N2, D_MODEL), mats, h_lat, n)
        y = lat_last_stage(mats["lat_g1i"], b.reshape(2 * LAT_N1, LAT_N2, D_MODEL), z_slabs, lat_blk(n), y, y_blk,
                           f_skip[n:n + 1], out_dtype=F32)
        y_blk = 0
    return y_ctx, y.reshape(T_LAT, D_MODEL)


def kernel(x_prompt, x_sample, cache_attn_k, cache_attn_v, cache_diff_k, cache_diff_v, state_ret, c, c_ctx, w_mod, b_mod, norm1_g, norm2_g, final_g, attn_w_qkv, attn_q_g, attn_k_g, attn_w_o, diff_w_qkv, diff_lambda, diff_subln_g, diff_w_o, ret_w_in, ret_log_decay, ret_gn_g, ret_w_o, hyena_w_in, hyena_sc_w, hyena_sc_b, hyena_f_w1, hyena_f_b1, hyena_f_w2, hyena_f_b2, hyena_f_w3, hyena_f_freq, hyena_f_skip, hyena_w_o, ffn_w_up, ffn_conv_w, ffn_conv_b, ffn_w_down):
    d = D_MODEL
    x = jnp.concatenate([x_prompt.reshape(T_CTX, d), x_sample.reshape(T_LAT, d)], axis=0)
    cond = jnp.zeros((SEG_ROWS, d), F32).at[0].set(c_ctx).at[1:N_SEG].set(c)
    mod = modulation(cond, w_mod, b_mod)
    mod = mod.reshape(DEPTH, SEG_ROWS, 6, d).transpose(0, 2, 1, 3).reshape(DEPTH, 6, SEG_ROWS, 1, d)
    rope = rope_tables()
    bf = lambda w: w.astype(BF16)
    out = {}

    for l in range(DEPTH):
        m, j = l % 4, l // 4
        sh1, sc1, g1, sh2, sc2, g2 = (mod[l, i] for i in range(6))
        if m == 0:
            nq, nk = A_HEADS * A_HEAD_DIM, A_KV_HEADS * A_HEAD_DIM
            scale = A_HEAD_DIM ** -0.5 * LOG2E
            qkv = norm_proj(x, norm1_g[l], sh1, sc1, bf(attn_w_qkv[j]))
            q_c, kv_c, kt_c, vt_c = qk_prep(qkv, row0=0, rows=T_CTX, nq=nq, nk=nk, q_g=attn_q_g[j], k_g=attn_k_g[j],
                                            scale=scale, cache="seq_minor")
            q_l, kv_l = qk_prep(qkv, row0=T_CTX, rows=T_LAT, nq=nq, nk=nk, q_g=attn_q_g[j], k_g=attn_k_g[j],
                                rope=rope, scale=scale)
            as_cache = lambda t: t.reshape(BATCH, A_KV_HEADS, A_HEAD_DIM, SEQ).transpose(0, 3, 1, 2)[:, None]
            out["attn_k"] = as_cache(kt_c)
            out["attn_v"] = as_cache(vt_c)
            cache = jnp.concatenate([cache_attn_k[:, j].reshape(DEC_BATCH, PAST_LEN, nk),
                                     cache_attn_v[:, j].reshape(DEC_BATCH, PAST_LEN, nk)], axis=-1).astype(BF16)
            lk = PAST_LEN + DEC_SEQ
            kv_all = jnp.concatenate([cache, kv_l.reshape(DEC_BATCH, DEC_SEQ, 2 * nk)], axis=1).reshape(DEC_BATCH * lk, 2 * nk)
            o_c = gqa_attention(q_c, kv_c, nb=BATCH, lq=SEQ, lk=SEQ, tq=SEQ)
            o_l = gqa_attention(q_l, kv_all, nb=DEC_BATCH, lq=DEC_SEQ, lk=lk, tq=128)
            x = out_proj_residual(o_c, o_l, bf(attn_w_o[j]), x, g1)
        elif m == 1:
            nq = nk = B_HEADS * 2 * B_HEAD_DIM
            scale = B_HEAD_DIM ** -0.5 * LOG2E
            lam_init = 0.8 - 0.6 * math.exp(-0.3 * l)
            qkv = norm_proj(x, norm1_g[l], sh1, sc1, bf(diff_w_qkv[j]))
            q_c, kv_c, k3_c, v3_c = qk_prep(qkv, row0=0, rows=T_CTX, nq=nq, nk=nk, scale=scale, cache="head_tile")
            q_l, kv_l = qk_prep(qkv, row0=T_CTX, rows=T_LAT, nq=nq, nk=nk, rope=rope, scale=scale)
            out["diff_k"] = k3_c.reshape(BATCH, 1, SEQ, B_HEADS, 2 * B_HEAD_DIM)
            out["diff_v"] = v3_c.reshape(BATCH, 1, SEQ, B_HEADS, 2 * B_HEAD_DIM)
            cache = jnp.concatenate([cache_diff_k[:, j].reshape(DEC_BATCH, PAST_LEN, nk),
                                     cache_diff_v[:, j].reshape(DEC_BATCH, PAST_LEN, nk)], axis=-1).astype(BF16)
            lk = PAST_LEN + DEC_SEQ
            kv_all = jnp.concatenate([cache, kv_l.reshape(DEC_BATCH, DEC_SEQ, 2 * nk)], axis=1).reshape(DEC_BATCH * lk, 2 * nk)
            dargs = (diff_lambda[j], diff_subln_g[j])
            o_c = diff_attention(q_c, kv_c, *dargs, nb=BATCH, lq=SEQ, lk=SEQ, tq=SEQ, lam_init=lam_init)
            o_l = diff_attention(q_l, kv_all, *dargs, nb=DEC_BATCH, lq=DEC_SEQ, lk=lk, tq=256, lam_init=lam_init)
            x = out_proj_residual(o_c, o_l, bf(diff_w_o[j]), x, g1)
        elif m == 2:
            proj = norm_proj(x, norm1_g[l], sh1, sc1, bf(ret_w_in[j]))
            o_c, st = retention(proj, ret_log_decay[j], None, row0=0, nb=BATCH, seq=SEQ)
            o_l, _ = retention(proj, ret_log_decay[j], state_ret[:, j], row0=T_CTX, nb=DEC_BATCH, seq=DEC_SEQ)
            out["ret_s"] = st.reshape(BATCH, 1, 2, C_HEADS, C_KEY_DIM, C_VAL_DIM)
            x = retention_out(o_c, o_l, proj, ret_gn_g[j], bf(ret_w_o[j]), x, g1)
        else:
            z3 = norm_proj_conv(x, norm1_g[l], sh1, sc1, bf(hyena_w_in[j]), hyena_sc_w[j], hyena_sc_b[j])
            y_c, y_l = hyena_core(z3, hyena_f_w1[j], hyena_f_b1[j], hyena_f_w2[j], hyena_f_b2[j], hyena_f_w3[j],
                                  hyena_f_freq[j], hyena_f_skip[j])
            x = out_proj_residual(y_c, y_l, bf(hyena_w_o[j]), x, g1)
        x = conv_ffn(x, norm2_g[l], sh2, sc2, g2, bf(ffn_w_up[l]), ffn_conv_w[l], ffn_conv_b[l], bf(ffn_w_down[l]),
                     final_g if l == DEPTH - 1 else None)

    y_prompt = x[:T_CTX].reshape(BATCH, SEQ, d)
    y_sample = x[T_CTX:].reshape(DEC_BATCH, DEC_SEQ, d)
    return (y_prompt, y_sample, out["attn_k"], out["attn_v"], out["diff_k"], out["diff_v"], out["ret_s"])
```

```python
import functools
import math

import jax
import jax.numpy as jnp
import numpy as np
from jax import lax
from jax.experimental import pallas as pl
from jax.experimental.pallas import tpu as pltpu

F32 = jnp.float32
BF16 = jnp.bfloat16

D_MODEL = 1024
BATCH = 32
SEQ = 256
DEPTH = 4
DEC_BATCH = 2
DEC_SEQ = 4096
PAST_LEN = 256
GRID_W = 64
ROPE_THETA = 10000.0
NORM_EPS = 1e-6
A_HEADS = 16
A_KV_HEADS = 4
A_HEAD_DIM = 64
A_GROUP = A_HEADS // A_KV_HEADS
B_HEADS = 8
B_HEAD_DIM = 64
B_SUBLN_EPS = 1e-5
C_HEADS = 4
C_KEY_DIM = 256
C_VAL_DIM = 512
D_BANDS = 16
D_EMB = 1 + 2 * D_BANDS
D_FILTER_HIDDEN = 64
D_FAST_DECAY_PCT = 0.3
D_SLOW_DECAY_PCT = 1.5
D_DECAY_TARGET = 1e-2
D_MOD_SHIFT = 0.05
FFN_DIM = 2816

T_CTX = BATCH * SEQ
T_LAT = DEC_BATCH * DEC_SEQ
T_ALL = T_CTX + T_LAT
N_SEG = 1 + DEC_BATCH
SEG_ROWS = 8

HALO = 16
RET_CHUNK = 256
VMEM_LIMIT = 56 * 1024 * 1024

LAT_N = 2 * DEC_SEQ
LAT_N1 = 64
LAT_N2 = LAT_N // LAT_N1
CTX_N = 2 * SEQ


def _cparams(*sem):
    return pltpu.CompilerParams(dimension_semantics=sem, vmem_limit_bytes=VMEM_LIMIT)


def _seg_of_tile(i, tm):
    start = i * tm
    return jnp.where(start < T_CTX, 0, 1 + (start - T_CTX) // DEC_SEQ)


def _silu(x):
    return x * jax.nn.sigmoid(x)


def _bdot(a, b):
    return jnp.dot(a, b, preferred_element_type=F32)


def _norm_mod(x, g, sh, sc):
    ms = jnp.mean(x * x, axis=-1, keepdims=True)
    y = (x * lax.rsqrt(ms + NORM_EPS)) * g
    return y * (1.0 + sc) + sh


def _mod_kernel(c_ref, w_ref, b_ref, o_ref):
    s = _silu(c_ref[...]).astype(BF16)
    o_ref[...] = _bdot(s, w_ref[...].astype(BF16)) + b_ref[...]


def modulation(cond, w_mod, b_mod):
    tn = 1536
    n = w_mod.shape[-1]
    return pl.pallas_call(
        _mod_kernel,
        grid=(DEPTH, n // tn),
        in_specs=[
            pl.BlockSpec((SEG_ROWS, D_MODEL), lambda l, j: (0, 0)),
            pl.BlockSpec((None, D_MODEL, tn), lambda l, j: (l, 0, j)),
            pl.BlockSpec((None, 1, tn), lambda l, j: (l, 0, j)),
        ],
        out_specs=pl.BlockSpec((None, SEG_ROWS, tn), lambda l, j: (l, 0, j)),
        out_shape=jax.ShapeDtypeStruct((DEPTH, SEG_ROWS, n), F32),
        compiler_params=_cparams("arbitrary", "arbitrary"),
        name="modulation",
    )(cond, w_mod, b_mod.reshape(DEPTH, 1, n))


def _proj_kernel(xc_ref, xl_ref, g_ref, sh_ref, sc_ref, w_ref, o_ref, h_ref, *, tm):
    @pl.when(pl.program_id(1) == 0)
    def _():
        x = _pick_group(xc_ref, xl_ref, tm)
        h_ref[...] = _norm_mod(x, g_ref[...], sh_ref[...], sc_ref[...]).astype(BF16)

    o_ref[...] = _bdot(h_ref[...], w_ref[...]).astype(o_ref.dtype)


def norm_proj(x_ctx, x_lat, g, sh, sc, w, *, tm=1024, tn=1536, out_dtype=F32):
    d, n = w.shape
    seg = lambda i, j: (_seg_of_tile(i, tm), 0, 0)
    return pl.pallas_call(
        functools.partial(_proj_kernel, tm=tm),
        grid=(T_ALL // tm, n // tn),
        in_specs=_group_specs(x_ctx, x_lat, tm, d) + [
            pl.BlockSpec((1, d), lambda i, j: (0, 0)),
            pl.BlockSpec((None, 1, d), seg),
            pl.BlockSpec((None, 1, d), seg),
            pl.BlockSpec((d, tn), lambda i, j: (0, j)),
        ],
        out_specs=pl.BlockSpec((tm, tn), lambda i, j: (i, j)),
        out_shape=jax.ShapeDtypeStruct((T_ALL, n), out_dtype),
        scratch_shapes=[pltpu.VMEM((tm, d), BF16)],
        compiler_params=_cparams("parallel", "arbitrary"),
        name="norm_proj",
    )(x_ctx, x_lat, g.reshape(1, d), sh, sc, w)


def _conv3(u, cw, cb, i, tm):
    rows = u.shape[0]
    up = pltpu.roll(u, 1, 0)[HALO:HALO + tm]
    uc = u[HALO:HALO + tm]
    un = pltpu.roll(u, rows - 1, 0)[HALO:HALO + tm]
    sub = lax.broadcasted_iota(jnp.int32, (8, 1), 0)
    is_ctx = i * tm < T_CTX
    ups, uns = [], []
    for r in range(0, tm, SEQ):
        start = i * tm + r
        first = jnp.logical_or(is_ctx, (start & (DEC_SEQ - 1)) == 0)
        last = jnp.logical_or(is_ctx, ((start + SEQ) & (DEC_SEQ - 1)) == 0)
        ups += [jnp.where(jnp.logical_and(sub == 0, first), 0.0, up[r:r + 8]), up[r + 8:r + SEQ]]
        uns += [un[r:r + SEQ - 8], jnp.where(jnp.logical_and(sub == 7, last), 0.0, un[r + SEQ - 8:r + SEQ])]
    up = jnp.concatenate(ups, axis=0)
    un = jnp.concatenate(uns, axis=0)
    return up * cw[0:1] + uc * cw[1:2] + un * cw[2:3] + cb


def _fill_h(h_ref, xp_ref, x_ref, xn_ref, g_ref, sh_ref, sc_ref, tm):
    g, sh, sc = g_ref[...], sh_ref[...], sc_ref[...]
    h_ref[0:HALO] = _norm_mod(xp_ref[...], g, sh, sc).astype(BF16)
    h_ref[HALO:HALO + tm] = _norm_mod(x_ref[...], g, sh, sc).astype(BF16)
    h_ref[HALO + tm:] = _norm_mod(xn_ref[...], g, sh, sc).astype(BF16)


def _proj_conv_kernel(xp_ref, x_ref, xn_ref, g_ref, sh_ref, sc_ref, w_ref, cw_ref, cb_ref, o_ref, h_ref, *, tm):
    i = pl.program_id(0)

    @pl.when(pl.program_id(1) == 0)
    def _():
        _fill_h(h_ref, xp_ref, x_ref, xn_ref, g_ref, sh_ref, sc_ref, tm)

    u = _bdot(h_ref[...], w_ref[...])
    o_ref[...] = _conv3(u, cw_ref[...], cb_ref[...], i, tm)


def _ffn_kernel(xp_ref, x_ref, xn_ref, g_ref, sh_ref, sc_ref, gate_ref, wu_ref, cw_ref, cb_ref, wd_ref, *rest, tm, tn,
                final):
    if final:
        fg_ref, oc_ref, ol_ref = rest
    else:
        (o_ref,) = rest
    i = pl.program_id(0)
    g, sh, sc = g_ref[...], sh_ref[...], sc_ref[...]
    x = x_ref[...]
    h = jnp.concatenate([_norm_mod(xp_ref[...], g, sh, sc).astype(BF16), _norm_mod(x, g, sh, sc).astype(BF16),
                         _norm_mod(xn_ref[...], g, sh, sc).astype(BF16)], axis=0)
    f = wd_ref.shape[0]
    acts = []
    for c in range(0, f, tn):
        a = _conv3(_bdot(h, wu_ref[:, c:c + tn]), cw_ref[:, c:c + tn], cb_ref[:, c:c + tn], i, tm)
        b = _conv3(_bdot(h, wu_ref[:, f + c:f + c + tn]), cw_ref[:, f + c:f + c + tn], cb_ref[:, f + c:f + c + tn], i, tm)
        acts.append((_silu(a) * b).astype(BF16))
    y = x + gate_ref[...] * _bdot(jnp.concatenate(acts, axis=1), wd_ref[...])
    if not final:
        o_ref[...] = y
        return
    ms = jnp.mean(y * y, axis=-1, keepdims=True)
    y = (y * lax.rsqrt(ms + NORM_EPS)) * fg_ref[...]
    is_ctx = i < T_CTX // tm

    @pl.when(is_ctx)
    def _():
        oc_ref[...] = y

    @pl.when(jnp.logical_not(is_ctx))
    def _():
        ol_ref[...] = y


def _halo_specs(t, tm, d):
    per = tm // HALO
    last_blk = t // HALO - 1
    return [
        pl.BlockSpec((HALO, d), lambda i, *_: (jnp.maximum(i * per - 1, 0), 0)),
        pl.BlockSpec((tm, d), lambda i, *_: (i, 0)),
        pl.BlockSpec((HALO, d), lambda i, *_: (jnp.minimum((i + 1) * per, last_blk), 0)),
    ]


def norm_proj_conv(x, g, sh, sc, w, cw, cb, *, tm=1024, tn=512):
    t, d = x.shape
    n = w.shape[1]
    per_group = d // tn
    seg = lambda i, j: (_seg_of_tile(i, tm), 0, 0)
    return pl.pallas_call(
        functools.partial(_proj_conv_kernel, tm=tm),
        grid=(t // tm, n // tn),
        in_specs=_halo_specs(t, tm, d) + [
            pl.BlockSpec((1, d), lambda i, j: (0, 0)),
            pl.BlockSpec((None, 1, d), seg),
            pl.BlockSpec((None, 1, d), seg),
            pl.BlockSpec((d, tn), lambda i, j: (0, j)),
            pl.BlockSpec((3, tn), lambda i, j: (0, j)),
            pl.BlockSpec((1, tn), lambda i, j: (0, j)),
        ],
        out_specs=pl.BlockSpec((None, tm, tn), lambda i, j: (j // per_group, i, j % per_group)),
        out_shape=jax.ShapeDtypeStruct((n // d, t, d), F32),
        scratch_shapes=[pltpu.VMEM((tm + 2 * HALO, d), BF16)],
        compiler_params=_cparams("parallel", "arbitrary"),
        name="norm_proj_conv",
    )(x, x, x, g.reshape(1, d), sh, sc, w, cw, cb.reshape(1, n))


def conv_ffn(x, g, sh, sc, gate, w_up, cw, cb, w_down, layer, final_g=None, *, tm=512, tn=256):
    t, d = x.shape
    f = w_down.shape[1]
    seg = lambda i: (_seg_of_tile(i, tm), 0, 0)
    whole = lambda a: pl.BlockSpec(a.shape, lambda i: (0, 0), pipeline_mode=pl.Buffered(1))
    of_layer = lambda a: pl.BlockSpec((None,) + a.shape[1:], lambda i: (layer, 0, 0), pipeline_mode=pl.Buffered(1))
    cb = cb.reshape(1, 2 * f)
    in_specs = _halo_specs(t, tm, d) + [
        pl.BlockSpec((1, d), lambda i: (0, 0)),
        pl.BlockSpec((None, 1, d), seg),
        pl.BlockSpec((None, 1, d), seg),
        pl.BlockSpec((None, 1, d), seg),
        of_layer(w_up), whole(cw), whole(cb), of_layer(w_down),
    ]
    args = [x, x, x, g.reshape(1, d), sh, sc, gate, w_up, cw, cb, w_down]
    out_specs = pl.BlockSpec((tm, d), lambda i: (i, 0))
    out_shape = jax.ShapeDtypeStruct((t, d), F32)
    if final_g is not None:
        nctx = T_CTX // tm
        in_specs.append(pl.BlockSpec((1, d), lambda i: (0, 0)))
        args.append(final_g.reshape(1, d))
        out_specs = [pl.BlockSpec((tm, d), lambda i: (jnp.minimum(i, nctx - 1), 0)),
                     pl.BlockSpec((tm, d), lambda i: (jnp.maximum(i - nctx, 0), 0))]
        out_shape = [jax.ShapeDtypeStruct((T_CTX, d), F32), jax.ShapeDtypeStruct((T_LAT, d), F32)]
    return pl.pallas_call(
        functools.partial(_ffn_kernel, tm=tm, tn=tn, final=final_g is not None),
        grid=(t // tm,),
        in_specs=in_specs,
        out_specs=out_specs,
        out_shape=out_shape,
        compiler_params=_cparams("arbitrary"),
        name="conv_ffn",
    )(*args)


def _group_specs(a_ctx, a_lat, tm, width):
    nctx = T_CTX // tm
    off = nctx if a_lat.shape[0] == T_ALL else 0
    return [
        pl.BlockSpec((tm, width), lambda i, *_: (jnp.minimum(i, nctx - 1), 0)),
        pl.BlockSpec((tm, width), lambda i, *_: (off + jnp.maximum(i - nctx, 0), 0)),
    ]


def _pick_group(ac_ref, al_ref, tm):
    return jnp.where(pl.program_id(0) < T_CTX // tm, ac_ref[...], al_ref[...])


def _out_proj_kernel(ac_ref, al_ref, w_ref, xc_ref, xl_ref, gate_ref, o_ref, *, tm):
    a = _pick_group(ac_ref, al_ref, tm).astype(BF16)
    o_ref[...] = _pick_group(xc_ref, xl_ref, tm) + gate_ref[...] * _bdot(a, w_ref[...])


def out_proj_residual(a_ctx, a_lat, w, x_ctx, x_lat, gate, *, tm=512):
    k, d = w.shape
    seg = lambda i: (_seg_of_tile(i, tm), 0, 0)
    return pl.pallas_call(
        functools.partial(_out_proj_kernel, tm=tm),
        grid=(T_ALL // tm,),
        in_specs=_group_specs(a_ctx, a_lat, tm, k) + [pl.BlockSpec((k, d), lambda i: (0, 0))]
        + _group_specs(x_ctx, x_lat, tm, d) + [pl.BlockSpec((None, 1, d), seg)],
        out_specs=pl.BlockSpec((tm, d), lambda i: (i, 0)),
        out_shape=jax.ShapeDtypeStruct((T_ALL, d), F32),
        compiler_params=_cparams("parallel"),
        name="out_proj_residual",
    )(a_ctx, a_lat, w, x_ctx, x_lat, gate)


def _head_mean_sq(x, head_dim):
    n = x.shape[1]
    x2 = x * x
    hi = x2.astype(BF16)
    lo = (x2 - hi.astype(F32)).astype(BF16)
    blk = 256
    r = lax.broadcasted_iota(jnp.int32, (blk, blk), 0) // head_dim
    c = lax.broadcasted_iota(jnp.int32, (blk, blk), 1) // head_dim
    ones = (r == c).astype(BF16)
    parts = []
    for s in range(0, n, blk):
        parts.append(_bdot(hi[:, s:s + blk], ones) + _bdot(lo[:, s:s + blk], ones))
    ss = parts[0] if len(parts) == 1 else jnp.concatenate(parts, axis=1)
    return ss * (1.0 / head_dim)


def _rope(x, cos, sin):
    n = x.shape[1]
    lane = lax.broadcasted_iota(jnp.int32, (1, 128), 1)
    lower = (lane & 31) < 16
    outs = []
    for s in range(0, n, 128):
        xs = x[:, s:s + 128]
        partner = jnp.where(lower, pltpu.roll(xs, 128 - 16, 1), pltpu.roll(xs, 16, 1))
        outs.append(xs * cos + partner * sin)
    return jnp.concatenate(outs, axis=1)


def _store_cache(o_ref, x, cache):
    if cache == "seq_minor":
        for b in range(o_ref.shape[0]):
            o_ref[b] = x[b * SEQ:(b + 1) * SEQ].T
    else:
        for h in range(o_ref.shape[1]):
            o_ref[:, h, :] = x[:, h * 128:(h + 1) * 128]


def _qk_prep_kernel(*refs, nq, nk, norm, rope, scale, cache):
    it = iter(refs)
    q_ref, k_ref, v_ref = next(it), next(it), next(it)
    if norm:
        qg_ref, kg_ref = next(it), next(it)
    if rope:
        cos_ref, sin_ref = next(it), next(it)
    qo_ref, kvo_ref = next(it), next(it)
    q = q_ref[...]
    k = k_ref[...]
    v = v_ref[...]
    if norm:
        q = (q * lax.rsqrt(_head_mean_sq(q, A_HEAD_DIM) + NORM_EPS)) * qg_ref[...]
        k = (k * lax.rsqrt(_head_mean_sq(k, A_HEAD_DIM) + NORM_EPS)) * kg_ref[...]
    if rope:
        cos, sin = cos_ref[...], sin_ref[...]
        q = _rope(q, cos, sin)
        k = _rope(k, cos, sin)
    qo_ref[...] = (q * scale).astype(BF16)
    kvo_ref[:, 0:nk] = k.astype(BF16)
    kvo_ref[:, nk:] = v.astype(BF16)
    if cache is not None:
        _store_cache(next(it), k, cache)
        _store_cache(next(it), v, cache)


def qk_prep(qkv, *, row0, rows, nq, nk, q_g=None, k_g=None, rope=None, scale, cache=None, tm=512):
    norm = q_g is not None
    r0 = row0 // tm
    qb = nq // nk
    in_specs = [
        pl.BlockSpec((tm, nq), lambda i: (i + r0, 0)),
        pl.BlockSpec((tm, nk), lambda i: (i + r0, qb)),
        pl.BlockSpec((tm, nk), lambda i: (i + r0, qb + 1)),
    ]
    args = [qkv, qkv, qkv]
    if norm:
        in_specs += [pl.BlockSpec((1, nq), lambda i: (0, 0)), pl.BlockSpec((1, nk), lambda i: (0, 0))]
        args += [jnp.tile(q_g, nq // q_g.shape[0]).reshape(1, nq), jnp.tile(k_g, nk // k_g.shape[0]).reshape(1, nk)]
    if rope is not None:
        per = DEC_SEQ // tm
        in_specs += [pl.BlockSpec((tm, 128), lambda i: (i % per, 0))] * 2
        args += list(rope)
    out_specs = [pl.BlockSpec((tm, nq), lambda i: (i, 0)), pl.BlockSpec((tm, 2 * nk), lambda i: (i, 0))]
    out_shape = [jax.ShapeDtypeStruct((rows, nq), BF16), jax.ShapeDtypeStruct((rows, 2 * nk), BF16)]
    if cache == "seq_minor":
        out_specs += [pl.BlockSpec((tm // SEQ, nk, SEQ), lambda i: (i, 0, 0))] * 2
        out_shape += [jax.ShapeDtypeStruct((rows // SEQ, nk, SEQ), F32)] * 2
    elif cache == "head_tile":
        out_specs += [pl.BlockSpec((tm, nk // 128, 128), lambda i: (i, 0, 0))] * 2
        out_shape += [jax.ShapeDtypeStruct((rows, nk // 128, 128), F32)] * 2
    return pl.pallas_call(
        functools.partial(_qk_prep_kernel, nq=nq, nk=nk, norm=norm, rope=rope is not None, scale=scale, cache=cache),
        grid=(rows // tm,),
        in_specs=in_specs,
        out_specs=out_specs,
        out_shape=out_shape,
        compiler_params=_cparams("parallel"),
        name="qk_prep",
    )(*args)


def rope_tables():
    t = jnp.arange(DEC_SEQ)
    row = (t // GRID_W).astype(F32)
    col = (t % GRID_W).astype(F32)
    half = A_HEAD_DIM // 4
    inv_freq = ROPE_THETA ** (-jnp.arange(half, dtype=F32) / half)
    ar = row[:, None] * inv_freq[None, :]
    ac = col[:, None] * inv_freq[None, :]
    cos = jnp.concatenate([jnp.cos(ar), jnp.cos(ar), jnp.cos(ac), jnp.cos(ac)], axis=1)
    sin = jnp.concatenate([-jnp.sin(ar), jnp.sin(ar), -jnp.sin(ac), jnp.sin(ac)], axis=1)
    return jnp.tile(cos, (1, 2)), jnp.tile(sin, (1, 2))


LOG2E = math.log2(math.e)


def _transpose_bf16(x):
    return x.astype(F32).T.astype(BF16)


ATT_TK = 256


def _fill_vt(vt_ref, kv_ref, kw):
    @pl.when(pl.program_id(1) == 0)
    def _():
        vt_ref[...] = _transpose_bf16(kv_ref[:, kw:])


def _attend(n, lk, k_piece, qts, vt_piece):
    out = []
    prev_sts = prev_m = None
    for s in range(n + 1):
        sts, m, l, acc = [], None, None, None
        for r0 in range(0, lk, ATT_TK):
            if s < n:
                st = _bdot(k_piece(s, r0), qts[s])
                m_c = jnp.max(st, axis=0, keepdims=True)
                m = m_c if m is None else jnp.maximum(m, m_c)
                sts.append(st)
            if s > 0:
                p = jnp.exp2(prev_sts[r0 // ATT_TK] - prev_m)
                l_c = jnp.sum(p, axis=0, keepdims=True)
                pv = _bdot(vt_piece(s - 1, r0), p.astype(BF16))
                l = l_c if l is None else l + l_c
                acc = pv if acc is None else acc + pv
        if s > 0:
            out.append((acc, l))
        prev_sts, prev_m = sts, m
    return out


def _gqa_kernel(q_ref, kv_ref, o_ref, vt_ref, *, tq):
    d = A_HEAD_DIM
    kvw = A_KV_HEADS * d
    _fill_vt(vt_ref, kv_ref, kvw)
    qt = _transpose_bf16(q_ref[...])
    qts = [jnp.concatenate([qt[(kh * A_GROUP + g) * d:(kh * A_GROUP + g + 1) * d] for g in range(A_GROUP)], axis=1)
           for kh in range(A_KV_HEADS)]

    k_piece = lambda kh, r0: kv_ref[r0:r0 + ATT_TK, kh * d:(kh + 1) * d]
    vt_piece = lambda kh, r0: vt_ref[kh * d:(kh + 1) * d, r0:r0 + ATT_TK]
    blocks = []
    for acc, l in _attend(A_KV_HEADS, kv_ref.shape[0], k_piece, qts, vt_piece):
        ot = acc / l
        blocks += [ot[:, g * tq:(g + 1) * tq] for g in range(A_GROUP)]
    o_ref[...] = jnp.concatenate(blocks, axis=0).T.astype(o_ref.dtype)


def gqa_attention(q, kv, *, nb, lq, lk, tq):
    nq = lq // tq
    kvw = kv.shape[1] // 2
    return pl.pallas_call(
        functools.partial(_gqa_kernel, tq=tq),
        grid=(nb, nq),
        in_specs=[
            pl.BlockSpec((tq, q.shape[1]), lambda b, i: (b * nq + i, 0)),
            pl.BlockSpec((lk, kv.shape[1]), lambda b, i: (b, 0), pipeline_mode=pl.Buffered(1)),
        ],
        out_specs=pl.BlockSpec((tq, q.shape[1]), lambda b, i: (b * nq + i, 0)),
        out_shape=jax.ShapeDtypeStruct(q.shape, BF16),
        scratch_shapes=[pltpu.VMEM((kvw, lk), BF16)],
        compiler_params=_cparams("parallel", "arbitrary"),
        name="gqa_attention",
    )(q, kv)


def _diff_kernel(q_ref, kv_ref, lam_ref, sg_ref, o_ref, vt_ref, *, lam_init):
    d = B_HEAD_DIM
    kw = B_HEADS * 2 * d
    _fill_vt(vt_ref, kv_ref, kw)
    lf = lam_ref[...]
    lam = (jnp.exp(jnp.sum(lf[0:1] * lf[1:2], axis=-1, keepdims=True))
           - jnp.exp(jnp.sum(lf[2:3] * lf[3:4], axis=-1, keepdims=True)) + lam_init)
    qt = _transpose_bf16(q_ref[...])

    qts = [qt[r * d:(r + 1) * d] for r in range(2 * B_HEADS)]
    k_piece = lambda r, r0: kv_ref[r0:r0 + ATT_TK, r * d:(r + 1) * d]
    vt_piece = lambda r, r0: vt_ref[(r // 2) * 2 * d:(r // 2 + 1) * 2 * d, r0:r0 + ATT_TK]
    res = _attend(2 * B_HEADS, kv_ref.shape[0], k_piece, qts, vt_piece)
    blocks = []
    for h in range(B_HEADS):
        (o1, l1), (o2, l2) = res[2 * h], res[2 * h + 1]
        ot = o1 * (1.0 / l1) - o2 * (lam / l2)
        ms = jnp.mean(ot * ot, axis=0, keepdims=True)
        blocks.append(((ot * lax.rsqrt(ms + B_SUBLN_EPS)) * sg_ref[...]) * (1.0 - lam_init))
    o_ref[...] = jnp.concatenate(blocks, axis=0).T.astype(o_ref.dtype)


def diff_attention(q, kv, lam, subln_g, *, nb, lq, lk, tq, lam_init):
    nq = lq // tq
    kw = kv.shape[1] // 2
    return pl.pallas_call(
        functools.partial(_diff_kernel, lam_init=lam_init),
        grid=(nb, nq),
        in_specs=[
            pl.BlockSpec((tq, q.shape[1]), lambda b, i: (b * nq + i, 0)),
            pl.BlockSpec((lk, kv.shape[1]), lambda b, i: (b, 0), pipeline_mode=pl.Buffered(1)),
            pl.BlockSpec(lam.shape, lambda b, i: (0, 0)),
            pl.BlockSpec((2 * B_HEAD_DIM, 1), lambda b, i: (0, 0)),
        ],
        out_specs=pl.BlockSpec((tq, q.shape[1]), lambda b, i: (b * nq + i, 0)),
        out_shape=jax.ShapeDtypeStruct(q.shape, BF16),
        scratch_shapes=[pltpu.VMEM((kw, lk), BF16)],
        compiler_params=_cparams("parallel", "arbitrary"),
        name="diff_attention",
    )(q, kv, lam, subln_g.reshape(2 * B_HEAD_DIM, 1))


def _decay(x, ld):
    return jnp.exp(-jnp.abs(x * ld))


def _ret_kernel(ld_ref, q_ref, k_ref, v_ref, *rest, nc, has_s0):
    single = nc == 1 and not has_s0
    if single:
        o_ref, st_ref = rest
    elif has_s0:
        s0_ref, o_ref, st_ref, s_ref = rest
    else:
        o_ref, st_ref, s_ref = rest
    c_len = RET_CHUNK
    h = pl.program_id(1)
    s = pl.program_id(2)
    ld_f = ld_ref[0, h]
    ld_b = ld_ref[1, h]
    qb = q_ref[...].astype(BF16)
    k = k_ref[...].astype(F32) * (C_KEY_DIM ** -0.5)
    vb = v_ref[...].astype(BF16)
    idx = lax.broadcasted_iota(jnp.int32, (c_len, 1), 0).astype(F32)
    full = jnp.full((1, 1), float(c_len), F32)

    def intra():
        rel = (lax.broadcasted_iota(jnp.int32, (c_len, c_len), 0)
               - lax.broadcasted_iota(jnp.int32, (c_len, c_len), 1)).astype(F32)
        dmat = (jnp.where(rel >= 0, _decay(jnp.maximum(rel, 0.0), ld_f), 0.0)
                + jnp.where(rel <= 0, _decay(jnp.maximum(-rel, 0.0), ld_b), 0.0))
        a = lax.dot_general(qb, k.astype(BF16), (((1,), (1,)), ((), ())), preferred_element_type=F32)
        return _bdot((a * dmat).astype(BF16), vb)

    def decayed_kv(k_dec):
        kd = (k * k_dec).astype(BF16)
        return lax.dot_general(kd, vb, (((0,), (0,)), ((), ())), preferred_element_type=F32)

    k_dec_f = _decay(c_len - 1.0 - idx, ld_f)
    k_dec_b = _decay(idx, ld_b)
    if single:
        o_ref[...] = intra()
        st_ref[0] = decayed_kv(k_dec_f)
        st_ref[1] = decayed_kv(k_dec_b)
        return

    def init_state(d):
        if has_s0:
            s_ref[...] = s0_ref[d]
        else:
            s_ref[...] = jnp.zeros_like(s_ref)

    @pl.when(s < nc)
    def _forward():
        @pl.when(s == 0)
        def _():
            init_state(0)

        row0 = pl.multiple_of(s * c_len, c_len)
        o_ref[pl.ds(row0, c_len), :] = intra() + _bdot(qb, s_ref[...].astype(BF16)) * _decay(idx + 1.0, ld_f)
        s_ref[...] = s_ref[...] * _decay(full, ld_f) + decayed_kv(k_dec_f)

        @pl.when(s == nc - 1)
        def _():
            st_ref[0] = s_ref[...]

    @pl.when(s >= nc)
    def _backward():
        @pl.when(s == nc)
        def _():
            init_state(1)

        row0 = pl.multiple_of((2 * nc - 1 - s) * c_len, c_len)
        o_ref[pl.ds(row0, c_len), :] += _bdot(qb, s_ref[...].astype(BF16)) * _decay(c_len - idx, ld_b)
        s_ref[...] = s_ref[...] * _decay(full, ld_b) + decayed_kv(k_dec_b)

        @pl.when(s == 2 * nc - 1)
        def _():
            st_ref[1] = s_ref[...]


def retention(proj, log_decay, s0, *, row0, nb, seq):
    c_len = RET_CHUNK
    nc = seq // c_len
    r0 = row0 // c_len
    single = nc == 1 and s0 is None
    steps = 1 if single else 2 * nc
    kblk = (C_HEADS * C_KEY_DIM) // C_KEY_DIM
    vblk = (2 * C_HEADS * C_KEY_DIM) // C_VAL_DIM

    def chunk(b, s):
        return r0 + b * nc + jnp.where(s < nc, s, 2 * nc - 1 - s)

    in_specs = [
        pl.BlockSpec(memory_space=pltpu.SMEM),
        pl.BlockSpec((c_len, C_KEY_DIM), lambda b, h, s: (chunk(b, s), h)),
        pl.BlockSpec((c_len, C_KEY_DIM), lambda b, h, s: (chunk(b, s), kblk + h)),
        pl.BlockSpec((c_len, C_VAL_DIM), lambda b, h, s: (chunk(b, s), vblk + h)),
    ]
    args = [log_decay, proj, proj, proj]
    if s0 is not None:
        in_specs.append(pl.BlockSpec((None, 2, None, C_KEY_DIM, C_VAL_DIM), lambda b, h, s: (b, 0, h, 0, 0)))
        args.append(s0)
    return pl.pallas_call(
        functools.partial(_ret_kernel, nc=nc, has_s0=s0 is not None),
        grid=(nb, C_HEADS, steps),
        in_specs=in_specs,
        out_specs=[
            pl.BlockSpec((seq, C_VAL_DIM), lambda b, h, s: (b, h)),
            pl.BlockSpec((None, 2, None, C_KEY_DIM, C_VAL_DIM), lambda b, h, s: (b, 0, h, 0, 0)),
        ],
        out_shape=[
            jax.ShapeDtypeStruct((nb * seq, C_HEADS * C_VAL_DIM), F32),
            jax.ShapeDtypeStruct((nb, 2, C_HEADS, C_KEY_DIM, C_VAL_DIM), F32),
        ],
        scratch_shapes=[] if single else [pltpu.VMEM((C_KEY_DIM, C_VAL_DIM), F32)],
        compiler_params=_cparams("parallel", "parallel", "arbitrary"),
        name="retention",
    )(*args)


def _ret_out_kernel(oc_ref, ol_ref, g_ref, gn_ref, w_ref, x_ref, gate_ref, o_ref, *, tm):
    o = _pick_group(oc_ref, ol_ref, tm)
    parts = []
    for h in range(C_HEADS):
        oh = o[:, h * C_VAL_DIM:(h + 1) * C_VAL_DIM]
        ms = jnp.mean(oh * oh, axis=-1, keepdims=True)
        parts.append((oh * lax.rsqrt(ms + NORM_EPS)) * gn_ref[:, h * C_VAL_DIM:(h + 1) * C_VAL_DIM])
    a = (_silu(g_ref[...].astype(F32)) * jnp.concatenate(parts, axis=1)).astype(BF16)
    o_ref[...] = x_ref[...] + gate_ref[...] * _bdot(a, w_ref[...])


def retention_out(o_ctx, o_lat, proj, gn_g, w, x, gate, *, tm=512):
    t, d = x.shape
    vd = C_HEADS * C_VAL_DIM
    gblk = proj.shape[1] // vd - 1
    seg = lambda i: (_seg_of_tile(i, tm), 0, 0)
    return pl.pallas_call(
        functools.partial(_ret_out_kernel, tm=tm),
        grid=(t // tm,),
        in_specs=_group_specs(o_ctx, o_lat, tm, vd) + [
            pl.BlockSpec((tm, vd), lambda i: (i, gblk)),
            pl.BlockSpec((1, vd), lambda i: (0, 0)),
            pl.BlockSpec((vd, d), lambda i: (0, 0)),
            pl.BlockSpec((tm, d), lambda i: (i, 0)),
            pl.BlockSpec((None, 1, d), seg),
        ],
        out_specs=pl.BlockSpec((tm, d), lambda i: (i, 0)),
        out_shape=jax.ShapeDtypeStruct((t, d), F32),
        compiler_params=_cparams("parallel"),
        name="retention_out",
    )(o_ctx, o_lat, proj, gn_g.reshape(1, vd), w, x, gate)


def _filter_rows(t, seq, band_ref, w1_ref, b1_ref, w2_ref, b2_ref, w3_ref, fr_ref, delta_ref):
    t_norm = t / max(seq - 1, 1)
    lane = lax.broadcasted_iota(jnp.int32, (1, 128), 1)
    ang = (2.0 * math.pi * t) * band_ref[...] / seq
    feat = jnp.where(lane == 0, t_norm,
                     jnp.where(lane <= D_BANDS, jnp.cos(ang), jnp.where(lane <= 2 * D_BANDS, -jnp.sin(ang), 0.0)))
    a = jnp.sin(fr_ref[0:1] * (_bdot(feat.astype(BF16), w1_ref[...].astype(BF16)) + b1_ref[...]))
    a = jnp.sin(fr_ref[1:2] * (_bdot(a.astype(BF16), w2_ref[...].astype(BF16)) + b2_ref[...]))
    f = _bdot(a.astype(BF16), w3_ref[...].astype(BF16))
    window = jnp.exp(-t_norm * delta_ref[...]) + D_MOD_SHIFT
    return f * jnp.concatenate([window] * 4, axis=1)


def _filter_sum_kernel(band_ref, w1_ref, b1_ref, w2_ref, b2_ref, w3_ref, fr_ref, delta_ref, o_ref, *, seq, tm):
    i = pl.program_id(0)
    t = (i * tm + lax.broadcasted_iota(jnp.int32, (tm, 1), 0)).astype(F32)
    f = _filter_rows(t, seq, band_ref, w1_ref, b1_ref, w2_ref, b2_ref, w3_ref, fr_ref, delta_ref)
    part = jnp.sum(jnp.abs(f), axis=0, keepdims=True)

    @pl.when(i == 0)
    def _():
        o_ref[...] = part

    @pl.when(i > 0)
    def _():
        o_ref[...] += part


def _filter_gen_kernel(band_ref, w1_ref, b1_ref, w2_ref, b2_ref, w3_ref, fr_ref, delta_ref, sum_ref, o_ref, *, seq, tm):
    i = pl.program_id(0)
    r = i * tm + lax.broadcasted_iota(jnp.int32, (tm, 1), 0)
    t = jnp.where(r < seq, r, 2 * seq - r).astype(F32)
    f = _filter_rows(t, seq, band_ref, w1_ref, b1_ref, w2_ref, b2_ref, w3_ref, fr_ref, delta_ref)
    f = f / (sum_ref[...] + 1e-6)
    half = 2 * D_MODEL
    fwd, bwd = f[:, :half], f[:, half:]
    o_ref[...] = jnp.where(r < seq, fwd, jnp.where(r > seq, bwd, 0.0)) + jnp.where(r == 0, bwd, 0.0)


def hyena_circular_kernel(seq, band, w1p, b1, w2, b2, w3, freq, delta, *, tm):
    consts = [band, w1p, b1, w2, b2, w3, freq, delta]
    cspecs = [pl.BlockSpec(a.shape, lambda i: (0, 0)) for a in consts]
    nf = w3.shape[1]
    sums = pl.pallas_call(
        functools.partial(_filter_sum_kernel, seq=seq, tm=tm),
        grid=(seq // tm,),
        in_specs=cspecs,
        out_specs=pl.BlockSpec((1, nf), lambda i: (0, 0)),
        out_shape=jax.ShapeDtypeStruct((1, nf), F32),
        compiler_params=_cparams("arbitrary"),
        name="hyena_filter_sum",
    )(*consts)
    return pl.pallas_call(
        functools.partial(_filter_gen_kernel, seq=seq, tm=tm),
        grid=(2 * seq // tm,),
        in_specs=cspecs + [pl.BlockSpec((1, nf), lambda i: (0, 0))],
        out_specs=pl.BlockSpec((tm, nf // 2), lambda i: (i, 0)),
        out_shape=jax.ShapeDtypeStruct((2 * seq, nf // 2), F32),
        compiler_params=_cparams("parallel"),
        name="hyena_filter_gen",
    )(*consts, sums)


def _dft_cs(rows, cols, n):
    m = np.outer(np.arange(rows), np.arange(cols)) % n
    ang = 2.0 * np.pi * m / n
    return np.cos(ang), np.sin(ang)


def _stack_fwd(c, s):
    return np.block([[c, s], [-s, c]])


def _stack_inv(c, s):
    return np.block([[c, -s], [s, c]])


def _dft_mats():
    as_bf16 = lambda a: jnp.asarray(a, F32).astype(BF16)
    c1, s1 = _dft_cs(LAT_N1, LAT_N1, LAT_N1)
    c2, s2 = _dft_cs(LAT_N2, LAT_N2, LAT_N2)
    cc, sc = _dft_cs(CTX_N, CTX_N, CTX_N)
    h1, hc = LAT_N1 // 2, CTX_N // 2
    return dict(
        lat_g1=as_bf16(_kron_rows(_stack_fwd(c1[:, :h1], s1[:, :h1]))),
        lat_g1r=as_bf16(_kron_rows(np.concatenate([c1, -s1], axis=0))),
        lat_g2=as_bf16(_stack_fwd(c2, s2)),
        lat_g2i=as_bf16(_stack_inv(c2, s2)),
        lat_g1i=as_bf16(_kron_rows(_stack_inv(c1[:h1], s1[:h1]) / LAT_N)),
        ctx_g=as_bf16(_stack_fwd(cc[:, :hc], sc[:, :hc])),
        ctx_gr=as_bf16(np.concatenate([cc, -sc], axis=0)),
        ctx_gi=as_bf16(_stack_inv(cc[:hc], sc[:hc]) / CTX_N),
    )


def _lmul_kernel(g_ref, x_ref, o_ref):
    o_ref[...] = _bdot(g_ref[...], x_ref[...].astype(BF16))


def left_matmul(g, x, *, row_blk=0, tc):
    m, k = g.shape
    n = x.shape[1]
    return pl.pallas_call(
        _lmul_kernel,
        grid=(n // tc,),
        in_specs=[pl.BlockSpec((m, k), lambda j: (0, 0)), pl.BlockSpec((k, tc), lambda j: (row_blk, j))],
        out_specs=pl.BlockSpec((m, tc), lambda j: (0, j)),
        out_shape=jax.ShapeDtypeStruct((m, n), F32),
        compiler_params=_cparams("parallel"),
        name="left_matmul",
    )(g, x)


SLAB_ROWS = 8


def _kron_rows(g):
    return np.kron(g, np.eye(SLAB_ROWS))


def _slab_dot(g_ref, x_ref):
    k, r, w = x_ref.shape
    return _bdot(g_ref[...], x_ref[...].reshape(k * r, w).astype(BF16))


def _lmul_slab_kernel(g_ref, x_ref, o_ref):
    o_ref[...] = _slab_dot(g_ref, x_ref).reshape(o_ref.shape)


def left_matmul_slabs(g8, x3, *, row_blk=0):
    m, k = g8.shape[0] // SLAB_ROWS, g8.shape[1] // SLAB_ROWS
    _, s, w = x3.shape
    return pl.pallas_call(
        _lmul_slab_kernel,
        grid=(s // SLAB_ROWS,),
        in_specs=[pl.BlockSpec(g8.shape, lambda j: (0, 0)),
                  pl.BlockSpec((k, SLAB_ROWS, w), lambda j: (row_blk, j, 0))],
        out_specs=pl.BlockSpec((m, SLAB_ROWS, w), lambda j: (0, j, 0)),
        out_shape=jax.ShapeDtypeStruct((m, s, w), F32),
        compiler_params=_cparams("parallel"),
        name="left_matmul_slabs",
    )(g8, x3)


def _gated_skip(conv, gate, y, skip_ref):
    return gate * (conv + skip_ref[...] * y)


def _lat_last_kernel(g_ref, b_ref, gate_ref, y_ref, skip_ref, o_ref):
    conv = _slab_dot(g_ref, b_ref).reshape(o_ref.shape)
    o_ref[...] = _gated_skip(conv, gate_ref[...], y_ref[...], skip_ref).astype(o_ref.dtype)


def lat_last_stage(g, b3, gate3, gate_blk, y3, y_blk, skip, *, out_dtype):
    m, k = g.shape[0] // SLAB_ROWS, g.shape[1] // SLAB_ROWS
    _, s, w = b3.shape
    return pl.pallas_call(
        _lat_last_kernel,
        grid=(s // SLAB_ROWS,),
        in_specs=[
            pl.BlockSpec(g.shape, lambda j: (0, 0)),
            pl.BlockSpec((k, SLAB_ROWS, w), lambda j: (0, j, 0)),
            pl.BlockSpec((m, SLAB_ROWS, w), lambda j: (gate_blk, j, 0)),
            pl.BlockSpec((m, SLAB_ROWS, w), lambda j: (y_blk, j, 0)),
            pl.BlockSpec((1, w), lambda j: (0, 0)),
        ],
        out_specs=pl.BlockSpec((m, SLAB_ROWS, w), lambda j: (0, j, 0)),
        out_shape=jax.ShapeDtypeStruct((m, s, w), out_dtype),
        compiler_params=_cparams("parallel"),
        name="hyena_lat_last",
    )(g, b3, gate3, y3, skip)


def _lat_mid_kernel(a_ref, *rest, conv):
    if conv:
        h_ref, g2_ref, g2i_ref, o_ref = rest
    else:
        g2_ref, o_ref = rest
    k1 = pl.program_id(0)
    n2 = lax.broadcasted_iota(jnp.int32, (LAT_N2, 1), 0)
    ang = (k1 * n2).astype(F32) * (2.0 * math.pi / LAT_N)
    c, s = jnp.cos(ang), jnp.sin(ang)
    ar, ai = a_ref[0], a_ref[1]
    t = jnp.concatenate([ar * c + ai * s, ai * c - ar * s], axis=0).astype(BF16)
    x = _bdot(g2_ref[...], t)
    xr, xi = x[:LAT_N2], x[LAT_N2:]
    if not conv:
        o_ref[0] = xr
        o_ref[1] = xi
        return
    hr, hi = h_ref[0], h_ref[1]
    y = jnp.concatenate([xr * hr - xi * hi, xr * hi + xi * hr], axis=0).astype(BF16)
    b = _bdot(g2i_ref[...], y)
    br, bi = b[:LAT_N2], b[LAT_N2:]
    o_ref[0] = br * c - bi * s
    o_ref[1] = bi * c + br * s


def lat_mid_stage(a, mats, h=None, order=0):
    w = a.shape[-1]
    blk = lambda width, col: pl.BlockSpec((2, None, LAT_N2, width), lambda k1, j: (0, k1, 0, col(j)))
    gspec = pl.BlockSpec((2 * LAT_N2, 2 * LAT_N2), lambda k1, j: (0, 0))
    if h is None:
        in_specs = [blk(D_MODEL, lambda j: j), gspec]
        args = [a, mats["lat_g2"]]
    else:
        in_specs = [blk(D_MODEL, lambda j: j), blk(D_MODEL, lambda j: order), gspec, gspec]
        args = [a, h, mats["lat_g2"], mats["lat_g2i"]]
    return pl.pallas_call(
        functools.partial(_lat_mid_kernel, conv=h is not None),
        grid=(LAT_N1, w // D_MODEL),
        in_specs=in_specs,
        out_specs=blk(D_MODEL, lambda j: j),
        out_shape=jax.ShapeDtypeStruct(a.shape, F32),
        compiler_params=_cparams("parallel", "arbitrary"),
        name="hyena_lat_mid",
    )(*args)


def _ctx_conv_kernel(y_ref, gate_ref, h_ref, g_ref, gi_ref, skip_ref, o_ref):
    n = CTX_N
    y = y_ref[...]
    z = _bdot(g_ref[...], y.astype(BF16))
    zr, zi = z[:n], z[n:]
    hr, hi = h_ref[0:n], h_ref[n:]
    w = jnp.concatenate([zr * hr - zi * hi, zr * hi + zi * hr], axis=0).astype(BF16)
    conv = _bdot(gi_ref[...], w)
    o_ref[...] = _gated_skip(conv, gate_ref[...], y, skip_ref).astype(o_ref.dtype)


def ctx_conv(y, y_plane, gate, gate_plane, h, order, mats, skip, *, out_dtype):
    rows = 2 * SEQ
    return pl.pallas_call(
        _ctx_conv_kernel,
        grid=(BATCH // 2,),
        in_specs=[
            pl.BlockSpec((None, rows, D_MODEL), lambda p: (y_plane, p, 0)),
            pl.BlockSpec((None, rows, D_MODEL), lambda p: (gate_plane, p, 0)),
            pl.BlockSpec((2 * CTX_N, D_MODEL), lambda p: (0, order)),
            pl.BlockSpec((2 * CTX_N, rows), lambda p: (0, 0)),
            pl.BlockSpec((rows, 2 * CTX_N), lambda p: (0, 0)),
            pl.BlockSpec((1, D_MODEL), lambda p: (0, 0)),
        ],
        out_specs=pl.BlockSpec((rows, D_MODEL), lambda p: (p, 0)),
        out_shape=jax.ShapeDtypeStruct((T_CTX, D_MODEL), out_dtype),
        compiler_params=_cparams("parallel"),
        name="hyena_ctx_conv",
    )(y, gate, h, mats["ctx_g"], mats["ctx_gi"], skip)


def hyena_core(z3, f_w1, f_b1, f_w2, f_b2, f_w3, f_freq, f_skip):
    mats = _dft_mats()
    bands = jnp.linspace(1e-4, D_BANDS - 1, D_BANDS, dtype=F32)
    band = jnp.zeros((1, 128), F32).at[0, 1:1 + D_BANDS].set(bands).at[0, 1 + D_BANDS:1 + 2 * D_BANDS].set(bands)
    w1p = jnp.zeros((128, D_FILTER_HIDDEN), F32).at[:D_EMB].set(f_w1)
    max_decay = math.log(D_DECAY_TARGET) / D_FAST_DECAY_PCT
    min_decay = math.log(D_DECAY_TARGET) / D_SLOW_DECAY_PCT
    delta = jnp.abs(jnp.linspace(min_decay, max_decay, D_MODEL, dtype=F32)).reshape(1, D_MODEL)
    fargs = (band, w1p, f_b1.reshape(1, -1), f_w2, f_b2.reshape(1, -1), f_w3, f_freq, delta)

    kc_ctx = hyena_circular_kernel(SEQ, *fargs, tm=SEQ)
    h_ctx = left_matmul(mats["ctx_gr"], kc_ctx, tc=D_MODEL)
    kc_lat = hyena_circular_kernel(DEC_SEQ, *fargs, tm=512)
    a = left_matmul_slabs(mats["lat_g1r"], kc_lat.reshape(LAT_N1, LAT_N2, 2 * D_MODEL))
    h_lat = lat_mid_stage(a.reshape(2, LAT_N1, LAT_N2, 2 * D_MODEL), mats)

    y1 = ctx_conv(z3, 2, z3, 0, h_ctx, 0, mats, f_skip[0:1], out_dtype=F32)
    y_ctx = ctx_conv(y1[None], 0, z3, 1, h_ctx, 1, mats, f_skip[1:2], out_dtype=BF16)

    slabs = DEC_BATCH * LAT_N1 // 2
    z_slabs = z3.reshape(3 * T_ALL // LAT_N2, LAT_N2, D_MODEL)
    plane_blks = T_ALL // LAT_N2 // slabs
    lat_blk = lambda plane: plane * plane_blks + T_CTX // LAT_N2 // slabs
    y, y_blk = z_slabs, lat_blk(2)
    for n in range(2):
        a = left_matmul_slabs(mats["lat_g1"], y, row_blk=y_blk)
        b = lat_mid_stage(a.reshape(2, LAT_N1, LAT_N2, D_MODEL), mats, h_lat, n)
        y = lat_last_stage(mats["lat_g1i"], b.reshape(2 * LAT_N1, LAT_N2, D_MODEL), z_slabs, lat_blk(n), y, y_blk,
                           f_skip[n:n + 1], out_dtype=F32)
        y_blk = 0
    return y_ctx, y.reshape(T_LAT, D_MODEL)


def kernel(x_prompt, x_sample, cache_attn_k, cache_attn_v, cache_diff_k, cache_diff_v, state_ret, c, c_ctx, w_mod, b_mod, norm1_g, norm2_g, final_g, attn_w_qkv, attn_q_g, attn_k_g, attn_w_o, diff_w_qkv, diff_lambda, diff_subln_g, diff_w_o, ret_w_in, ret_log_decay, ret_gn_g, ret_w_o, hyena_w_in, hyena_sc_w, hyena_sc_b, hyena_f_w1, hyena_f_b1, hyena_f_w2, hyena_f_b2, hyena_f_w3, hyena_f_freq, hyena_f_skip, hyena_w_o, ffn_w_up, ffn_conv_w, ffn_conv_b, ffn_w_down):
    d = D_MODEL
    xs = (x_prompt.reshape(T_CTX, d), x_sample.reshape(T_LAT, d))
    cond =jnp.zeros((SEG_ROWS, d), F32).at[0].set(c_ctx).at[1:N_SEG].set(c)
    mod = modulation(cond, w_mod, b_mod)
    mod = mod.reshape(DEPTH, SEG_ROWS, 6, d).transpose(0, 2, 1, 3).reshape(DEPTH, 6, SEG_ROWS, 1, d)
    rope = rope_tables()
    bf = lambda w: w.astype(BF16)
    w_up, w_down = bf(ffn_w_up), bf(ffn_w_down)
    out = {}

    for l in range(DEPTH):
        m, j = l % 4, l // 4
        sh1, sc1, g1, sh2, sc2, g2 = (mod[l, i] for i in range(6))
        if m == 0:
            nq, nk = A_HEADS * A_HEAD_DIM, A_KV_HEADS * A_HEAD_DIM
            scale = A_HEAD_DIM ** -0.5 * LOG2E
            qkv = norm_proj(*xs, norm1_g[l], sh1, sc1, bf(attn_w_qkv[j]))
            q_c, kv_c, kt_c, vt_c = qk_prep(qkv, row0=0, rows=T_CTX, nq=nq, nk=nk, q_g=attn_q_g[j], k_g=attn_k_g[j],
                                            scale=scale, cache="seq_minor")
            q_l, kv_l = qk_prep(qkv, row0=T_CTX, rows=T_LAT, nq=nq, nk=nk, q_g=attn_q_g[j], k_g=attn_k_g[j],
                                rope=rope, scale=scale)
            as_cache = lambda t: t.reshape(BATCH, A_KV_HEADS, A_HEAD_DIM, SEQ).transpose(0, 3, 1, 2)[:, None]
            out["attn_k"] = as_cache(kt_c)
            out["attn_v"] = as_cache(vt_c)
            cache = jnp.concatenate([cache_attn_k[:, j].reshape(DEC_BATCH, PAST_LEN, nk),
                                     cache_attn_v[:, j].reshape(DEC_BATCH, PAST_LEN, nk)], axis=-1).astype(BF16)
            lk = PAST_LEN + DEC_SEQ
            kv_all = jnp.concatenate([cache, kv_l.reshape(DEC_BATCH, DEC_SEQ, 2 * nk)], axis=1).reshape(DEC_BATCH * lk, 2 * nk)
            o_c = gqa_attention(q_c, kv_c, nb=BATCH, lq=SEQ, lk=SEQ, tq=SEQ)
            o_l = gqa_attention(q_l, kv_all, nb=DEC_BATCH, lq=DEC_SEQ, lk=lk, tq=128)
            x = out_proj_residual(o_c, o_l, bf(attn_w_o[j]), *xs, g1)
        elif m == 1:
            nq = nk = B_HEADS * 2 * B_HEAD_DIM
            scale = B_HEAD_DIM ** -0.5 * LOG2E
            lam_init = 0.8 - 0.6 * math.exp(-0.3 * l)
            qkv = norm_proj(*xs, norm1_g[l], sh1, sc1, bf(diff_w_qkv[j]))
            q_c, kv_c, k3_c, v3_c = qk_prep(qkv, row0=0, rows=T_CTX, nq=nq, nk=nk, scale=scale, cache="head_tile")
            q_l, kv_l = qk_prep(qkv, row0=T_CTX, rows=T_LAT, nq=nq, nk=nk, rope=rope, scale=scale)
            out["diff_k"] = k3_c.reshape(BATCH, 1, SEQ, B_HEADS, 2 * B_HEAD_DIM)
            out["diff_v"] = v3_c.reshape(BATCH, 1, SEQ, B_HEADS, 2 * B_HEAD_DIM)
            cache = jnp.concatenate([cache_diff_k[:, j].reshape(DEC_BATCH, PAST_LEN, nk),
                                     cache_diff_v[:, j].reshape(DEC_BATCH, PAST_LEN, nk)], axis=-1).astype(BF16)
            lk = PAST_LEN + DEC_SEQ
            kv_all = jnp.concatenate([cache, kv_l.reshape(DEC_BATCH, DEC_SEQ, 2 * nk)], axis=1).reshape(DEC_BATCH * lk, 2 * nk)
            dargs = (diff_lambda[j], diff_subln_g[j])
            o_c = diff_attention(q_c, kv_c, *dargs, nb=BATCH, lq=SEQ, lk=SEQ, tq=SEQ, lam_init=lam_init)
            o_l = diff_attention(q_l, kv_all, *dargs, nb=DEC_BATCH, lq=DEC_SEQ, lk=lk, tq=256, lam_init=lam_init)
            x = out_proj_residual(o_c, o_l, bf(diff_w_o[j]), *xs, g1)
        elif m == 2:
            proj = norm_proj(*xs, norm1_g[l], sh1, sc1, bf(ret_w_in[j]), out_dtype=BF16)
            o_c, st = retention(proj, ret_log_decay[j], None, row0=0, nb=BATCH, seq=SEQ)
            o_l, _ = retention(proj, ret_log_decay[j], state_ret[:, j], row0=T_CTX, nb=DEC_BATCH, seq=DEC_SEQ)
            out["ret_s"] = st.reshape(BATCH, 1, 2, C_HEADS, C_KEY_DIM, C_VAL_DIM)
            x = retention_out(o_c, o_l, proj, ret_gn_g[j], bf(ret_w_o[j]), x, g1)
        else:
            z3 = norm_proj_conv(x, norm1_g[l], sh1, sc1, bf(hyena_w_in[j]), hyena_sc_w[j], hyena_sc_b[j])
            y_c, y_l = hyena_core(z3, hyena_f_w1[j], hyena_f_b1[j], hyena_f_w2[j], hyena_f_b2[j], hyena_f_w3[j],
                                  hyena_f_freq[j], hyena_f_skip[j])
            x = out_proj_residual(y_c, y_l, bf(hyena_w_o[j]), *xs, g1)
        x = conv_ffn(x, norm2_g[l], sh2, sc2, g2, w_up, ffn_conv_w[l], ffn_conv_b[l], w_down, l,
                     final_g if l == DEPTH - 1 else None)
        xs = (x, x)

    y_prompt = x[0].reshape(BATCH, SEQ, d)
    y_sample = x[1].reshape(DEC_BATCH, DEC_SEQ, d)
    return (y_prompt, y_sample, out["attn_k"], out["attn_v"], out["diff_k"], out["diff_v"], out["ret_s"])
```

```python
import functools
import math

import jax
import jax.numpy as jnp
import numpy as np
from jax import lax
from jax.experimental import pallas as pl
from jax.experimental.pallas import tpu as pltpu

F32 = jnp.float32
BF16 = jnp.bfloat16

D_MODEL = 1024
BATCH = 32
SEQ = 256
DEPTH = 4
DEC_BATCH = 2
DEC_SEQ = 4096
PAST_LEN = 256
GRID_W = 64
ROPE_THETA = 10000.0
NORM_EPS = 1e-6
A_HEADS = 16
A_KV_HEADS = 4
A_HEAD_DIM = 64
A_GROUP = A_HEADS // A_KV_HEADS
B_HEADS = 8
B_HEAD_DIM = 64
B_SUBLN_EPS = 1e-5
C_HEADS = 4
C_KEY_DIM = 256
C_VAL_DIM = 512
D_BANDS = 16
D_EMB = 1 + 2 * D_BANDS
D_FILTER_HIDDEN = 64
D_FAST_DECAY_PCT = 0.3
D_SLOW_DECAY_PCT = 1.5
D_DECAY_TARGET = 1e-2
D_MOD_SHIFT = 0.05
FFN_DIM = 2816

T_CTX = BATCH * SEQ
T_LAT = DEC_BATCH * DEC_SEQ
T_ALL = T_CTX + T_LAT
N_SEG = 1 + DEC_BATCH
SEG_ROWS = 8

HALO = 16
RET_CHUNK = 256
VMEM_LIMIT = 56 * 1024 * 1024

LAT_N = 2 * DEC_SEQ
LAT_N1 = 64
LAT_N2 = LAT_N // LAT_N1
CTX_N = 2 * SEQ


def _cparams(*sem):
    return pltpu.CompilerParams(dimension_semantics=sem, vmem_limit_bytes=VMEM_LIMIT)


def _seg_of_tile(i, tm):
    start = i * tm
    return jnp.where(start < T_CTX, 0, 1 + (start - T_CTX) // DEC_SEQ)


def _silu(x):
    return x * jax.nn.sigmoid(x)


def _bdot(a, b):
    return jnp.dot(a, b, preferred_element_type=F32)


def _norm_mod(x, g, sh, sc):
    ms = jnp.mean(x * x, axis=-1, keepdims=True)
    y = (x * lax.rsqrt(ms + NORM_EPS)) * g
    return y * (1.0 + sc) + sh


def _mod_kernel(c_ref, w_ref, b_ref, o_ref):
    s = _silu(c_ref[...]).astype(BF16)
    o_ref[...] = _bdot(s, w_ref[...].astype(BF16)) + b_ref[...]


def modulation(cond, w_mod, b_mod):
    tn = 1536
    n = w_mod.shape[-1]
    return pl.pallas_call(
        _mod_kernel,
        grid=(DEPTH, n // tn),
        in_specs=[
            pl.BlockSpec((SEG_ROWS, D_MODEL), lambda l, j: (0, 0)),
            pl.BlockSpec((None, D_MODEL, tn), lambda l, j: (l, 0, j)),
            pl.BlockSpec((None, 1, tn), lambda l, j: (l, 0, j)),
        ],
        out_specs=pl.BlockSpec((None, SEG_ROWS, tn), lambda l, j: (l, 0, j)),
        out_shape=jax.ShapeDtypeStruct((DEPTH, SEG_ROWS, n), F32),
        compiler_params=_cparams("arbitrary", "arbitrary"),
        name="modulation",
    )(cond, w_mod, b_mod.reshape(DEPTH, 1, n))


def _proj_kernel(xc_ref, xl_ref, g_ref, sh_ref, sc_ref, w_ref, o_ref, *, tm, tn):
    x = _pick_group(xc_ref, xl_ref, tm)
    h = _norm_mod(x, g_ref[...], sh_ref[...], sc_ref[...]).astype(BF16)
    for c in range(0, w_ref.shape[1], tn):
        o_ref[:, c:c + tn] = _bdot(h, w_ref[:, c:c + tn]).astype(o_ref.dtype)


def norm_proj(x_ctx, x_lat, g, sh, sc, w, *, tm=512, tn=1536, out_dtype=F32):
    d, n = w.shape
    seg = lambda i: (_seg_of_tile(i, tm), 0, 0)
    return pl.pallas_call(
        functools.partial(_proj_kernel, tm=tm, tn=tn),
        grid=(T_ALL // tm,),
        in_specs=_group_specs(x_ctx, x_lat, tm, d) + [
            pl.BlockSpec((1, d), lambda i: (0, 0)),
            pl.BlockSpec((None, 1, d), seg),
            pl.BlockSpec((None, 1, d), seg),
            pl.BlockSpec((d, n), lambda i: (0, 0), pipeline_mode=pl.Buffered(1)),
        ],
        out_specs=pl.BlockSpec((tm, n), lambda i: (i, 0)),
        out_shape=jax.ShapeDtypeStruct((T_ALL, n), out_dtype),
        compiler_params=_cparams("parallel"),
        name="norm_proj",
    )(x_ctx, x_lat, g.reshape(1, d), sh, sc, w)


def _conv3(u, cw, cb, i, tm):
    rows = u.shape[0]
    up = pltpu.roll(u, 1, 0)[HALO:HALO + tm]
    uc = u[HALO:HALO + tm]
    un = pltpu.roll(u, rows - 1, 0)[HALO:HALO + tm]
    sub = lax.broadcasted_iota(jnp.int32, (8, 1), 0)
    is_ctx = i * tm < T_CTX
    ups, uns = [], []
    for r in range(0, tm, SEQ):
        start = i * tm + r
        first = jnp.logical_or(is_ctx, (start & (DEC_SEQ - 1)) == 0)
        last = jnp.logical_or(is_ctx, ((start + SEQ) & (DEC_SEQ - 1)) == 0)
        ups += [jnp.where(jnp.logical_and(sub == 0, first), 0.0, up[r:r + 8]), up[r + 8:r + SEQ]]
        uns += [un[r:r + SEQ - 8], jnp.where(jnp.logical_and(sub == 7, last), 0.0, un[r + SEQ - 8:r + SEQ])]
    up = jnp.concatenate(ups, axis=0)
    un = jnp.concatenate(uns, axis=0)
    return up * cw[0:1] + uc * cw[1:2] + un * cw[2:3] + cb


def _halo_h(xp_ref, x, xn_ref, g_ref, sh_ref, sc_ref):
    g, sh, sc = g_ref[...], sh_ref[...], sc_ref[...]
    return jnp.concatenate([_norm_mod(xp_ref[...], g, sh, sc).astype(BF16), _norm_mod(x, g, sh, sc).astype(BF16),
                            _norm_mod(xn_ref[...], g, sh, sc).astype(BF16)], axis=0)


def _proj_conv_kernel(xp_ref, x_ref, xn_ref, g_ref, sh_ref, sc_ref, w_ref, cw_ref, cb_ref, o_ref, *, tm, tn):
    i = pl.program_id(0)
    h = _halo_h(xp_ref, x_ref[...], xn_ref, g_ref, sh_ref, sc_ref)
    d = o_ref.shape[-1]
    for c in range(0, w_ref.shape[1], tn):
        u = _conv3(_bdot(h, w_ref[:, c:c + tn]), cw_ref[:, c:c + tn], cb_ref[:, c:c + tn], i, tm)
        o_ref[c // d, :, c % d:c % d + tn] = u


def _ffn_kernel(xp_ref, x_ref, xn_ref, g_ref, sh_ref, sc_ref, gate_ref, wu_ref, cw_ref, cb_ref, wd_ref, *rest, tm, tn,
                final):
    if final:
        fg_ref, oc_ref, ol_ref = rest
    else:
        (o_ref,) = rest
    i = pl.program_id(0)
    x = x_ref[...]
    h = _halo_h(xp_ref, x, xn_ref, g_ref, sh_ref, sc_ref)
    f = wd_ref.shape[0]
    acts = []
    for c in range(0, f, tn):
        a = _conv3(_bdot(h, wu_ref[:, c:c + tn]), cw_ref[:, c:c + tn], cb_ref[:, c:c + tn], i, tm)
        b = _conv3(_bdot(h, wu_ref[:, f + c:f + c + tn]), cw_ref[:, f + c:f + c + tn], cb_ref[:, f + c:f + c + tn], i, tm)
        acts.append((_silu(a) * b).astype(BF16))
    y = x + gate_ref[...] * _bdot(jnp.concatenate(acts, axis=1), wd_ref[...])
    if not final:
        o_ref[...] = y
        return
    ms = jnp.mean(y * y, axis=-1, keepdims=True)
    y = (y * lax.rsqrt(ms + NORM_EPS)) * fg_ref[...]
    is_ctx = i < T_CTX // tm

    @pl.when(is_ctx)
    def _():
        oc_ref[...] = y

    @pl.when(jnp.logical_not(is_ctx))
    def _():
        ol_ref[...] = y


def _halo_specs(t, tm, d):
    per = tm // HALO
    last_blk = t // HALO - 1
    return [
        pl.BlockSpec((HALO, d), lambda i, *_: (jnp.maximum(i * per - 1, 0), 0)),
        pl.BlockSpec((tm, d), lambda i, *_: (i, 0)),
        pl.BlockSpec((HALO, d), lambda i, *_: (jnp.minimum((i + 1) * per, last_blk), 0)),
    ]


def norm_proj_conv(x, g, sh, sc, w, cw, cb, *, tm=512, tn=512):
    t, d = x.shape
    n = w.shape[1]
    seg = lambda i: (_seg_of_tile(i, tm), 0, 0)
    whole = lambda a: pl.BlockSpec(a.shape, lambda i: (0, 0), pipeline_mode=pl.Buffered(1))
    cb = cb.reshape(1, n)
    return pl.pallas_call(
        functools.partial(_proj_conv_kernel, tm=tm, tn=tn),
        grid=(t // tm,),
        in_specs=_halo_specs(t, tm, d) + [
            pl.BlockSpec((1, d), lambda i: (0, 0)),
            pl.BlockSpec((None, 1, d), seg),
            pl.BlockSpec((None, 1, d), seg),
            whole(w), whole(cw), whole(cb),
        ],
        out_specs=pl.BlockSpec((n // d, tm, d), lambda i: (0, i, 0)),
        out_shape=jax.ShapeDtypeStruct((n // d, t, d), F32),
        compiler_params=_cparams("parallel"),
        name="norm_proj_conv",
    )(x, x, x, g.reshape(1, d), sh, sc, w, cw, cb)


def conv_ffn(x, g, sh, sc, gate, w_up, cw, cb, w_down, layer, final_g=None, *, tm=512, tn=256):
    t, d = x.shape
    f = w_down.shape[1]
    seg = lambda i: (_seg_of_tile(i, tm), 0, 0)
    whole = lambda a: pl.BlockSpec(a.shape, lambda i: (0, 0), pipeline_mode=pl.Buffered(1))
    of_layer = lambda a: pl.BlockSpec((None,) + a.shape[1:], lambda i: (layer, 0, 0), pipeline_mode=pl.Buffered(1))
    cb = cb.reshape(1, 2 * f)
    in_specs = _halo_specs(t, tm, d) + [
        pl.BlockSpec((1, d), lambda i: (0, 0)),
        pl.BlockSpec((None, 1, d), seg),
        pl.BlockSpec((None, 1, d), seg),
        pl.BlockSpec((None, 1, d), seg),
        of_layer(w_up), whole(cw), whole(cb), of_layer(w_down),
    ]
    args = [x, x, x, g.reshape(1, d), sh, sc, gate, w_up, cw, cb, w_down]
    out_specs = pl.BlockSpec((tm, d), lambda i: (i, 0))
    out_shape = jax.ShapeDtypeStruct((t, d), F32)
    if final_g is not None:
        nctx = T_CTX // tm
        in_specs.append(pl.BlockSpec((1, d), lambda i: (0, 0)))
        args.append(final_g.reshape(1, d))
        out_specs = [pl.BlockSpec((tm, d), lambda i: (jnp.minimum(i, nctx - 1), 0)),
                     pl.BlockSpec((tm, d), lambda i: (jnp.maximum(i - nctx, 0), 0))]
        out_shape = [jax.ShapeDtypeStruct((T_CTX, d), F32), jax.ShapeDtypeStruct((T_LAT, d), F32)]
    return pl.pallas_call(
        functools.partial(_ffn_kernel, tm=tm, tn=tn, final=final_g is not None),
        grid=(t // tm,),
        in_specs=in_specs,
        out_specs=out_specs,
        out_shape=out_shape,
        compiler_params=_cparams("arbitrary"),
        name="conv_ffn",
    )(*args)


def _group_specs(a_ctx, a_lat, tm, width):
    nctx = T_CTX // tm
    off = nctx if a_lat.shape[0] == T_ALL else 0
    return [
        pl.BlockSpec((tm, width), lambda i, *_: (jnp.minimum(i, nctx - 1), 0)),
        pl.BlockSpec((tm, width), lambda i, *_: (off + jnp.maximum(i - nctx, 0), 0)),
    ]


def _pick_group(ac_ref, al_ref, tm):
    return jnp.where(pl.program_id(0) < T_CTX // tm, ac_ref[...], al_ref[...])


def _out_proj_kernel(ac_ref, al_ref, w_ref, xc_ref, xl_ref, gate_ref, o_ref, *, tm):
    a = _pick_group(ac_ref, al_ref, tm).astype(BF16)
    o_ref[...] = _pick_group(xc_ref, xl_ref, tm) + gate_ref[...] * _bdot(a, w_ref[...])


def out_proj_residual(a_ctx, a_lat, w, x_ctx, x_lat, gate, *, tm=512):
    k, d = w.shape
    seg = lambda i: (_seg_of_tile(i, tm), 0, 0)
    return pl.pallas_call(
        functools.partial(_out_proj_kernel, tm=tm),
        grid=(T_ALL // tm,),
        in_specs=_group_specs(a_ctx, a_lat, tm, k) + [pl.BlockSpec((k, d), lambda i: (0, 0))]
        + _group_specs(x_ctx, x_lat, tm, d) + [pl.BlockSpec((None, 1, d), seg)],
        out_specs=pl.BlockSpec((tm, d), lambda i: (i, 0)),
        out_shape=jax.ShapeDtypeStruct((T_ALL, d), F32),
        compiler_params=_cparams("parallel"),
        name="out_proj_residual",
    )(a_ctx, a_lat, w, x_ctx, x_lat, gate)


def _head_mean_sq(x, head_dim):
    n = x.shape[1]
    x2 = x * x
    hi = x2.astype(BF16)
    lo = (x2 - hi.astype(F32)).astype(BF16)
    blk = 256
    r = lax.broadcasted_iota(jnp.int32, (blk, blk), 0) // head_dim
    c = lax.broadcasted_iota(jnp.int32, (blk, blk), 1) // head_dim
    ones = (r == c).astype(BF16)
    parts = []
    for s in range(0, n, blk):
        parts.append(_bdot(hi[:, s:s + blk], ones) + _bdot(lo[:, s:s + blk], ones))
    ss = parts[0] if len(parts) == 1 else jnp.concatenate(parts, axis=1)
    return ss * (1.0 / head_dim)


def _rope(x, cos, sin):
    n = x.shape[1]
    lane = lax.broadcasted_iota(jnp.int32, (1, 128), 1)
    lower = (lane & 31) < 16
    outs = []
    for s in range(0, n, 128):
        xs = x[:, s:s + 128]
        partner = jnp.where(lower, pltpu.roll(xs, 128 - 16, 1), pltpu.roll(xs, 16, 1))
        outs.append(xs * cos + partner * sin)
    return jnp.concatenate(outs, axis=1)


def _store_cache(o_ref, x, cache):
    if cache == "seq_minor":
        for b in range(o_ref.shape[0]):
            o_ref[b] = x[b * SEQ:(b + 1) * SEQ].T
    else:
        for h in range(o_ref.shape[1]):
            o_ref[:, h, :] = x[:, h * 128:(h + 1) * 128]


def _qk_prep_kernel(*refs, nq, nk, norm, rope, scale, cache):
    it = iter(refs)
    q_ref, k_ref, v_ref = next(it), next(it), next(it)
    if norm:
        qg_ref, kg_ref = next(it), next(it)
    if rope:
        cos_ref, sin_ref = next(it), next(it)
    qo_ref, kvo_ref = next(it), next(it)
    q = q_ref[...]
    k = k_ref[...]
    v = v_ref[...]
    if norm:
        q = (q * lax.rsqrt(_head_mean_sq(q, A_HEAD_DIM) + NORM_EPS)) * qg_ref[...]
        k = (k * lax.rsqrt(_head_mean_sq(k, A_HEAD_DIM) + NORM_EPS)) * kg_ref[...]
    if rope:
        cos, sin = cos_ref[...], sin_ref[...]
        q = _rope(q, cos, sin)
        k = _rope(k, cos, sin)
    qo_ref[...] = (q * scale).astype(BF16)
    kvo_ref[:, 0:nk] = k.astype(BF16)
    kvo_ref[:, nk:] = v.astype(BF16)
    if cache is not None:
        _store_cache(next(it), k, cache)
        _store_cache(next(it), v, cache)


def qk_prep(qkv, *, row0, rows, nq, nk, q_g=None, k_g=None, rope=None, scale, cache=None, tm=512):
    norm = q_g is not None
    r0 = row0 // tm
    qb = nq // nk
    in_specs = [
        pl.BlockSpec((tm, nq), lambda i: (i + r0, 0)),
        pl.BlockSpec((tm, nk), lambda i: (i + r0, qb)),
        pl.BlockSpec((tm, nk), lambda i: (i + r0, qb + 1)),
    ]
    args = [qkv, qkv, qkv]
    if norm:
        in_specs += [pl.BlockSpec((1, nq), lambda i: (0, 0)), pl.BlockSpec((1, nk), lambda i: (0, 0))]
        args += [jnp.tile(q_g, nq // q_g.shape[0]).reshape(1, nq), jnp.tile(k_g, nk // k_g.shape[0]).reshape(1, nk)]
    if rope is not None:
        per = DEC_SEQ // tm
        in_specs += [pl.BlockSpec((tm, 128), lambda i: (i % per, 0))] * 2
        args += list(rope)
    out_specs = [pl.BlockSpec((tm, nq), lambda i: (i, 0)), pl.BlockSpec((tm, 2 * nk), lambda i: (i, 0))]
    out_shape = [jax.ShapeDtypeStruct((rows, nq), BF16), jax.ShapeDtypeStruct((rows, 2 * nk), BF16)]
    if cache == "seq_minor":
        out_specs += [pl.BlockSpec((tm // SEQ, nk, SEQ), lambda i: (i, 0, 0))] * 2
        out_shape += [jax.ShapeDtypeStruct((rows // SEQ, nk, SEQ), F32)] * 2
    elif cache == "head_tile":
        out_specs += [pl.BlockSpec((tm, nk // 128, 128), lambda i: (i, 0, 0))] * 2
        out_shape += [jax.ShapeDtypeStruct((rows, nk // 128, 128), F32)] * 2
    return pl.pallas_call(
        functools.partial(_qk_prep_kernel, nq=nq, nk=nk, norm=norm, rope=rope is not None, scale=scale, cache=cache),
        grid=(rows // tm,),
        in_specs=in_specs,
        out_specs=out_specs,
        out_shape=out_shape,
        compiler_params=_cparams("parallel"),
        name="qk_prep",
    )(*args)


def rope_tables():
    t = jnp.arange(DEC_SEQ)
    row = (t // GRID_W).astype(F32)
    col = (t % GRID_W).astype(F32)
    half = A_HEAD_DIM // 4
    inv_freq = ROPE_THETA ** (-jnp.arange(half, dtype=F32) / half)
    ar = row[:, None] * inv_freq[None, :]
    ac = col[:, None] * inv_freq[None, :]
    cos = jnp.concatenate([jnp.cos(ar), jnp.cos(ar), jnp.cos(ac), jnp.cos(ac)], axis=1)
    sin = jnp.concatenate([-jnp.sin(ar), jnp.sin(ar), -jnp.sin(ac), jnp.sin(ac)], axis=1)
    return jnp.tile(cos, (1, 2)), jnp.tile(sin, (1, 2))


LOG2E = math.log2(math.e)


def _transpose_bf16(x):
    return x.astype(F32).T.astype(BF16)


ATT_TK = 256


def _fill_vt(vt_ref, kv_ref, kw):
    @pl.when(pl.program_id(1) == 0)
    def _():
        vt_ref[...] = _transpose_bf16(kv_ref[:, kw:])


def _attend(n, lk, k_piece, qts, vt_piece):
    out = []
    prev_sts = prev_m = None
    for s in range(n + 1):
        sts, m, l, acc = [], None, None, None
        for r0 in range(0, lk, ATT_TK):
            if s < n:
                st = _bdot(k_piece(s, r0), qts[s])
                m_c = jnp.max(st, axis=0, keepdims=True)
                m = m_c if m is None else jnp.maximum(m, m_c)
                sts.append(st)
            if s > 0:
                p = jnp.exp2(prev_sts[r0 // ATT_TK] - prev_m)
                l_c = jnp.sum(p, axis=0, keepdims=True)
                pv = _bdot(vt_piece(s - 1, r0), p.astype(BF16))
                l = l_c if l is None else l + l_c
                acc = pv if acc is None else acc + pv
        if s > 0:
            out.append((acc, l))
        prev_sts, prev_m = sts, m
    return out


def _gqa_kernel(q_ref, kv_ref, o_ref, vt_ref, *, tq):
    d = A_HEAD_DIM
    kvw = A_KV_HEADS * d
    _fill_vt(vt_ref, kv_ref, kvw)
    qt = _transpose_bf16(q_ref[...])
    qts = [jnp.concatenate([qt[(kh * A_GROUP + g) * d:(kh * A_GROUP + g + 1) * d] for g in range(A_GROUP)], axis=1)
           for kh in range(A_KV_HEADS)]

    k_piece = lambda kh, r0: kv_ref[r0:r0 + ATT_TK, kh * d:(kh + 1) * d]
    vt_piece = lambda kh, r0: vt_ref[kh * d:(kh + 1) * d, r0:r0 + ATT_TK]
    blocks = []
    for acc, l in _attend(A_KV_HEADS, kv_ref.shape[0], k_piece, qts, vt_piece):
        ot = acc / l
        blocks += [ot[:, g * tq:(g + 1) * tq] for g in range(A_GROUP)]
    o_ref[...] = jnp.concatenate(blocks, axis=0).T.astype(o_ref.dtype)


def gqa_attention(q, kv, *, nb, lq, lk, tq):
    nq = lq // tq
    kvw = kv.shape[1] // 2
    return pl.pallas_call(
        functools.partial(_gqa_kernel, tq=tq),
        grid=(nb, nq),
        in_specs=[
            pl.BlockSpec((tq, q.shape[1]), lambda b, i: (b * nq + i, 0)),
            pl.BlockSpec((lk, kv.shape[1]), lambda b, i: (b, 0), pipeline_mode=pl.Buffered(1)),
        ],
        out_specs=pl.BlockSpec((tq, q.shape[1]), lambda b, i: (b * nq + i, 0)),
        out_shape=jax.ShapeDtypeStruct(q.shape, BF16),
        scratch_shapes=[pltpu.VMEM((kvw, lk), BF16)],
        compiler_params=_cparams("parallel", "arbitrary"),
        name="gqa_attention",
    )(q, kv)


def _diff_kernel(q_ref, kv_ref, lam_ref, sg_ref, o_ref, vt_ref, *, lam_init):
    d = B_HEAD_DIM
    kw = B_HEADS * 2 * d
    _fill_vt(vt_ref, kv_ref, kw)
    lf = lam_ref[...]
    lam = (jnp.exp(jnp.sum(lf[0:1] * lf[1:2], axis=-1, keepdims=True))
           - jnp.exp(jnp.sum(lf[2:3] * lf[3:4], axis=-1, keepdims=True)) + lam_init)
    qt = _transpose_bf16(q_ref[...])

    qts = [qt[r * d:(r + 1) * d] for r in range(2 * B_HEADS)]
    k_piece = lambda r, r0: kv_ref[r0:r0 + ATT_TK, r * d:(r + 1) * d]
    vt_piece = lambda r, r0: vt_ref[(r // 2) * 2 * d:(r // 2 + 1) * 2 * d, r0:r0 + ATT_TK]
    res = _attend(2 * B_HEADS, kv_ref.shape[0], k_piece, qts, vt_piece)
    blocks = []
    for h in range(B_HEADS):
        (o1, l1), (o2, l2) = res[2 * h], res[2 * h + 1]
        ot = o1 * (1.0 / l1) - o2 * (lam / l2)
        ms = jnp.mean(ot * ot, axis=0, keepdims=True)
        blocks.append(((ot * lax.rsqrt(ms + B_SUBLN_EPS)) * sg_ref[...]) * (1.0 - lam_init))
    o_ref[...] = jnp.concatenate(blocks, axis=0).T.astype(o_ref.dtype)


def diff_attention(q, kv, lam, subln_g, *, nb, lq, lk, tq, lam_init):
    nq = lq // tq
    kw = kv.shape[1] // 2
    return pl.pallas_call(
        functools.partial(_diff_kernel, lam_init=lam_init),
        grid=(nb, nq),
        in_specs=[
            pl.BlockSpec((tq, q.shape[1]), lambda b, i: (b * nq + i, 0)),
            pl.BlockSpec((lk, kv.shape[1]), lambda b, i: (b, 0), pipeline_mode=pl.Buffered(1)),
            pl.BlockSpec(lam.shape, lambda b, i: (0, 0)),
            pl.BlockSpec((2 * B_HEAD_DIM, 1), lambda b, i: (0, 0)),
        ],
        out_specs=pl.BlockSpec((tq, q.shape[1]), lambda b, i: (b * nq + i, 0)),
        out_shape=jax.ShapeDtypeStruct(q.shape, BF16),
        scratch_shapes=[pltpu.VMEM((kw, lk), BF16)],
        compiler_params=_cparams("parallel", "arbitrary"),
        name="diff_attention",
    )(q, kv, lam, subln_g.reshape(2 * B_HEAD_DIM, 1))


def _decay(x, ld):
    return jnp.exp(-jnp.abs(x * ld))


def _ret_kernel(ld_ref, q_ref, k_ref, v_ref, *rest, nc, has_s0):
    single = nc == 1 and not has_s0
    if single:
        o_ref, st_ref = rest
    elif has_s0:
        s0_ref, o_ref, st_ref, s_ref = rest
    else:
        o_ref, st_ref, s_ref = rest
    c_len = RET_CHUNK
    h = pl.program_id(1)
    s = pl.program_id(2)
    ld_f = ld_ref[0, h]
    ld_b = ld_ref[1, h]
    qb = q_ref[...].astype(BF16)
    k = k_ref[...].astype(F32) * (C_KEY_DIM ** -0.5)
    vb = v_ref[...].astype(BF16)
    idx = lax.broadcasted_iota(jnp.int32, (c_len, 1), 0).astype(F32)
    full = jnp.full((1, 1), float(c_len), F32)

    def intra():
        rel = (lax.broadcasted_iota(jnp.int32, (c_len, c_len), 0)
               - lax.broadcasted_iota(jnp.int32, (c_len, c_len), 1)).astype(F32)
        dmat = (jnp.where(rel >= 0, _decay(jnp.maximum(rel, 0.0), ld_f), 0.0)
                + jnp.where(rel <= 0, _decay(jnp.maximum(-rel, 0.0), ld_b), 0.0))
        a = lax.dot_general(qb, k.astype(BF16), (((1,), (1,)), ((), ())), preferred_element_type=F32)
        return _bdot((a * dmat).astype(BF16), vb)

    def decayed_kv(k_dec):
        kd = (k * k_dec).astype(BF16)
        return lax.dot_general(kd, vb, (((0,), (0,)), ((), ())), preferred_element_type=F32)

    k_dec_f = _decay(c_len - 1.0 - idx, ld_f)
    k_dec_b = _decay(idx, ld_b)
    if single:
        o_ref[...] = intra()
        st_ref[0] = decayed_kv(k_dec_f)
        st_ref[1] = decayed_kv(k_dec_b)
        return

    def init_state(d):
        if has_s0:
            s_ref[...] = s0_ref[d]
        else:
            s_ref[...] = jnp.zeros_like(s_ref)

    @pl.when(s < nc)
    def _forward():
        @pl.when(s == 0)
        def _():
            init_state(0)

        row0 = pl.multiple_of(s * c_len, c_len)
        o_ref[pl.ds(row0, c_len), :] = intra() + _bdot(qb, s_ref[...].astype(BF16)) * _decay(idx + 1.0, ld_f)
        s_ref[...] = s_ref[...] * _decay(full, ld_f) + decayed_kv(k_dec_f)

        @pl.when(s == nc - 1)
        def _():
            st_ref[0] = s_ref[...]

    @pl.when(s >= nc)
    def _backward():
        @pl.when(s == nc)
        def _():
            init_state(1)

        row0 = pl.multiple_of((2 * nc - 1 - s) * c_len, c_len)
        o_ref[pl.ds(row0, c_len), :] += _bdot(qb, s_ref[...].astype(BF16)) * _decay(c_len - idx, ld_b)
        s_ref[...] = s_ref[...] * _decay(full, ld_b) + decayed_kv(k_dec_b)

        @pl.when(s == 2 * nc - 1)
        def _():
            st_ref[1] = s_ref[...]


def retention(proj, log_decay, s0, *, row0, nb, seq):
    c_len = RET_CHUNK
    nc = seq // c_len
    r0 = row0 // c_len
    single = nc == 1 and s0 is None
    steps = 1 if single else 2 * nc
    kblk = (C_HEADS * C_KEY_DIM) // C_KEY_DIM
    vblk = (2 * C_HEADS * C_KEY_DIM) // C_VAL_DIM

    def chunk(b, s):
        return r0 + b * nc + jnp.where(s < nc, s, 2 * nc - 1 - s)

    in_specs = [
        pl.BlockSpec(memory_space=pltpu.SMEM),
        pl.BlockSpec((c_len, C_KEY_DIM), lambda b, h, s: (chunk(b, s), h)),
        pl.BlockSpec((c_len, C_KEY_DIM), lambda b, h, s: (chunk(b, s), kblk + h)),
        pl.BlockSpec((c_len, C_VAL_DIM), lambda b, h, s: (chunk(b, s), vblk + h)),
    ]
    args = [log_decay, proj, proj, proj]
    if s0 is not None:
        in_specs.append(pl.BlockSpec((None, 2, None, C_KEY_DIM, C_VAL_DIM), lambda b, h, s: (b, 0, h, 0, 0)))
        args.append(s0)
    return pl.pallas_call(
        functools.partial(_ret_kernel, nc=nc, has_s0=s0 is not None),
        grid=(nb, C_HEADS, steps),
        in_specs=in_specs,
        out_specs=[
            pl.BlockSpec((seq, C_VAL_DIM), lambda b, h, s: (b, h)),
            pl.BlockSpec((None, 2, None, C_KEY_DIM, C_VAL_DIM), lambda b, h, s: (b, 0, h, 0, 0)),
        ],
        out_shape=[
            jax.ShapeDtypeStruct((nb * seq, C_HEADS * C_VAL_DIM), F32),
            jax.ShapeDtypeStruct((nb, 2, C_HEADS, C_KEY_DIM, C_VAL_DIM), F32),
        ],
        scratch_shapes=[] if single else [pltpu.VMEM((C_KEY_DIM, C_VAL_DIM), F32)],
        compiler_params=_cparams("parallel", "parallel", "arbitrary"),
        name="retention",
    )(*args)


def _ret_out_kernel(oc_ref, ol_ref, g_ref, gn_ref, w_ref, x_ref, gate_ref, o_ref, *, tm):
    o = _pick_group(oc_ref, ol_ref, tm)
    parts = []
    for h in range(C_HEADS):
        oh = o[:, h * C_VAL_DIM:(h + 1) * C_VAL_DIM]
        ms = jnp.mean(oh * oh, axis=-1, keepdims=True)
        parts.append((oh * lax.rsqrt(ms + NORM_EPS)) * gn_ref[:, h * C_VAL_DIM:(h + 1) * C_VAL_DIM])
    a = (_silu(g_ref[...].astype(F32)) * jnp.concatenate(parts, axis=1)).astype(BF16)
    o_ref[...] = x_ref[...] + gate_ref[...] * _bdot(a, w_ref[...])


def retention_out(o_ctx, o_lat, proj, gn_g, w, x, gate, *, tm=512):
    t, d = x.shape
    vd = C_HEADS * C_VAL_DIM
    gblk = proj.shape[1] // vd - 1
    seg = lambda i: (_seg_of_tile(i, tm), 0, 0)
    return pl.pallas_call(
        functools.partial(_ret_out_kernel, tm=tm),
        grid=(t // tm,),
        in_specs=_group_specs(o_ctx, o_lat, tm, vd) + [
            pl.BlockSpec((tm, vd), lambda i: (i, gblk)),
            pl.BlockSpec((1, vd), lambda i: (0, 0)),
            pl.BlockSpec((vd, d), lambda i: (0, 0)),
            pl.BlockSpec((tm, d), lambda i: (i, 0)),
            pl.BlockSpec((None, 1, d), seg),
        ],
        out_specs=pl.BlockSpec((tm, d), lambda i: (i, 0)),
        out_shape=jax.ShapeDtypeStruct((t, d), F32),
        compiler_params=_cparams("parallel"),
        name="retention_out",
    )(o_ctx, o_lat, proj, gn_g.reshape(1, vd), w, x, gate)


def _filter_rows(t, seq, band_ref, w1_ref, b1_ref, w2_ref, b2_ref, w3_ref, fr_ref, delta_ref):
    t_norm = t / max(seq - 1, 1)
    lane = lax.broadcasted_iota(jnp.int32, (1, 128), 1)
    ang = (2.0 * math.pi * t) * band_ref[...] / seq
    feat = jnp.where(lane == 0, t_norm,
                     jnp.where(lane <= D_BANDS, jnp.cos(ang), jnp.where(lane <= 2 * D_BANDS, -jnp.sin(ang), 0.0)))
    a = jnp.sin(fr_ref[0:1] * (_bdot(feat.astype(BF16), w1_ref[...].astype(BF16)) + b1_ref[...]))
    a = jnp.sin(fr_ref[1:2] * (_bdot(a.astype(BF16), w2_ref[...].astype(BF16)) + b2_ref[...]))
    f = _bdot(a.astype(BF16), w3_ref[...].astype(BF16))
    window = jnp.exp(-t_norm * delta_ref[...]) + D_MOD_SHIFT
    return f * jnp.concatenate([window] * 4, axis=1)


def _filter_sum_kernel(band_ref, w1_ref, b1_ref, w2_ref, b2_ref, w3_ref, fr_ref, delta_ref, o_ref, *, seq, tm):
    i = pl.program_id(0)
    t = (i * tm + lax.broadcasted_iota(jnp.int32, (tm, 1), 0)).astype(F32)
    f = _filter_rows(t, seq, band_ref, w1_ref, b1_ref, w2_ref, b2_ref, w3_ref, fr_ref, delta_ref)
    part = jnp.sum(jnp.abs(f), axis=0, keepdims=True)

    @pl.when(i == 0)
    def _():
        o_ref[...] = part

    @pl.when(i > 0)
    def _():
        o_ref[...] += part


def _filter_gen_kernel(band_ref, w1_ref, b1_ref, w2_ref, b2_ref, w3_ref, fr_ref, delta_ref, sum_ref, o_ref, *, seq, tm):
    i = pl.program_id(0)
    r = i * tm + lax.broadcasted_iota(jnp.int32, (tm, 1), 0)
    t = jnp.where(r < seq, r, 2 * seq - r).astype(F32)
    f = _filter_rows(t, seq, band_ref, w1_ref, b1_ref, w2_ref, b2_ref, w3_ref, fr_ref, delta_ref)
    f = f / (sum_ref[...] + 1e-6)
    half = 2 * D_MODEL
    fwd, bwd = f[:, :half], f[:, half:]
    o_ref[...] = jnp.where(r < seq, fwd, jnp.where(r > seq, bwd, 0.0)) + jnp.where(r == 0, bwd, 0.0)


def hyena_circular_kernel(seq, band, w1p, b1, w2, b2, w3, freq, delta, *, tm):
    consts = [band, w1p, b1, w2, b2, w3, freq, delta]
    cspecs = [pl.BlockSpec(a.shape, lambda i: (0, 0)) for a in consts]
    nf = w3.shape[1]
    sums = pl.pallas_call(
        functools.partial(_filter_sum_kernel, seq=seq, tm=tm),
        grid=(seq // tm,),
        in_specs=cspecs,
        out_specs=pl.BlockSpec((1, nf), lambda i: (0, 0)),
        out_shape=jax.ShapeDtypeStruct((1, nf), F32),
        compiler_params=_cparams("arbitrary"),
        name="hyena_filter_sum",
    )(*consts)
    return pl.pallas_call(
        functools.partial(_filter_gen_kernel, seq=seq, tm=tm),
        grid=(2 * seq // tm,),
        in_specs=cspecs + [pl.BlockSpec((1, nf), lambda i: (0, 0))],
        out_specs=pl.BlockSpec((tm, nf // 2), lambda i: (i, 0)),
        out_shape=jax.ShapeDtypeStruct((2 * seq, nf // 2), F32),
        compiler_params=_cparams("parallel"),
        name="hyena_filter_gen",
    )(*consts, sums)


def _dft_cs(rows, cols, n):
    m = np.outer(np.arange(rows), np.arange(cols)) % n
    ang = 2.0 * np.pi * m / n
    return np.cos(ang), np.sin(ang)


def _stack_fwd(c, s):
    return np.block([[c, s], [-s, c]])


def _stack_inv(c, s):
    return np.block([[c, -s], [s, c]])


def _dft_mats():
    as_bf16 = lambda a: jnp.asarray(a, F32).astype(BF16)
    c1, s1 = _dft_cs(LAT_N1, LAT_N1, LAT_N1)
    c2, s2 = _dft_cs(LAT_N2, LAT_N2, LAT_N2)
    cc, sc = _dft_cs(CTX_N, CTX_N, CTX_N)
    h1, hc = LAT_N1 // 2, CTX_N // 2
    return dict(
        lat_g1=as_bf16(_kron_rows(_stack_fwd(c1[:, :h1], s1[:, :h1]))),
        lat_g1r=as_bf16(_kron_rows(np.concatenate([c1, -s1], axis=0))),
        lat_g2=as_bf16(_stack_fwd(c2, s2)),
        lat_g2i=as_bf16(_stack_inv(c2, s2)),
        lat_g1i=as_bf16(_kron_rows(_stack_inv(c1[:h1], s1[:h1]) / LAT_N)),
        ctx_g=as_bf16(_stack_fwd(cc[:, :hc], sc[:, :hc])),
        ctx_gr=as_bf16(np.concatenate([cc, -sc], axis=0)),
        ctx_gi=as_bf16(_stack_inv(cc[:hc], sc[:hc]) / CTX_N),
    )


def _lmul_kernel(g_ref, x_ref, o_ref):
    o_ref[...] = _bdot(g_ref[...], x_ref[...].astype(BF16))


def left_matmul(g, x, *, row_blk=0, tc):
    m, k = g.shape
    n = x.shape[1]
    return pl.pallas_call(
        _lmul_kernel,
        grid=(n // tc,),
        in_specs=[pl.BlockSpec((m, k), lambda j: (0, 0)), pl.BlockSpec((k, tc), lambda j: (row_blk, j))],
        out_specs=pl.BlockSpec((m, tc), lambda j: (0, j)),
        out_shape=jax.ShapeDtypeStruct((m, n), F32),
        compiler_params=_cparams("parallel"),
        name="left_matmul",
    )(g, x)


SLAB_ROWS = 8


def _kron_rows(g):
    return np.kron(g, np.eye(SLAB_ROWS))


def _slab_dot(g_ref, x_ref):
    k, r, w = x_ref.shape
    return _bdot(g_ref[...], x_ref[...].reshape(k * r, w).astype(BF16))


def _lmul_slab_kernel(g_ref, x_ref, o_ref):
    o_ref[...] = _slab_dot(g_ref, x_ref).reshape(o_ref.shape)


def left_matmul_slabs(g8, x3, *, row_blk=0):
    m, k = g8.shape[0] // SLAB_ROWS, g8.shape[1] // SLAB_ROWS
    _, s, w = x3.shape
    return pl.pallas_call(
        _lmul_slab_kernel,
        grid=(s // SLAB_ROWS,),
        in_specs=[pl.BlockSpec(g8.shape, lambda j: (0, 0)),
                  pl.BlockSpec((k, SLAB_ROWS, w), lambda j: (row_blk, j, 0))],
        out_specs=pl.BlockSpec((m, SLAB_ROWS, w), lambda j: (0, j, 0)),
        out_shape=jax.ShapeDtypeStruct((m, s, w), F32),
        compiler_params=_cparams("parallel"),
        name="left_matmul_slabs",
    )(g8, x3)


def _gated_skip(conv, gate, y, skip_ref):
    return gate * (conv + skip_ref[...] * y)


def _lat_last_kernel(g_ref, b_ref, gate_ref, y_ref, skip_ref, o_ref):
    conv = _slab_dot(g_ref, b_ref).reshape(o_ref.shape)
    o_ref[...] = _gated_skip(conv, gate_ref[...], y_ref[...], skip_ref).astype(o_ref.dtype)


def lat_last_stage(g, b3, gate3, gate_blk, y3, y_blk, skip, *, out_dtype):
    m, k = g.shape[0] // SLAB_ROWS, g.shape[1] // SLAB_ROWS
    _, s, w = b3.shape
    return pl.pallas_call(
        _lat_last_kernel,
        grid=(s // SLAB_ROWS,),
        in_specs=[
            pl.BlockSpec(g.shape, lambda j: (0, 0)),
            pl.BlockSpec((k, SLAB_ROWS, w), lambda j: (0, j, 0)),
            pl.BlockSpec((m, SLAB_ROWS, w), lambda j: (gate_blk, j, 0)),
            pl.BlockSpec((m, SLAB_ROWS, w), lambda j: (y_blk, j, 0)),
            pl.BlockSpec((1, w), lambda j: (0, 0)),
        ],
        out_specs=pl.BlockSpec((m, SLAB_ROWS, w), lambda j: (0, j, 0)),
        out_shape=jax.ShapeDtypeStruct((m, s, w), out_dtype),
        compiler_params=_cparams("parallel"),
        name="hyena_lat_last",
    )(g, b3, gate3, y3, skip)


def _lat_mid_kernel(a_ref, *rest, conv):
    if conv:
        h_ref, g2_ref, g2i_ref, o_ref = rest
    else:
        g2_ref, o_ref = rest
    k1 = pl.program_id(0)
    n2 = lax.broadcasted_iota(jnp.int32, (LAT_N2, 1), 0)
    ang = (k1 * n2).astype(F32) * (2.0 * math.pi / LAT_N)
    c, s = jnp.cos(ang), jnp.sin(ang)
    ar, ai = a_ref[0], a_ref[1]
    t = jnp.concatenate([ar * c + ai * s, ai * c - ar * s], axis=0).astype(BF16)
    x = _bdot(g2_ref[...], t)
    xr, xi = x[:LAT_N2], x[LAT_N2:]
    if not conv:
        o_ref[0] = xr
        o_ref[1] = xi
        return
    hr, hi = h_ref[0], h_ref[1]
    y = jnp.concatenate([xr * hr - xi * hi, xr * hi + xi * hr], axis=0).astype(BF16)
    b = _bdot(g2i_ref[...], y)
    br, bi = b[:LAT_N2], b[LAT_N2:]
    o_ref[0] = br * c - bi * s
    o_ref[1] = bi * c + br * s


def lat_mid_stage(a, mats, h=None, order=0):
    w = a.shape[-1]
    blk = lambda width, col: pl.BlockSpec((2, None, LAT_N2, width), lambda k1, j: (0, k1, 0, col(j)))
    gspec = pl.BlockSpec((2 * LAT_N2, 2 * LAT_N2), lambda k1, j: (0, 0))
    if h is None:
        in_specs = [blk(D_MODEL, lambda j: j), gspec]
        args = [a, mats["lat_g2"]]
    else:
        in_specs = [blk(D_MODEL, lambda j: j), blk(D_MODEL, lambda j: order), gspec, gspec]
        args = [a, h, mats["lat_g2"], mats["lat_g2i"]]
    return pl.pallas_call(
        functools.partial(_lat_mid_kernel, conv=h is not None),
        grid=(LAT_N1, w // D_MODEL),
        in_specs=in_specs,
        out_specs=blk(D_MODEL, lambda j: j),
        out_shape=jax.ShapeDtypeStruct(a.shape, F32),
        compiler_params=_cparams("parallel", "arbitrary"),
        name="hyena_lat_mid",
    )(*args)


def _ctx_conv_kernel(y_ref, gate_ref, h_ref, g_ref, gi_ref, skip_ref, o_ref):
    n = CTX_N
    y = y_ref[...]
    z = _bdot(g_ref[...], y.astype(BF16))
    zr, zi = z[:n], z[n:]
    hr, hi = h_ref[0:n], h_ref[n:]
    w = jnp.concatenate([zr * hr - zi * hi, zr * hi + zi * hr], axis=0).astype(BF16)
    conv = _bdot(gi_ref[...], w)
    o_ref[...] = _gated_skip(conv, gate_ref[...], y, skip_ref).astype(o_ref.dtype)


def ctx_conv(y, y_plane, gate, gate_plane, h, order, mats, skip, *, out_dtype):
    rows = 2 * SEQ
    return pl.pallas_call(
        _ctx_conv_kernel,
        grid=(BATCH // 2,),
        in_specs=[
            pl.BlockSpec((None, rows, D_MODEL), lambda p: (y_plane, p, 0)),
            pl.BlockSpec((None, rows, D_MODEL), lambda p: (gate_plane, p, 0)),
            pl.BlockSpec((2 * CTX_N, D_MODEL), lambda p: (0, order)),
            pl.BlockSpec((2 * CTX_N, rows), lambda p: (0, 0)),
            pl.BlockSpec((rows, 2 * CTX_N), lambda p: (0, 0)),
            pl.BlockSpec((1, D_MODEL), lambda p: (0, 0)),
        ],
        out_specs=pl.BlockSpec((rows, D_MODEL), lambda p: (p, 0)),
        out_shape=jax.ShapeDtypeStruct((T_CTX, D_MODEL), out_dtype),
        compiler_params=_cparams("parallel"),
        name="hyena_ctx_conv",
    )(y, gate, h, mats["ctx_g"], mats["ctx_gi"], skip)


def hyena_core(z3, f_w1, f_b1, f_w2, f_b2, f_w3, f_freq, f_skip):
    mats = _dft_mats()
    bands = jnp.linspace(1e-4, D_BANDS - 1, D_BANDS, dtype=F32)
    band = jnp.zeros((1, 128), F32).at[0, 1:1 + D_BANDS].set(bands).at[0, 1 + D_BANDS:1 + 2 * D_BANDS].set(bands)
    w1p = jnp.zeros((128, D_FILTER_HIDDEN), F32).at[:D_EMB].set(f_w1)
    max_decay = math.log(D_DECAY_TARGET) / D_FAST_DECAY_PCT
    min_decay = math.log(D_DECAY_TARGET) / D_SLOW_DECAY_PCT
    delta = jnp.abs(jnp.linspace(min_decay, max_decay, D_MODEL, dtype=F32)).reshape(1, D_MODEL)
    fargs = (band, w1p, f_b1.reshape(1, -1), f_w2, f_b2.reshape(1, -1), f_w3, f_freq, delta)

    kc_ctx = hyena_circular_kernel(SEQ, *fargs, tm=SEQ)
    h_ctx = left_matmul(mats["ctx_gr"], kc_ctx, tc=D_MODEL)
    kc_lat = hyena_circular_kernel(DEC_SEQ, *fargs, tm=512)
    a = left_matmul_slabs(mats["lat_g1r"], kc_lat.reshape(LAT_N1, LAT_N2, 2 * D_MODEL))
    h_lat = lat_mid_stage(a.reshape(2, LAT_N1, LAT_N2, 2 * D_MODEL), mats)

    y1 = ctx_conv(z3, 2, z3, 0, h_ctx, 0, mats, f_skip[0:1], out_dtype=F32)
    y_ctx = ctx_conv(y1[None], 0, z3, 1, h_ctx, 1, mats, f_skip[1:2], out_dtype=BF16)

    slabs = DEC_BATCH * LAT_N1 // 2
    z_slabs = z3.reshape(3 * T_ALL // LAT_N2, LAT_N2, D_MODEL)
    plane_blks = T_ALL // LAT_N2 // slabs
    lat_blk = lambda plane: plane * plane_blks + T_CTX // LAT_N2 // slabs
    y, y_blk = z_slabs, lat_blk(2)
    for n in range(2):
        a = left_matmul_slabs(mats["lat_g1"], y, row_blk=y_blk)
        b = lat_mid_stage(a.reshape(2, LAT_N1, LAT_N2, D_MODEL), mats, h_lat, n)
        y = lat_last_stage(mats["lat_g1i"], b.reshape(2 * LAT_N1, LAT_N2, D_MODEL), z_slabs, lat_blk(n), y, y_blk,
                           f_skip[n:n + 1], out_dtype=F32)
        y_blk = 0
    return y_ctx, y.reshape(T_LAT, D_MODEL)


def kernel(x_prompt, x_sample, cache_attn_k, cache_attn_v, cache_diff_k, cache_diff_v, state_ret, c, c_ctx, w_mod, b_mod, norm1_g, norm2_g, final_g, attn_w_qkv, attn_q_g, attn_k_g, attn_w_o, diff_w_qkv, diff_lambda, diff_subln_g, diff_w_o, ret_w_in, ret_log_decay, ret_gn_g, ret_w_o, hyena_w_in, hyena_sc_w, hyena_sc_b, hyena_f_w1, hyena_f_b1, hyena_f_w2, hyena_f_b2, hyena_f_w3, hyena_f_freq, hyena_f_skip, hyena_w_o, ffn_w_up, ffn_conv_w, ffn_conv_b, ffn_w_down):
    d = D_MODEL
    xs = (x_prompt.reshape(T_CTX, d), x_sample.reshape(T_LAT, d))
    cond =jnp.zeros((SEG_ROWS, d), F32).at[0].set(c_ctx).at[1:N_SEG].set(c)
    mod = modulation(cond, w_mod, b_mod)
    mod = mod.reshape(DEPTH, SEG_ROWS, 6, d).transpose(0, 2, 1, 3).reshape(DEPTH, 6, SEG_ROWS, 1, d)
    rope = rope_tables()
    bf = lambda w: w.astype(BF16)
    w_up, w_down = bf(ffn_w_up), bf(ffn_w_down)
    out = {}

    for l in range(DEPTH):
        m, j = l % 4, l // 4
        sh1, sc1, g1, sh2, sc2, g2 = (mod[l, i] for i in range(6))
        if m == 0:
            nq, nk = A_HEADS * A_HEAD_DIM, A_KV_HEADS * A_HEAD_DIM
            scale = A_HEAD_DIM ** -0.5 * LOG2E
            qkv = norm_proj(*xs, norm1_g[l], sh1, sc1, bf(attn_w_qkv[j]))
            q_c, kv_c, kt_c, vt_c = qk_prep(qkv, row0=0, rows=T_CTX, nq=nq, nk=nk, q_g=attn_q_g[j], k_g=attn_k_g[j],
                                            scale=scale, cache="seq_minor")
            q_l, kv_l = qk_prep(qkv, row0=T_CTX, rows=T_LAT, nq=nq, nk=nk, q_g=attn_q_g[j], k_g=attn_k_g[j],
                                rope=rope, scale=scale)
            as_cache = lambda t: t.reshape(BATCH, A_KV_HEADS, A_HEAD_DIM, SEQ).transpose(0, 3, 1, 2)[:, None]
            out["attn_k"] = as_cache(kt_c)
            out["attn_v"] = as_cache(vt_c)
            cache = jnp.concatenate([cache_attn_k[:, j].reshape(DEC_BATCH, PAST_LEN, nk),
                                     cache_attn_v[:, j].reshape(DEC_BATCH, PAST_LEN, nk)], axis=-1).astype(BF16)
            lk = PAST_LEN + DEC_SEQ
            kv_all = jnp.concatenate([cache, kv_l.reshape(DEC_BATCH, DEC_SEQ, 2 * nk)], axis=1).reshape(DEC_BATCH * lk, 2 * nk)
            o_c = gqa_attention(q_c, kv_c, nb=BATCH, lq=SEQ, lk=SEQ, tq=SEQ)
            o_l = gqa_attention(q_l, kv_all, nb=DEC_BATCH, lq=DEC_SEQ, lk=lk, tq=128)
            x = out_proj_residual(o_c, o_l, bf(attn_w_o[j]), *xs, g1)
        elif m == 1:
            nq = nk = B_HEADS * 2 * B_HEAD_DIM
            scale = B_HEAD_DIM ** -0.5 * LOG2E
            lam_init = 0.8 - 0.6 * math.exp(-0.3 * l)
            qkv = norm_proj(*xs, norm1_g[l], sh1, sc1, bf(diff_w_qkv[j]))
            q_c, kv_c, k3_c, v3_c = qk_prep(qkv, row0=0, rows=T_CTX, nq=nq, nk=nk, scale=scale, cache="head_tile")
            q_l, kv_l = qk_prep(qkv, row0=T_CTX, rows=T_LAT, nq=nq, nk=nk, rope=rope, scale=scale)
            out["diff_k"] = k3_c.reshape(BATCH, 1, SEQ, B_HEADS, 2 * B_HEAD_DIM)
            out["diff_v"] = v3_c.reshape(BATCH, 1, SEQ, B_HEADS, 2 * B_HEAD_DIM)
            cache = jnp.concatenate([cache_diff_k[:, j].reshape(DEC_BATCH, PAST_LEN, nk),
                                     cache_diff_v[:, j].reshape(DEC_BATCH, PAST_LEN, nk)], axis=-1).astype(BF16)
            lk = PAST_LEN + DEC_SEQ
            kv_all = jnp.concatenate([cache, kv_l.reshape(DEC_BATCH, DEC_SEQ, 2 * nk)], axis=1).reshape(DEC_BATCH * lk, 2 * nk)
            dargs = (diff_lambda[j], diff_subln_g[j])
            o_c = diff_attention(q_c, kv_c, *dargs, nb=BATCH, lq=SEQ, lk=SEQ, tq=SEQ, lam_init=lam_init)
            o_l = diff_attention(q_l, kv_all, *dargs, nb=DEC_BATCH, lq=DEC_SEQ, lk=lk, tq=256, lam_init=lam_init)
            x = out_proj_residual(o_c, o_l, bf(diff_w_o[j]), *xs, g1)
        elif m == 2:
            proj = norm_proj(*xs, norm1_g[l], sh1, sc1, bf(ret_w_in[j]), out_dtype=BF16)
            o_c, st = retention(proj, ret_log_decay[j], None, row0=0, nb=BATCH, seq=SEQ)
            o_l, _ = retention(proj, ret_log_decay[j], state_ret[:, j], row0=T_CTX, nb=DEC_BATCH, seq=DEC_SEQ)
            out["ret_s"] = st.reshape(BATCH, 1, 2, C_HEADS, C_KEY_DIM, C_VAL_DIM)
            x = retention_out(o_c, o_l, proj, ret_gn_g[j], bf(ret_w_o[j]), x, g1)
        else:
            z3 = norm_proj_conv(x, norm1_g[l], sh1, sc1, bf(hyena_w_in[j]), hyena_sc_w[j], hyena_sc_b[j])
            y_c, y_l = hyena_core(z3, hyena_f_w1[j], hyena_f_b1[j], hyena_f_w2[j], hyena_f_b2[j], hyena_f_w3[j],
                                  hyena_f_freq[j], hyena_f_skip[j])
            x = out_proj_residual(y_c, y_l, bf(hyena_w_o[j]), *xs, g1)
        x = conv_ffn(x, norm2_g[l], sh2, sc2, g2, w_up, ffn_conv_w[l], ffn_conv_b[l], w_down, l,
                     final_g if l == DEPTH - 1 else None)
        xs = (x, x)

    y_prompt = x[0].reshape(BATCH, SEQ, d)
    y_sample = x[1].reshape(DEC_BATCH, DEC_SEQ, d)
    return (y_prompt, y_sample, out["attn_k"], out["attn_v"], out["diff_k"], out["diff_v"], out["ret_s"])
```

```python
import functools
import math

import jax
import jax.numpy as jnp
import numpy as np
from jax import lax
from jax.experimental import pallas as pl
from jax.experimental.pallas import tpu as pltpu

F32 = jnp.float32
BF16 = jnp.bfloat16

D_MODEL = 1024
BATCH = 32
SEQ = 256
DEPTH = 4
DEC_BATCH = 2
DEC_SEQ = 4096
PAST_LEN = 256
GRID_W = 64
ROPE_THETA = 10000.0
NORM_EPS = 1e-6
A_HEADS = 16
A_KV_HEADS = 4
A_HEAD_DIM = 64
A_GROUP = A_HEADS // A_KV_HEADS
B_HEADS = 8
B_HEAD_DIM = 64
B_SUBLN_EPS = 1e-5
C_HEADS = 4
C_KEY_DIM = 256
C_VAL_DIM = 512
D_BANDS = 16
D_EMB = 1 + 2 * D_BANDS
D_FILTER_HIDDEN = 64
D_FAST_DECAY_PCT = 0.3
D_SLOW_DECAY_PCT = 1.5
D_DECAY_TARGET = 1e-2
D_MOD_SHIFT = 0.05
FFN_DIM = 2816

T_CTX = BATCH * SEQ
T_LAT = DEC_BATCH * DEC_SEQ
T_ALL = T_CTX + T_LAT
N_SEG = 1 + DEC_BATCH
SEG_ROWS = 8

HALO = 16
RET_CHUNK = 256
VMEM_LIMIT = 56 * 1024 * 1024

LAT_N = 2 * DEC_SEQ
LAT_N1 = 64
LAT_N2 = LAT_N // LAT_N1
CTX_N = 2 * SEQ


def _cparams(*sem):
    return pltpu.CompilerParams(dimension_semantics=sem, vmem_limit_bytes=VMEM_LIMIT)


def _seg_of_tile(i, tm):
    start = i * tm
    return jnp.where(start < T_CTX, 0, 1 + (start - T_CTX) // DEC_SEQ)


def _silu(x):
    return x * jax.nn.sigmoid(x)


def _bdot(a, b):
    return jnp.dot(a, b, preferred_element_type=F32)


def _norm_mod(x, g, sh, sc):
    ms = jnp.mean(x * x, axis=-1, keepdims=True)
    y = (x * lax.rsqrt(ms + NORM_EPS)) * g
    return y * (1.0 + sc) + sh


def _mod_kernel(c_ref, w_ref, b_ref, o_ref):
    s = _silu(c_ref[...]).astype(BF16)
    o_ref[...] = _bdot(s, w_ref[...].astype(BF16)) + b_ref[...]


def modulation(cond, w_mod, b_mod):
    tn = 1536
    n = w_mod.shape[-1]
    return pl.pallas_call(
        _mod_kernel,
        grid=(DEPTH, n // tn),
        in_specs=[
            pl.BlockSpec((SEG_ROWS, D_MODEL), lambda l, j: (0, 0)),
            pl.BlockSpec((None, D_MODEL, tn), lambda l, j: (l, 0, j)),
            pl.BlockSpec((None, 1, tn), lambda l, j: (l, 0, j)),
        ],
        out_specs=pl.BlockSpec((None, SEG_ROWS, tn), lambda l, j: (l, 0, j)),
        out_shape=jax.ShapeDtypeStruct((DEPTH, SEG_ROWS, n), F32),
        compiler_params=_cparams("arbitrary", "arbitrary"),
        name="modulation",
    )(cond, w_mod, b_mod.reshape(DEPTH, 1, n))


def _proj_kernel(xc_ref, xl_ref, g_ref, sh_ref, sc_ref, w_ref, o_ref, *, tm, tn):
    x = _pick_group(xc_ref, xl_ref, tm)
    h = _norm_mod(x, g_ref[...], sh_ref[...], sc_ref[...]).astype(BF16)
    for c in range(0, w_ref.shape[1], tn):
        o_ref[:, c:c + tn] = _bdot(h, w_ref[:, c:c + tn]).astype(o_ref.dtype)


def norm_proj(x_ctx, x_lat, g, sh, sc, w, *, tm=512, tn=1536, out_dtype=F32):
    d, n = w.shape
    seg = lambda i: (_seg_of_tile(i, tm), 0, 0)
    return pl.pallas_call(
        functools.partial(_proj_kernel, tm=tm, tn=tn),
        grid=(T_ALL // tm,),
        in_specs=_group_specs(x_ctx, x_lat, tm, d) + [
            pl.BlockSpec((1, d), lambda i: (0, 0)),
            pl.BlockSpec((None, 1, d), seg),
            pl.BlockSpec((None, 1, d), seg),
            pl.BlockSpec((d, n), lambda i: (0, 0), pipeline_mode=pl.Buffered(1)),
        ],
        out_specs=pl.BlockSpec((tm, n), lambda i: (i, 0)),
        out_shape=jax.ShapeDtypeStruct((T_ALL, n), out_dtype),
        compiler_params=_cparams("parallel"),
        name="norm_proj",
    )(x_ctx, x_lat, g.reshape(1, d), sh, sc, w)


def _conv3(u, cw, cb, i, tm):
    rows = u.shape[0]
    up = pltpu.roll(u, 1, 0)[HALO:HALO + tm]
    uc = u[HALO:HALO + tm]
    un = pltpu.roll(u, rows - 1, 0)[HALO:HALO + tm]
    sub = lax.broadcasted_iota(jnp.int32, (8, 1), 0)
    is_ctx = i * tm < T_CTX
    ups, uns = [], []
    for r in range(0, tm, SEQ):
        start = i * tm + r
        first = jnp.logical_or(is_ctx, (start & (DEC_SEQ - 1)) == 0)
        last = jnp.logical_or(is_ctx, ((start + SEQ) & (DEC_SEQ - 1)) == 0)
        ups += [jnp.where(jnp.logical_and(sub == 0, first), 0.0, up[r:r + 8]), up[r + 8:r + SEQ]]
        uns += [un[r:r + SEQ - 8], jnp.where(jnp.logical_and(sub == 7, last), 0.0, un[r + SEQ - 8:r + SEQ])]
    up = jnp.concatenate(ups, axis=0)
    un = jnp.concatenate(uns, axis=0)
    return up * cw[0:1] + uc * cw[1:2] + un * cw[2:3] + cb


def _halo_h(xp_ref, x, xn_ref, g_ref, sh_ref, sc_ref):
    g, sh, sc = g_ref[...], sh_ref[...], sc_ref[...]
    return jnp.concatenate([_norm_mod(xp_ref[...], g, sh, sc).astype(BF16), _norm_mod(x, g, sh, sc).astype(BF16),
                            _norm_mod(xn_ref[...], g, sh, sc).astype(BF16)], axis=0)


def _proj_conv_kernel(xp_ref, x_ref, xn_ref, g_ref, sh_ref, sc_ref, w_ref, cw_ref, cb_ref, o_ref, *, tm, tn):
    i = pl.program_id(0)
    h = _halo_h(xp_ref, x_ref[...], xn_ref, g_ref, sh_ref, sc_ref)
    d = o_ref.shape[-1]
    for c in range(0, w_ref.shape[1], tn):
        u = _conv3(_bdot(h, w_ref[:, c:c + tn]), cw_ref[:, c:c + tn], cb_ref[:, c:c + tn], i, tm)
        o_ref[c // d, :, c % d:c % d + tn] = u


def _ffn_kernel(xp_ref, x_ref, xn_ref, g_ref, sh_ref, sc_ref, gate_ref, wu_ref, cw_ref, cb_ref, wd_ref, *rest, tm, tn,
                final):
    if final:
        fg_ref, oc_ref, ol_ref = rest
    else:
        (o_ref,) = rest
    i = pl.program_id(0)
    x = x_ref[...]
    h = _halo_h(xp_ref, x, xn_ref, g_ref, sh_ref, sc_ref)
    f = wd_ref.shape[0]
    acts = []
    for c in range(0, f, tn):
        a = _conv3(_bdot(h, wu_ref[:, c:c + tn]), cw_ref[:, c:c + tn], cb_ref[:, c:c + tn], i, tm)
        b = _conv3(_bdot(h, wu_ref[:, f + c:f + c + tn]), cw_ref[:, f + c:f + c + tn], cb_ref[:, f + c:f + c + tn], i, tm)
        acts.append((_silu(a) * b).astype(BF16))
    y = x + gate_ref[...] * _bdot(jnp.concatenate(acts, axis=1), wd_ref[...])
    if not final:
        o_ref[...] = y
        return
    ms = jnp.mean(y * y, axis=-1, keepdims=True)
    y = (y * lax.rsqrt(ms + NORM_EPS)) * fg_ref[...]
    is_ctx = i < T_CTX // tm

    @pl.when(is_ctx)
    def _():
        oc_ref[...] = y

    @pl.when(jnp.logical_not(is_ctx))
    def _():
        ol_ref[...] = y


def _halo_specs(t, tm, d):
    per = tm // HALO
    last_blk = t // HALO - 1
    return [
        pl.BlockSpec((HALO, d), lambda i, *_: (jnp.maximum(i * per - 1, 0), 0)),
        pl.BlockSpec((tm, d), lambda i, *_: (i, 0)),
        pl.BlockSpec((HALO, d), lambda i, *_: (jnp.minimum((i + 1) * per, last_blk), 0)),
    ]


def norm_proj_conv(x, g, sh, sc, w, cw, cb, *, tm=512, tn=512):
    t, d = x.shape
    n = w.shape[1]
    seg = lambda i: (_seg_of_tile(i, tm), 0, 0)
    whole = lambda a: pl.BlockSpec(a.shape, lambda i: (0, 0), pipeline_mode=pl.Buffered(1))
    cb = cb.reshape(1, n)
    return pl.pallas_call(
        functools.partial(_proj_conv_kernel, tm=tm, tn=tn),
        grid=(t // tm,),
        in_specs=_halo_specs(t, tm, d) + [
            pl.BlockSpec((1, d), lambda i: (0, 0)),
            pl.BlockSpec((None, 1, d), seg),
            pl.BlockSpec((None, 1, d), seg),
            whole(w), whole(cw), whole(cb),
        ],
        out_specs=pl.BlockSpec((n // d, tm, d), lambda i: (0, i, 0)),
        out_shape=jax.ShapeDtypeStruct((n // d, t, d), F32),
        compiler_params=_cparams("parallel"),
        name="norm_proj_conv",
    )(x, x, x, g.reshape(1, d), sh, sc, w, cw, cb)


def conv_ffn(x, g, sh, sc, gate, w_up, cw, cb, w_down, layer, final_g=None, *, tm=512, tn=256):
    t, d = x.shape
    f = w_down.shape[1]
    seg = lambda i: (_seg_of_tile(i, tm), 0, 0)
    whole = lambda a: pl.BlockSpec(a.shape, lambda i: (0, 0), pipeline_mode=pl.Buffered(1))
    of_layer = lambda a: pl.BlockSpec((None,) + a.shape[1:], lambda i: (layer, 0, 0), pipeline_mode=pl.Buffered(1))
    cb = cb.reshape(1, 2 * f)
    in_specs = _halo_specs(t, tm, d) + [
        pl.BlockSpec((1, d), lambda i: (0, 0)),
        pl.BlockSpec((None, 1, d), seg),
        pl.BlockSpec((None, 1, d), seg),
        pl.BlockSpec((None, 1, d), seg),
        of_layer(w_up), whole(cw), whole(cb), of_layer(w_down),
    ]
    args = [x, x, x, g.reshape(1, d), sh, sc, gate, w_up, cw, cb, w_down]
    out_specs = pl.BlockSpec((tm, d), lambda i: (i, 0))
    out_shape = jax.ShapeDtypeStruct((t, d), F32)
    if final_g is not None:
        nctx = T_CTX // tm
        in_specs.append(pl.BlockSpec((1, d), lambda i: (0, 0)))
        args.append(final_g.reshape(1, d))
        out_specs = [pl.BlockSpec((tm, d), lambda i: (jnp.minimum(i, nctx - 1), 0)),
                     pl.BlockSpec((tm, d), lambda i: (jnp.maximum(i - nctx, 0), 0))]
        out_shape = [jax.ShapeDtypeStruct((T_CTX, d), F32), jax.ShapeDtypeStruct((T_LAT, d), F32)]
    return pl.pallas_call(
        functools.partial(_ffn_kernel, tm=tm, tn=tn, final=final_g is not None),
        grid=(t // tm,),
        in_specs=in_specs,
        out_specs=out_specs,
        out_shape=out_shape,
        compiler_params=_cparams("arbitrary"),
        name="conv_ffn",
    )(*args)


def _group_specs(a_ctx, a_lat, tm, width):
    nctx = T_CTX // tm
    off = nctx if a_lat.shape[0] == T_ALL else 0
    return [
        pl.BlockSpec((tm, width), lambda i, *_: (jnp.minimum(i, nctx - 1), 0)),
        pl.BlockSpec((tm, width), lambda i, *_: (off + jnp.maximum(i - nctx, 0), 0)),
    ]


def _pick_group(ac_ref, al_ref, tm):
    return jnp.where(pl.program_id(0) < T_CTX // tm, ac_ref[...], al_ref[...])


def _out_proj_kernel(ac_ref, al_ref, w_ref, xc_ref, xl_ref, gate_ref, o_ref, *, tm):
    a = _pick_group(ac_ref, al_ref, tm).astype(BF16)
    o_ref[...] = _pick_group(xc_ref, xl_ref, tm) + gate_ref[...] * _bdot(a, w_ref[...])


def out_proj_residual(a_ctx, a_lat, w, x_ctx, x_lat, gate, *, tm=512):
    k, d = w.shape
    seg = lambda i: (_seg_of_tile(i, tm), 0, 0)
    return pl.pallas_call(
        functools.partial(_out_proj_kernel, tm=tm),
        grid=(T_ALL // tm,),
        in_specs=_group_specs(a_ctx, a_lat, tm, k) + [pl.BlockSpec((k, d), lambda i: (0, 0))]
        + _group_specs(x_ctx, x_lat, tm, d) + [pl.BlockSpec((None, 1, d), seg)],
        out_specs=pl.BlockSpec((tm, d), lambda i: (i, 0)),
        out_shape=jax.ShapeDtypeStruct((T_ALL, d), F32),
        compiler_params=_cparams("parallel"),
        name="out_proj_residual",
    )(a_ctx, a_lat, w, x_ctx, x_lat, gate)


def _head_mean_sq(x, head_dim):
    n = x.shape[1]
    x2 = x * x
    hi = x2.astype(BF16)
    lo = (x2 - hi.astype(F32)).astype(BF16)
    blk = 256
    r = lax.broadcasted_iota(jnp.int32, (blk, blk), 0) // head_dim
    c = lax.broadcasted_iota(jnp.int32, (blk, blk), 1) // head_dim
    ones = (r == c).astype(BF16)
    parts = []
    for s in range(0, n, blk):
        parts.append(_bdot(hi[:, s:s + blk], ones) + _bdot(lo[:, s:s + blk], ones))
    ss = parts[0] if len(parts) == 1 else jnp.concatenate(parts, axis=1)
    return ss * (1.0 / head_dim)


def _rope(x, cos, sin):
    n = x.shape[1]
    lane = lax.broadcasted_iota(jnp.int32, (1, 128), 1)
    lower = (lane & 31) < 16
    outs = []
    for s in range(0, n, 128):
        xs = x[:, s:s + 128]
        partner = jnp.where(lower, pltpu.roll(xs, 128 - 16, 1), pltpu.roll(xs, 16, 1))
        outs.append(xs * cos + partner * sin)
    return jnp.concatenate(outs, axis=1)


def _store_cache(o_ref, x, cache):
    if cache == "seq_minor":
        for b in range(o_ref.shape[0]):
            o_ref[b] = x[b * SEQ:(b + 1) * SEQ].T
    else:
        for h in range(o_ref.shape[1]):
            o_ref[:, h, :] = x[:, h * 128:(h + 1) * 128]


def _qk_prep_kernel(*refs, nq, nk, norm, rope, scale, cache):
    it = iter(refs)
    q_ref, k_ref, v_ref = next(it), next(it), next(it)
    if norm:
        qg_ref, kg_ref = next(it), next(it)
    if rope:
        cos_ref, sin_ref = next(it), next(it)
    qo_ref, kvo_ref = next(it), next(it)
    q = q_ref[...]
    k = k_ref[...]
    v = v_ref[...]
    if norm:
        q = (q * lax.rsqrt(_head_mean_sq(q, A_HEAD_DIM) + NORM_EPS)) * qg_ref[...]
        k = (k * lax.rsqrt(_head_mean_sq(k, A_HEAD_DIM) + NORM_EPS)) * kg_ref[...]
    if rope:
        cos, sin = cos_ref[...], sin_ref[...]
        q = _rope(q, cos, sin)
        k = _rope(k, cos, sin)
    qo_ref[...] = (q * scale).astype(BF16)
    kvo_ref[:, 0:nk] = k.astype(BF16)
    kvo_ref[:, nk:] = v.astype(BF16)
    if cache is not None:
        _store_cache(next(it), k, cache)
        _store_cache(next(it), v, cache)


def qk_prep(qkv, *, row0, rows, nq, nk, q_g=None, k_g=None, rope=None, scale, cache=None, tm=512):
    norm = q_g is not None
    r0 = row0 // tm
    qb = nq // nk
    in_specs = [
        pl.BlockSpec((tm, nq), lambda i: (i + r0, 0)),
        pl.BlockSpec((tm, nk), lambda i: (i + r0, qb)),
        pl.BlockSpec((tm, nk), lambda i: (i + r0, qb + 1)),
    ]
    args = [qkv, qkv, qkv]
    if norm:
        in_specs += [pl.BlockSpec((1, nq), lambda i: (0, 0)), pl.BlockSpec((1, nk), lambda i: (0, 0))]
        args += [jnp.tile(q_g, nq // q_g.shape[0]).reshape(1, nq), jnp.tile(k_g, nk // k_g.shape[0]).reshape(1, nk)]
    if rope is not None:
        per = DEC_SEQ // tm
        in_specs += [pl.BlockSpec((tm, 128), lambda i: (i % per, 0))] * 2
        args += list(rope)
    out_specs = [pl.BlockSpec((tm, nq), lambda i: (i, 0)), pl.BlockSpec((tm, 2 * nk), lambda i: (i, 0))]
    out_shape = [jax.ShapeDtypeStruct((rows, nq), BF16), jax.ShapeDtypeStruct((rows, 2 * nk), BF16)]
    if cache == "seq_minor":
        out_specs += [pl.BlockSpec((tm // SEQ, nk, SEQ), lambda i: (i, 0, 0))] * 2
        out_shape += [jax.ShapeDtypeStruct((rows // SEQ, nk, SEQ), F32)] * 2
    elif cache == "head_tile":
        out_specs += [pl.BlockSpec((tm, nk // 128, 128), lambda i: (i, 0, 0))] * 2
        out_shape += [jax.ShapeDtypeStruct((rows, nk // 128, 128), F32)] * 2
    return pl.pallas_call(
        functools.partial(_qk_prep_kernel, nq=nq, nk=nk, norm=norm, rope=rope is not None, scale=scale, cache=cache),
        grid=(rows // tm,),
        in_specs=in_specs,
        out_specs=out_specs,
        out_shape=out_shape,
        compiler_params=_cparams("parallel"),
        name="qk_prep",
    )(*args)


def rope_tables():
    t = jnp.arange(DEC_SEQ)
    row = (t // GRID_W).astype(F32)
    col = (t % GRID_W).astype(F32)
    half = A_HEAD_DIM // 4
    inv_freq = ROPE_THETA ** (-jnp.arange(half, dtype=F32) / half)
    ar = row[:, None] * inv_freq[None, :]
    ac = col[:, None] * inv_freq[None, :]
    cos = jnp.concatenate([jnp.cos(ar), jnp.cos(ar), jnp.cos(ac), jnp.cos(ac)], axis=1)
    sin = jnp.concatenate([-jnp.sin(ar), jnp.sin(ar), -jnp.sin(ac), jnp.sin(ac)], axis=1)
    return jnp.tile(cos, (1, 2)), jnp.tile(sin, (1, 2))


LOG2E = math.log2(math.e)


def _transpose_bf16(x):
    return x.astype(F32).T.astype(BF16)


ATT_TK = 256


def _fill_vt(vt_ref, kv_ref, kw):
    @pl.when(pl.program_id(1) == 0)
    def _():
        vt_ref[...] = _transpose_bf16(kv_ref[:, kw:])


def _attend(n, lk, k_piece, qts, vt_piece):
    out = []
    prev_sts = prev_m = None
    for s in range(n + 1):
        sts, m, l, acc = [], None, None, None
        for r0 in range(0, lk, ATT_TK):
            if s < n:
                st = _bdot(k_piece(s, r0), qts[s])
                m_c = jnp.max(st, axis=0, keepdims=True)
                m = m_c if m is None else jnp.maximum(m, m_c)
                sts.append(st)
            if s > 0:
                p = jnp.exp2(prev_sts[r0 // ATT_TK] - prev_m)
                l_c = jnp.sum(p, axis=0, keepdims=True)
                pv = _bdot(vt_piece(s - 1, r0), p.astype(BF16))
                l = l_c if l is None else l + l_c
                acc = pv if acc is None else acc + pv
        if s > 0:
            out.append((acc, l))
        prev_sts, prev_m = sts, m
    return out


def _gqa_kernel(q_ref, kv_ref, o_ref, vt_ref, *, tq):
    d = A_HEAD_DIM
    kvw = A_KV_HEADS * d
    _fill_vt(vt_ref, kv_ref, kvw)
    qt = _transpose_bf16(q_ref[...])
    qts = [jnp.concatenate([qt[(kh * A_GROUP + g) * d:(kh * A_GROUP + g + 1) * d] for g in range(A_GROUP)], axis=1)
           for kh in range(A_KV_HEADS)]

    k_piece = lambda kh, r0: kv_ref[r0:r0 + ATT_TK, kh * d:(kh + 1) * d]
    vt_piece = lambda kh, r0: vt_ref[kh * d:(kh + 1) * d, r0:r0 + ATT_TK]
    blocks = []
    for acc, l in _attend(A_KV_HEADS, kv_ref.shape[0], k_piece, qts, vt_piece):
        ot = acc / l
        blocks += [ot[:, g * tq:(g + 1) * tq] for g in range(A_GROUP)]
    o_ref[...] = jnp.concatenate(blocks, axis=0).T.astype(o_ref.dtype)


def gqa_attention(q, kv, *, nb, lq, lk, tq):
    nq = lq // tq
    kvw = kv.shape[1] // 2
    return pl.pallas_call(
        functools.partial(_gqa_kernel, tq=tq),
        grid=(nb, nq),
        in_specs=[
            pl.BlockSpec((tq, q.shape[1]), lambda b, i: (b * nq + i, 0)),
            pl.BlockSpec((lk, kv.shape[1]), lambda b, i: (b, 0), pipeline_mode=pl.Buffered(1)),
        ],
        out_specs=pl.BlockSpec((tq, q.shape[1]), lambda b, i: (b * nq + i, 0)),
        out_shape=jax.ShapeDtypeStruct(q.shape, BF16),
        scratch_shapes=[pltpu.VMEM((kvw, lk), BF16)],
        compiler_params=_cparams("parallel", "arbitrary"),
        name="gqa_attention",
    )(q, kv)


def _diff_kernel(q_ref, kv_ref, lam_ref, sg_ref, o_ref, vt_ref, *, lam_init):
    d = B_HEAD_DIM
    kw = B_HEADS * 2 * d
    _fill_vt(vt_ref, kv_ref, kw)
    lf = lam_ref[...]
    lam = (jnp.exp(jnp.sum(lf[0:1] * lf[1:2], axis=-1, keepdims=True))
           - jnp.exp(jnp.sum(lf[2:3] * lf[3:4], axis=-1, keepdims=True)) + lam_init)
    qt = _transpose_bf16(q_ref[...])

    qts = [qt[r * d:(r + 1) * d] for r in range(2 * B_HEADS)]
    k_piece = lambda r, r0: kv_ref[r0:r0 + ATT_TK, r * d:(r + 1) * d]
    vt_piece = lambda r, r0: vt_ref[(r // 2) * 2 * d:(r // 2 + 1) * 2 * d, r0:r0 + ATT_TK]
    res = _attend(2 * B_HEADS, kv_ref.shape[0], k_piece, qts, vt_piece)
    blocks = []
    for h in range(B_HEADS):
        (o1, l1), (o2, l2) = res[2 * h], res[2 * h + 1]
        ot = o1 * (1.0 / l1) - o2 * (lam / l2)
        ms = jnp.mean(ot * ot, axis=0, keepdims=True)
        blocks.append(((ot * lax.rsqrt(ms + B_SUBLN_EPS)) * sg_ref[...]) * (1.0 - lam_init))
    o_ref[...] = jnp.concatenate(blocks, axis=0).T.astype(o_ref.dtype)


def diff_attention(q, kv, lam, subln_g, *, nb, lq, lk, tq, lam_init):
    nq = lq // tq
    kw = kv.shape[1] // 2
    return pl.pallas_call(
        functools.partial(_diff_kernel, lam_init=lam_init),
        grid=(nb, nq),
        in_specs=[
            pl.BlockSpec((tq, q.shape[1]), lambda b, i: (b * nq + i, 0)),
            pl.BlockSpec((lk, kv.shape[1]), lambda b, i: (b, 0), pipeline_mode=pl.Buffered(1)),
            pl.BlockSpec(lam.shape, lambda b, i: (0, 0)),
            pl.BlockSpec((2 * B_HEAD_DIM, 1), lambda b, i: (0, 0)),
        ],
        out_specs=pl.BlockSpec((tq, q.shape[1]), lambda b, i: (b * nq + i, 0)),
        out_shape=jax.ShapeDtypeStruct(q.shape, BF16),
        scratch_shapes=[pltpu.VMEM((kw, lk), BF16)],
        compiler_params=_cparams("parallel", "arbitrary"),
        name="diff_attention",
    )(q, kv, lam, subln_g.reshape(2 * B_HEAD_DIM, 1))


def _decay(x, ld):
    return jnp.exp(-jnp.abs(x * ld))


def _ret_operands(q_ref, k_ref, v_ref, head=None):
    qk = slice(None) if head is None else slice(head * C_KEY_DIM, (head + 1) * C_KEY_DIM)
    vv = slice(None) if head is None else slice(head * C_VAL_DIM, (head + 1) * C_VAL_DIM)
    return q_ref[:, qk].astype(BF16), k_ref[:, qk].astype(F32) * (C_KEY_DIM ** -0.5), v_ref[:, vv].astype(BF16)


def _ret_intra(qb, k, vb, ld_f, ld_b):
    c_len = RET_CHUNK
    rel = (lax.broadcasted_iota(jnp.int32, (c_len, c_len), 0)
           - lax.broadcasted_iota(jnp.int32, (c_len, c_len), 1)).astype(F32)
    dmat = (jnp.where(rel >= 0, _decay(jnp.maximum(rel, 0.0), ld_f), 0.0)
            + jnp.where(rel <= 0, _decay(jnp.maximum(-rel, 0.0), ld_b), 0.0))
    a = lax.dot_general(qb, k.astype(BF16), (((1,), (1,)), ((), ())), preferred_element_type=F32)
    return _bdot((a * dmat).astype(BF16), vb)


def _ret_kv(k, vb, k_dec):
    kd = (k * k_dec).astype(BF16)
    return lax.dot_general(kd, vb, (((0,), (0,)), ((), ())), preferred_element_type=F32)


def _ret_idx():
    return lax.broadcasted_iota(jnp.int32, (RET_CHUNK, 1), 0).astype(F32)


def _ret_single_kernel(ld_ref, q_ref, k_ref, v_ref, o_ref, st_ref):
    idx = _ret_idx()
    for h in range(C_HEADS):
        ld_f, ld_b = ld_ref[0, h], ld_ref[1, h]
        qb, k, vb = _ret_operands(q_ref, k_ref, v_ref, h)
        o_ref[:, h * C_VAL_DIM:(h + 1) * C_VAL_DIM] = _ret_intra(qb, k, vb, ld_f, ld_b)
        st_ref[0, h] = _ret_kv(k, vb, _decay(RET_CHUNK - 1.0 - idx, ld_f))
        st_ref[1, h] = _ret_kv(k, vb, _decay(idx, ld_b))


def _ret_sweep_kernel(ld_ref, qf_ref, kf_ref, vf_ref, qr_ref, kr_ref, vr_ref, s0_ref, o_ref, st_ref, sf_ref, sr_ref, *, nc):
    c_len = RET_CHUNK
    h = pl.program_id(1)
    s = pl.program_id(2)
    ld_f, ld_b = ld_ref[0, h], ld_ref[1, h]
    idx = _ret_idx()
    full = jnp.full((1, 1), float(c_len), F32)

    @pl.when(s == 0)
    def _():
        sf_ref[...] = s0_ref[0]
        sr_ref[...] = s0_ref[1]
        o_ref[...] = jnp.zeros_like(o_ref)

    qb, k, vb = _ret_operands(qf_ref, kf_ref, vf_ref)
    row = pl.multiple_of(s * c_len, c_len)
    o_ref[pl.ds(row, c_len), :] += (_ret_intra(qb, k, vb, ld_f, ld_b)
                                    + _bdot(qb, sf_ref[...].astype(BF16)) * _decay(idx + 1.0, ld_f))
    sf_ref[...] = sf_ref[...] * _decay(full, ld_f) + _ret_kv(k, vb, _decay(c_len - 1.0 - idx, ld_f))

    qb, k, vb = _ret_operands(qr_ref, kr_ref, vr_ref)
    row = pl.multiple_of((nc - 1 - s) * c_len, c_len)
    o_ref[pl.ds(row, c_len), :] += _bdot(qb, sr_ref[...].astype(BF16)) * _decay(c_len - idx, ld_b)
    sr_ref[...] = sr_ref[...] * _decay(full, ld_b) + _ret_kv(k, vb, _decay(idx, ld_b))

    @pl.when(s == nc - 1)
    def _():
        st_ref[0] = sf_ref[...]
        st_ref[1] = sr_ref[...]


def retention(proj, log_decay, s0, *, row0, nb, seq):
    c_len = RET_CHUNK
    nc = seq // c_len
    r0 = row0 // c_len
    qw, vw = C_HEADS * C_KEY_DIM, C_HEADS * C_VAL_DIM
    out_shape = [jax.ShapeDtypeStruct((nb * seq, vw), F32),
                 jax.ShapeDtypeStruct((nb, 2, C_HEADS, C_KEY_DIM, C_VAL_DIM), F32)]
    smem = pl.BlockSpec(memory_space=pltpu.SMEM)
    if s0 is None:
        assert nc == 1
        return pl.pallas_call(
            _ret_single_kernel,
            grid=(nb,),
            in_specs=[smem,
                      pl.BlockSpec((c_len, qw), lambda b: (r0 + b, 0)),
                      pl.BlockSpec((c_len, qw), lambda b: (r0 + b, 1)),
                      pl.BlockSpec((c_len, vw), lambda b: (r0 + b, 2 * qw // vw))],
            out_specs=[pl.BlockSpec((seq, vw), lambda b: (b, 0)),
                       pl.BlockSpec((None, 2, C_HEADS, C_KEY_DIM, C_VAL_DIM), lambda b: (b, 0, 0, 0, 0))],
            out_shape=out_shape,
            compiler_params=_cparams("parallel"),
            name="retention_single",
        )(log_decay, proj, proj, proj)

    kblk, vblk = qw // C_KEY_DIM, 2 * qw // C_VAL_DIM
    fwd = lambda b, s: r0 + b * nc + s
    rev = lambda b, s: r0 + b * nc + nc - 1 - s
    qkv = lambda chunk: [pl.BlockSpec((c_len, C_KEY_DIM), lambda b, h, s: (chunk(b, s), h)),
                         pl.BlockSpec((c_len, C_KEY_DIM), lambda b, h, s: (chunk(b, s), kblk + h)),
                         pl.BlockSpec((c_len, C_VAL_DIM), lambda b, h, s: (chunk(b, s), vblk + h))]
    state = pl.BlockSpec((None, 2, None, C_KEY_DIM, C_VAL_DIM), lambda b, h, s: (b, 0, h, 0, 0))
    return pl.pallas_call(
        functools.partial(_ret_sweep_kernel, nc=nc),
        grid=(nb, C_HEADS, nc),
        in_specs=[smem] + qkv(fwd) + qkv(rev) + [state],
        out_specs=[pl.BlockSpec((seq, C_VAL_DIM), lambda b, h, s: (b, h)), state],
        out_shape=out_shape,
        scratch_shapes=[pltpu.VMEM((C_KEY_DIM, C_VAL_DIM), F32)] * 2,
        compiler_params=_cparams("parallel", "parallel", "arbitrary"),
        name="retention_sweep",
    )(log_decay, proj, proj, proj, proj, proj, proj, s0)


def _ret_out_kernel(oc_ref, ol_ref, g_ref, gn_ref, w_ref, x_ref, gate_ref, o_ref, *, tm):
    o = _pick_group(oc_ref, ol_ref, tm)
    parts = []
    for h in range(C_HEADS):
        oh = o[:, h * C_VAL_DIM:(h + 1) * C_VAL_DIM]
        ms = jnp.mean(oh * oh, axis=-1, keepdims=True)
        parts.append((oh * lax.rsqrt(ms + NORM_EPS)) * gn_ref[:, h * C_VAL_DIM:(h + 1) * C_VAL_DIM])
    a = (_silu(g_ref[...].astype(F32)) * jnp.concatenate(parts, axis=1)).astype(BF16)
    o_ref[...] = x_ref[...] + gate_ref[...] * _bdot(a, w_ref[...])


def retention_out(o_ctx, o_lat, proj, gn_g, w, x, gate, *, tm=512):
    t, d = x.shape
    vd = C_HEADS * C_VAL_DIM
    gblk = proj.shape[1] // vd - 1
    seg = lambda i: (_seg_of_tile(i, tm), 0, 0)
    return pl.pallas_call(
        functools.partial(_ret_out_kernel, tm=tm),
        grid=(t // tm,),
        in_specs=_group_specs(o_ctx, o_lat, tm, vd) + [
            pl.BlockSpec((tm, vd), lambda i: (i, gblk)),
            pl.BlockSpec((1, vd), lambda i: (0, 0)),
            pl.BlockSpec((vd, d), lambda i: (0, 0)),
            pl.BlockSpec((tm, d), lambda i: (i, 0)),
            pl.BlockSpec((None, 1, d), seg),
        ],
        out_specs=pl.BlockSpec((tm, d), lambda i: (i, 0)),
        out_shape=jax.ShapeDtypeStruct((t, d), F32),
        compiler_params=_cparams("parallel"),
        name="retention_out",
    )(o_ctx, o_lat, proj, gn_g.reshape(1, vd), w, x, gate)


def _filter_rows(t, seq, band_ref, w1_ref, b1_ref, w2_ref, b2_ref, w3_ref, fr_ref, delta_ref):
    t_norm = t / max(seq - 1, 1)
    lane = lax.broadcasted_iota(jnp.int32, (1, 128), 1)
    ang = (2.0 * math.pi * t) * band_ref[...] / seq
    feat = jnp.where(lane == 0, t_norm,
                     jnp.where(lane <= D_BANDS, jnp.cos(ang), jnp.where(lane <= 2 * D_BANDS, -jnp.sin(ang), 0.0)))
    a = jnp.sin(fr_ref[0:1] * (_bdot(feat.astype(BF16), w1_ref[...].astype(BF16)) + b1_ref[...]))
    a = jnp.sin(fr_ref[1:2] * (_bdot(a.astype(BF16), w2_ref[...].astype(BF16)) + b2_ref[...]))
    f = _bdot(a.astype(BF16), w3_ref[...].astype(BF16))
    window = jnp.exp(-t_norm * delta_ref[...]) + D_MOD_SHIFT
    return f * jnp.concatenate([window] * 4, axis=1)


def _filter_sum_kernel(band_ref, w1_ref, b1_ref, w2_ref, b2_ref, w3_ref, fr_ref, delta_ref, o_ref, *, seq, tm):
    i = pl.program_id(0)
    t = (i * tm + lax.broadcasted_iota(jnp.int32, (tm, 1), 0)).astype(F32)
    f = _filter_rows(t, seq, band_ref, w1_ref, b1_ref, w2_ref, b2_ref, w3_ref, fr_ref, delta_ref)
    part = jnp.sum(jnp.abs(f), axis=0, keepdims=True)

    @pl.when(i == 0)
    def _():
        o_ref[...] = part

    @pl.when(i > 0)
    def _():
        o_ref[...] += part


def _filter_gen_kernel(band_ref, w1_ref, b1_ref, w2_ref, b2_ref, w3_ref, fr_ref, delta_ref, sum_ref, o_ref, *, seq, tm):
    i = pl.program_id(0)
    r = i * tm + lax.broadcasted_iota(jnp.int32, (tm, 1), 0)
    t = jnp.where(r < seq, r, 2 * seq - r).astype(F32)
    f = _filter_rows(t, seq, band_ref, w1_ref, b1_ref, w2_ref, b2_ref, w3_ref, fr_ref, delta_ref)
    f = f / (sum_ref[...] + 1e-6)
    half = 2 * D_MODEL
    fwd, bwd = f[:, :half], f[:, half:]
    o_ref[...] = jnp.where(r < seq, fwd, jnp.where(r > seq, bwd, 0.0)) + jnp.where(r == 0, bwd, 0.0)


def hyena_circular_kernel(seq, band, w1p, b1, w2, b2, w3, freq, delta, *, tm):
    consts = [band, w1p, b1, w2, b2, w3, freq, delta]
    cspecs = [pl.BlockSpec(a.shape, lambda i: (0, 0)) for a in consts]
    nf = w3.shape[1]
    sums = pl.pallas_call(
        functools.partial(_filter_sum_kernel, seq=seq, tm=tm),
        grid=(seq // tm,),
        in_specs=cspecs,
        out_specs=pl.BlockSpec((1, nf), lambda i: (0, 0)),
        out_shape=jax.ShapeDtypeStruct((1, nf), F32),
        compiler_params=_cparams("arbitrary"),
        name="hyena_filter_sum",
    )(*consts)
    return pl.pallas_call(
        functools.partial(_filter_gen_kernel, seq=seq, tm=tm),
        grid=(2 * seq // tm,),
        in_specs=cspecs + [pl.BlockSpec((1, nf), lambda i: (0, 0))],
        out_specs=pl.BlockSpec((tm, nf // 2), lambda i: (i, 0)),
        out_shape=jax.ShapeDtypeStruct((2 * seq, nf // 2), F32),
        compiler_params=_cparams("parallel"),
        name="hyena_filter_gen",
    )(*consts, sums)


def _dft_cs(rows, cols, n):
    m = np.outer(np.arange(rows), np.arange(cols)) % n
    ang = 2.0 * np.pi * m / n
    return np.cos(ang), np.sin(ang)


def _stack_fwd(c, s):
    return np.block([[c, s], [-s, c]])


def _stack_inv(c, s):
    return np.block([[c, -s], [s, c]])


def _dft_mats():
    as_bf16 = lambda a: jnp.asarray(a, F32).astype(BF16)
    c1, s1 = _dft_cs(LAT_N1, LAT_N1, LAT_N1)
    c2, s2 = _dft_cs(LAT_N2, LAT_N2, LAT_N2)
    cc, sc = _dft_cs(CTX_N, CTX_N, CTX_N)
    h1, hc = LAT_N1 // 2, CTX_N // 2
    return dict(
        lat_g1=as_bf16(_kron_rows(_stack_fwd(c1[:, :h1], s1[:, :h1]))),
        lat_g1r=as_bf16(_kron_rows(np.concatenate([c1, -s1], axis=0))),
        lat_g2=as_bf16(_stack_fwd(c2, s2)),
        lat_g2i=as_bf16(_stack_inv(c2, s2)),
        lat_g1i=as_bf16(_kron_rows(_stack_inv(c1[:h1], s1[:h1]) / LAT_N)),
        ctx_g=as_bf16(_stack_fwd(cc[:, :hc], sc[:, :hc])),
        ctx_gr=as_bf16(np.concatenate([cc, -sc], axis=0)),
        ctx_gi=as_bf16(_stack_inv(cc[:hc], sc[:hc]) / CTX_N),
    )


def _lmul_kernel(g_ref, x_ref, o_ref):
    o_ref[...] = _bdot(g_ref[...], x_ref[...].astype(BF16))


def left_matmul(g, x, *, row_blk=0, tc):
    m, k = g.shape
    n = x.shape[1]
    return pl.pallas_call(
        _lmul_kernel,
        grid=(n // tc,),
        in_specs=[pl.BlockSpec((m, k), lambda j: (0, 0)), pl.BlockSpec((k, tc), lambda j: (row_blk, j))],
        out_specs=pl.BlockSpec((m, tc), lambda j: (0, j)),
        out_shape=jax.ShapeDtypeStruct((m, n), F32),
        compiler_params=_cparams("parallel"),
        name="left_matmul",
    )(g, x)


SLAB_ROWS = 8


def _kron_rows(g):
    return np.kron(g, np.eye(SLAB_ROWS))


def _slab_dot(g_ref, x_ref):
    k, r, w = x_ref.shape
    return _bdot(g_ref[...], x_ref[...].reshape(k * r, w).astype(BF16))


def _lmul_slab_kernel(g_ref, x_ref, o_ref):
    o_ref[...] = _slab_dot(g_ref, x_ref).reshape(o_ref.shape)


def left_matmul_slabs(g8, x3, *, row_blk=0):
    m, k = g8.shape[0] // SLAB_ROWS, g8.shape[1] // SLAB_ROWS
    _, s, w = x3.shape
    return pl.pallas_call(
        _lmul_slab_kernel,
        grid=(s // SLAB_ROWS,),
        in_specs=[pl.BlockSpec(g8.shape, lambda j: (0, 0)),
                  pl.BlockSpec((k, SLAB_ROWS, w), lambda j: (row_blk, j, 0))],
        out_specs=pl.BlockSpec((m, SLAB_ROWS, w), lambda j: (0, j, 0)),
        out_shape=jax.ShapeDtypeStruct((m, s, w), F32),
        compiler_params=_cparams("parallel"),
        name="left_matmul_slabs",
    )(g8, x3)


def _gated_skip(conv, gate, y, skip_ref):
    return gate * (conv + skip_ref[...] * y)


def _lat_last_kernel(g_ref, b_ref, gate_ref, y_ref, skip_ref, o_ref):
    conv = _slab_dot(g_ref, b_ref).reshape(o_ref.shape)
    o_ref[...] = _gated_skip(conv, gate_ref[...], y_ref[...], skip_ref).astype(o_ref.dtype)


def lat_last_stage(g, b3, gate3, gate_blk, y3, y_blk, skip, *, out_dtype):
    m, k = g.shape[0] // SLAB_ROWS, g.shape[1] // SLAB_ROWS
    _, s, w = b3.shape
    return pl.pallas_call(
        _lat_last_kernel,
        grid=(s // SLAB_ROWS,),
        in_specs=[
            pl.BlockSpec(g.shape, lambda j: (0, 0)),
            pl.BlockSpec((k, SLAB_ROWS, w), lambda j: (0, j, 0)),
            pl.BlockSpec((m, SLAB_ROWS, w), lambda j: (gate_blk, j, 0)),
            pl.BlockSpec((m, SLAB_ROWS, w), lambda j: (y_blk, j, 0)),
            pl.BlockSpec((1, w), lambda j: (0, 0)),
        ],
        out_specs=pl.BlockSpec((m, SLAB_ROWS, w), lambda j: (0, j, 0)),
        out_shape=jax.ShapeDtypeStruct((m, s, w), out_dtype),
        compiler_params=_cparams("parallel"),
        name="hyena_lat_last",
    )(g, b3, gate3, y3, skip)


def _lat_mid_kernel(a_ref, *rest, conv):
    if conv:
        h_ref, g2_ref, g2i_ref, o_ref = rest
    else:
        g2_ref, o_ref = rest
    k1 = pl.program_id(0)
    n2 = lax.broadcasted_iota(jnp.int32, (LAT_N2, 1), 0)
    ang = (k1 * n2).astype(F32) * (2.0 * math.pi / LAT_N)
    c, s = jnp.cos(ang), jnp.sin(ang)
    ar, ai = a_ref[0], a_ref[1]
    t = jnp.concatenate([ar * c + ai * s, ai * c - ar * s], axis=0).astype(BF16)
    x = _bdot(g2_ref[...], t)
    xr, xi = x[:LAT_N2], x[LAT_N2:]
    if not conv:
        o_ref[0] = xr
        o_ref[1] = xi
        return
    hr, hi = h_ref[0], h_ref[1]
    y = jnp.concatenate([xr * hr - xi * hi, xr * hi + xi * hr], axis=0).astype(BF16)
    b = _bdot(g2i_ref[...], y)
    br, bi = b[:LAT_N2], b[LAT_N2:]
    o_ref[0] = br * c - bi * s
    o_ref[1] = bi * c + br * s


def lat_mid_stage(a, mats, h=None, order=0):
    w = a.shape[-1]
    blk = lambda width, col: pl.BlockSpec((2, None, LAT_N2, width), lambda k1, j: (0, k1, 0, col(j)))
    gspec = pl.BlockSpec((2 * LAT_N2, 2 * LAT_N2), lambda k1, j: (0, 0))
    if h is None:
        in_specs = [blk(D_MODEL, lambda j: j), gspec]
        args = [a, mats["lat_g2"]]
    else:
        in_specs = [blk(D_MODEL, lambda j: j), blk(D_MODEL, lambda j: order), gspec, gspec]
        args = [a, h, mats["lat_g2"], mats["lat_g2i"]]
    return pl.pallas_call(
        functools.partial(_lat_mid_kernel, conv=h is not None),
        grid=(LAT_N1, w // D_MODEL),
        in_specs=in_specs,
        out_specs=blk(D_MODEL, lambda j: j),
        out_shape=jax.ShapeDtypeStruct(a.shape, F32),
        compiler_params=_cparams("parallel", "arbitrary"),
        name="hyena_lat_mid",
    )(*args)


def _ctx_conv_kernel(y_ref, gate_ref, h_ref, g_ref, gi_ref, skip_ref, o_ref):
    n = CTX_N
    y = y_ref[...]
    z = _bdot(g_ref[...], y.astype(BF16))
    zr, zi = z[:n], z[n:]
    hr, hi = h_ref[0:n], h_ref[n:]
    w = jnp.concatenate([zr * hr - zi * hi, zr * hi + zi * hr], axis=0).astype(BF16)
    conv = _bdot(gi_ref[...], w)
    o_ref[...] = _gated_skip(conv, gate_ref[...], y, skip_ref).astype(o_ref.dtype)


def ctx_conv(y, y_plane, gate, gate_plane, h, order, mats, skip, *, out_dtype):
    rows = 2 * SEQ
    return pl.pallas_call(
        _ctx_conv_kernel,
        grid=(BATCH // 2,),
        in_specs=[
            pl.BlockSpec((None, rows, D_MODEL), lambda p: (y_plane, p, 0)),
            pl.BlockSpec((None, rows, D_MODEL), lambda p: (gate_plane, p, 0)),
            pl.BlockSpec((2 * CTX_N, D_MODEL), lambda p: (0, order)),
            pl.BlockSpec((2 * CTX_N, rows), lambda p: (0, 0)),
            pl.BlockSpec((rows, 2 * CTX_N), lambda p: (0, 0)),
            pl.BlockSpec((1, D_MODEL), lambda p: (0, 0)),
        ],
        out_specs=pl.BlockSpec((rows, D_MODEL), lambda p: (p, 0)),
        out_shape=jax.ShapeDtypeStruct((T_CTX, D_MODEL), out_dtype),
        compiler_params=_cparams("parallel"),
        name="hyena_ctx_conv",
    )(y, gate, h, mats["ctx_g"], mats["ctx_gi"], skip)


def hyena_core(z3, f_w1, f_b1, f_w2, f_b2, f_w3, f_freq, f_skip):
    mats = _dft_mats()
    bands = jnp.linspace(1e-4, D_BANDS - 1, D_BANDS, dtype=F32)
    band = jnp.zeros((1, 128), F32).at[0, 1:1 + D_BANDS].set(bands).at[0, 1 + D_BANDS:1 + 2 * D_BANDS].set(bands)
    w1p = jnp.zeros((128, D_FILTER_HIDDEN), F32).at[:D_EMB].set(f_w1)
    max_decay = math.log(D_DECAY_TARGET) / D_FAST_DECAY_PCT
    min_decay = math.log(D_DECAY_TARGET) / D_SLOW_DECAY_PCT
    delta = jnp.abs(jnp.linspace(min_decay, max_decay, D_MODEL, dtype=F32)).reshape(1, D_MODEL)
    fargs = (band, w1p, f_b1.reshape(1, -1), f_w2, f_b2.reshape(1, -1), f_w3, f_freq, delta)

    kc_ctx = hyena_circular_kernel(SEQ, *fargs, tm=SEQ)
    h_ctx = left_matmul(mats["ctx_gr"], kc_ctx, tc=D_MODEL)
    kc_lat = hyena_circular_kernel(DEC_SEQ, *fargs, tm=512)
    a = left_matmul_slabs(mats["lat_g1r"], kc_lat.reshape(LAT_N1, LAT_N2, 2 * D_MODEL))
    h_lat = lat_mid_stage(a.reshape(2, LAT_N1, LAT_N2, 2 * D_MODEL), mats)

    y1 = ctx_conv(z3, 2, z3, 0, h_ctx, 0, mats, f_skip[0:1], out_dtype=F32)
    y_ctx = ctx_conv(y1[None], 0, z3, 1, h_ctx, 1, mats, f_skip[1:2], out_dtype=BF16)

    slabs = DEC_BATCH * LAT_N1 // 2
    z_slabs = z3.reshape(3 * T_ALL // LAT_N2, LAT_N2, D_MODEL)
    plane_blks = T_ALL // LAT_N2 // slabs
    lat_blk = lambda plane: plane * plane_blks + T_CTX // LAT_N2 // slabs
    y, y_blk = z_slabs, lat_blk(2)
    for n in range(2):
        a = left_matmul_slabs(mats["lat_g1"], y, row_blk=y_blk)
        b = lat_mid_stage(a.reshape(2, LAT_N1, LAT_N2, D_MODEL), mats, h_lat, n)
        y = lat_last_stage(mats["lat_g1i"], b.reshape(2 * LAT_N1, LAT_N2, D_MODEL), z_slabs, lat_blk(n), y, y_blk,
                           f_skip[n:n + 1], out_dtype=F32)
        y_blk = 0
    return y_ctx, y.reshape(T_LAT, D_MODEL)


def kernel(x_prompt, x_sample, cache_attn_k, cache_attn_v, cache_diff_k, cache_diff_v, state_ret, c, c_ctx, w_mod, b_mod, norm1_g, norm2_g, final_g, attn_w_qkv, attn_q_g, attn_k_g, attn_w_o, diff_w_qkv, diff_lambda, diff_subln_g, diff_w_o, ret_w_in, ret_log_decay, ret_gn_g, ret_w_o, hyena_w_in, hyena_sc_w, hyena_sc_b, hyena_f_w1, hyena_f_b1, hyena_f_w2, hyena_f_b2, hyena_f_w3, hyena_f_freq, hyena_f_skip, hyena_w_o, ffn_w_up, ffn_conv_w, ffn_conv_b, ffn_w_down):
    d = D_MODEL
    xs = (x_prompt.reshape(T_CTX, d), x_sample.reshape(T_LAT, d))
    cond =jnp.zeros((SEG_ROWS, d), F32).at[0].set(c_ctx).at[1:N_SEG].set(c)
    mod = modulation(cond, w_mod, b_mod)
    mod = mod.reshape(DEPTH, SEG_ROWS, 6, d).transpose(0, 2, 1, 3).reshape(DEPTH, 6, SEG_ROWS, 1, d)
    rope = rope_tables()
    bf = lambda w: w.astype(BF16)
    w_up, w_down = bf(ffn_w_up), bf(ffn_w_down)
    out = {}

    for l in range(DEPTH):
        m, j = l % 4, l // 4
        sh1, sc1, g1, sh2, sc2, g2 = (mod[l, i] for i in range(6))
        if m == 0:
            nq, nk = A_HEADS * A_HEAD_DIM, A_KV_HEADS * A_HEAD_DIM
            scale = A_HEAD_DIM ** -0.5 * LOG2E
            qkv = norm_proj(*xs, norm1_g[l], sh1, sc1, bf(attn_w_qkv[j]))
            q_c, kv_c, kt_c, vt_c = qk_prep(qkv, row0=0, rows=T_CTX, nq=nq, nk=nk, q_g=attn_q_g[j], k_g=attn_k_g[j],
                                            scale=scale, cache="seq_minor")
            q_l, kv_l = qk_prep(qkv, row0=T_CTX, rows=T_LAT, nq=nq, nk=nk, q_g=attn_q_g[j], k_g=attn_k_g[j],
                                rope=rope, scale=scale)
            as_cache = lambda t: t.reshape(BATCH, A_KV_HEADS, A_HEAD_DIM, SEQ).transpose(0, 3, 1, 2)[:, None]
            out["attn_k"] = as_cache(kt_c)
            out["attn_v"] = as_cache(vt_c)
            cache = jnp.concatenate([cache_attn_k[:, j].reshape(DEC_BATCH, PAST_LEN, nk),
                                     cache_attn_v[:, j].reshape(DEC_BATCH, PAST_LEN, nk)], axis=-1).astype(BF16)
            lk = PAST_LEN + DEC_SEQ
            kv_all = jnp.concatenate([cache, kv_l.reshape(DEC_BATCH, DEC_SEQ, 2 * nk)], axis=1).reshape(DEC_BATCH * lk, 2 * nk)
            o_c = gqa_attention(q_c, kv_c, nb=BATCH, lq=SEQ, lk=SEQ, tq=SEQ)
            o_l = gqa_attention(q_l, kv_all, nb=DEC_BATCH, lq=DEC_SEQ, lk=lk, tq=128)
            x = out_proj_residual(o_c, o_l, bf(attn_w_o[j]), *xs, g1)
        elif m == 1:
            nq = nk = B_HEADS * 2 * B_HEAD_DIM
            scale = B_HEAD_DIM ** -0.5 * LOG2E
            lam_init = 0.8 - 0.6 * math.exp(-0.3 * l)
            qkv = norm_proj(*xs, norm1_g[l], sh1, sc1, bf(diff_w_qkv[j]))
            q_c, kv_c, k3_c, v3_c = qk_prep(qkv, row0=0, rows=T_CTX, nq=nq, nk=nk, scale=scale, cache="head_tile")
            q_l, kv_l = qk_prep(qkv, row0=T_CTX, rows=T_LAT, nq=nq, nk=nk, rope=rope, scale=scale)
            out["diff_k"] = k3_c.reshape(BATCH, 1, SEQ, B_HEADS, 2 * B_HEAD_DIM)
            out["diff_v"] = v3_c.reshape(BATCH, 1, SEQ, B_HEADS, 2 * B_HEAD_DIM)
            cache = jnp.concatenate([cache_diff_k[:, j].reshape(DEC_BATCH, PAST_LEN, nk),
                                     cache_diff_v[:, j].reshape(DEC_BATCH, PAST_LEN, nk)], axis=-1).astype(BF16)
            lk = PAST_LEN + DEC_SEQ
            kv_all = jnp.concatenate([cache, kv_l.reshape(DEC_BATCH, DEC_SEQ, 2 * nk)], axis=1).reshape(DEC_BATCH * lk, 2 * nk)
            dargs = (diff_lambda[j], diff_subln_g[j])
            o_c = diff_attention(q_c, kv_c, *dargs, nb=BATCH, lq=SEQ, lk=SEQ, tq=SEQ, lam_init=lam_init)
            o_l = diff_attention(q_l, kv_all, *dargs, nb=DEC_BATCH, lq=DEC_SEQ, lk=lk, tq=256, lam_init=lam_init)
            x = out_proj_residual(o_c, o_l, bf(diff_w_o[j]), *xs, g1)
        elif m == 2:
            proj = norm_proj(*xs, norm1_g[l], sh1, sc1, bf(ret_w_in[j]), out_dtype=BF16)
            o_c, st = retention(proj, ret_log_decay[j], None, row0=0, nb=BATCH, seq=SEQ)
            o_l, _ = retention(proj, ret_log_decay[j], state_ret[:, j], row0=T_CTX, nb=DEC_BATCH, seq=DEC_SEQ)
            out["ret_s"] = st.reshape(BATCH, 1, 2, C_HEADS, C_KEY_DIM, C_VAL_DIM)
            x = retention_out(o_c, o_l, proj, ret_gn_g[j], bf(ret_w_o[j]), x, g1)
        else:
            z3 = norm_proj_conv(x, norm1_g[l], sh1, sc1, bf(hyena_w_in[j]), hyena_sc_w[j], hyena_sc_b[j])
            y_c, y_l = hyena_core(z3, hyena_f_w1[j], hyena_f_b1[j], hyena_f_w2[j], hyena_f_b2[j], hyena_f_w3[j],
                                  hyena_f_freq[j], hyena_f_skip[j])
            x = out_proj_residual(y_c, y_l, bf(hyena_w_o[j]), *xs, g1)
        x = conv_ffn(x, norm2_g[l], sh2, sc2, g2, w_up, ffn_conv_w[l], ffn_conv_b[l], w_down, l,
                     final_g if l == DEPTH - 1 else None)
        xs = (x, x)

    y_prompt = x[0].reshape(BATCH, SEQ, d)
    y_sample = x[1].reshape(DEC_BATCH, DEC_SEQ, d)
    return (y_prompt, y_sample, out["attn_k"], out["attn_v"], out["diff_k"], out["diff_v"], out["ret_s"])
```

```python
import functools
import math

import jax
import jax.numpy as jnp
import numpy as np
from jax import lax
from jax.experimental import pallas as pl
from jax.experimental.pallas import tpu as pltpu

F32 = jnp.float32
BF16 = jnp.bfloat16

D_MODEL = 1024
BATCH = 32
SEQ = 256
DEPTH = 4
DEC_BATCH = 2
DEC_SEQ = 4096
PAST_LEN = 256
GRID_W = 64
ROPE_THETA = 10000.0
NORM_EPS = 1e-6
A_HEADS = 16
A_KV_HEADS = 4
A_HEAD_DIM = 64
A_GROUP = A_HEADS // A_KV_HEADS
B_HEADS = 8
B_HEAD_DIM = 64
B_SUBLN_EPS = 1e-5
C_HEADS = 4
C_KEY_DIM = 256
C_VAL_DIM = 512
D_BANDS = 16
D_EMB = 1 + 2 * D_BANDS
D_FILTER_HIDDEN = 64
D_FAST_DECAY_PCT = 0.3
D_SLOW_DECAY_PCT = 1.5
D_DECAY_TARGET = 1e-2
D_MOD_SHIFT = 0.05
FFN_DIM = 2816

T_CTX = BATCH * SEQ
T_LAT = DEC_BATCH * DEC_SEQ
T_ALL = T_CTX + T_LAT
N_SEG = 1 + DEC_BATCH
SEG_ROWS = 8

HALO = 16
RET_CHUNK = 256
VMEM_LIMIT = 56 * 1024 * 1024

LAT_N = 2 * DEC_SEQ
LAT_N1 = 64
LAT_N2 = LAT_N // LAT_N1
CTX_N = 2 * SEQ


def _cparams(*sem):
    return pltpu.CompilerParams(dimension_semantics=sem, vmem_limit_bytes=VMEM_LIMIT)


def _seg_of_tile(i, tm):
    start = i * tm
    return jnp.where(start < T_CTX, 0, 1 + (start - T_CTX) // DEC_SEQ)


def _silu(x):
    return x * jax.nn.sigmoid(x)


def _bdot(a, b):
    return jnp.dot(a, b, preferred_element_type=F32)


def _norm_mod(x, g, sh, sc):
    ms = jnp.mean(x * x, axis=-1, keepdims=True)
    y = (x * lax.rsqrt(ms + NORM_EPS)) * g
    return y * (1.0 + sc) + sh


def _mod_kernel(c_ref, w_ref, b_ref, o_ref):
    s = _silu(c_ref[...]).astype(BF16)
    o_ref[...] = _bdot(s, w_ref[...].astype(BF16)) + b_ref[...]


def modulation(cond, w_mod, b_mod):
    tn = 1536
    n = w_mod.shape[-1]
    return pl.pallas_call(
        _mod_kernel,
        grid=(DEPTH, n // tn),
        in_specs=[
            pl.BlockSpec((SEG_ROWS, D_MODEL), lambda l, j: (0, 0)),
            pl.BlockSpec((None, D_MODEL, tn), lambda l, j: (l, 0, j)),
            pl.BlockSpec((None, 1, tn), lambda l, j: (l, 0, j)),
        ],
        out_specs=pl.BlockSpec((None, SEG_ROWS, tn), lambda l, j: (l, 0, j)),
        out_shape=jax.ShapeDtypeStruct((DEPTH, SEG_ROWS, n), F32),
        compiler_params=_cparams("arbitrary", "arbitrary"),
        name="modulation",
    )(cond, w_mod, b_mod.reshape(DEPTH, 1, n))


def _proj_kernel(xc_ref, xl_ref, g_ref, sh_ref, sc_ref, w_ref, o_ref, *, tm, tn):
    x = _pick_group(xc_ref, xl_ref, tm)
    h = _norm_mod(x, g_ref[...], sh_ref[...], sc_ref[...]).astype(BF16)
    for c in range(0, w_ref.shape[1], tn):
        o_ref[:, c:c + tn] = _bdot(h, w_ref[:, c:c + tn]).astype(o_ref.dtype)


def norm_proj(x_ctx, x_lat, g, sh, sc, w, *, tm=512, tn=1536, out_dtype=F32):
    d, n = w.shape
    seg = lambda i: (_seg_of_tile(i, tm), 0, 0)
    return pl.pallas_call(
        functools.partial(_proj_kernel, tm=tm, tn=tn),
        grid=(T_ALL // tm,),
        in_specs=_group_specs(x_ctx, x_lat, tm, d) + [
            pl.BlockSpec((1, d), lambda i: (0, 0)),
            pl.BlockSpec((None, 1, d), seg),
            pl.BlockSpec((None, 1, d), seg),
            pl.BlockSpec((d, n), lambda i: (0, 0), pipeline_mode=pl.Buffered(1)),
        ],
        out_specs=pl.BlockSpec((tm, n), lambda i: (i, 0)),
        out_shape=jax.ShapeDtypeStruct((T_ALL, n), out_dtype),
        compiler_params=_cparams("parallel"),
        name="norm_proj",
    )(x_ctx, x_lat, g.reshape(1, d), sh, sc, w)


def _conv3(u, cw, cb, i, tm):
    rows = u.shape[0]
    up = pltpu.roll(u, 1, 0)[HALO:HALO + tm]
    uc = u[HALO:HALO + tm]
    un = pltpu.roll(u, rows - 1, 0)[HALO:HALO + tm]
    sub = lax.broadcasted_iota(jnp.int32, (8, 1), 0)
    is_ctx = i * tm < T_CTX
    ups, uns = [], []
    for r in range(0, tm, SEQ):
        start = i * tm + r
        first = jnp.logical_or(is_ctx, (start & (DEC_SEQ - 1)) == 0)
        last = jnp.logical_or(is_ctx, ((start + SEQ) & (DEC_SEQ - 1)) == 0)
        ups += [jnp.where(jnp.logical_and(sub == 0, first), 0.0, up[r:r + 8]), up[r + 8:r + SEQ]]
        uns += [un[r:r + SEQ - 8], jnp.where(jnp.logical_and(sub == 7, last), 0.0, un[r + SEQ - 8:r + SEQ])]
    up = jnp.concatenate(ups, axis=0)
    un = jnp.concatenate(uns, axis=0)
    return up * cw[0:1] + uc * cw[1:2] + un * cw[2:3] + cb


def _halo_h(xp_ref, x, xn_ref, g_ref, sh_ref, sc_ref):
    g, sh, sc = g_ref[...], sh_ref[...], sc_ref[...]
    return jnp.concatenate([_norm_mod(xp_ref[...], g, sh, sc).astype(BF16), _norm_mod(x, g, sh, sc).astype(BF16),
                            _norm_mod(xn_ref[...], g, sh, sc).astype(BF16)], axis=0)


def _proj_conv_kernel(xp_ref, x_ref, xn_ref, g_ref, sh_ref, sc_ref, w_ref, cw_ref, cb_ref, o_ref, *, tm, tn):
    i = pl.program_id(0)
    h = _halo_h(xp_ref, x_ref[...], xn_ref, g_ref, sh_ref, sc_ref)
    d = o_ref.shape[-1]
    for c in range(0, w_ref.shape[1], tn):
        u = _conv3(_bdot(h, w_ref[:, c:c + tn]), cw_ref[:, c:c + tn], cb_ref[:, c:c + tn], i, tm)
        o_ref[c // d, :, c % d:c % d + tn] = u


def _ffn_kernel(xp_ref, x_ref, xn_ref, g_ref, sh_ref, sc_ref, gate_ref, wu_ref, cw_ref, cb_ref, wd_ref, *rest, tm, tn,
                final):
    if final:
        fg_ref, oc_ref, ol_ref = rest
    else:
        (o_ref,) = rest
    i = pl.program_id(0)
    x = x_ref[...]
    h = _halo_h(xp_ref, x, xn_ref, g_ref, sh_ref, sc_ref)
    f = wd_ref.shape[0]
    acts = []
    for c in range(0, f, tn):
        a = _conv3(_bdot(h, wu_ref[:, c:c + tn]), cw_ref[:, c:c + tn], cb_ref[:, c:c + tn], i, tm)
        b = _conv3(_bdot(h, wu_ref[:, f + c:f + c + tn]), cw_ref[:, f + c:f + c + tn], cb_ref[:, f + c:f + c + tn], i, tm)
        acts.append((_silu(a) * b).astype(BF16))
    y = x + gate_ref[...] * _bdot(jnp.concatenate(acts, axis=1), wd_ref[...])
    if not final:
        o_ref[...] = y
        return
    ms = jnp.mean(y * y, axis=-1, keepdims=True)
    y = (y * lax.rsqrt(ms + NORM_EPS)) * fg_ref[...]
    is_ctx = i < T_CTX // tm

    @pl.when(is_ctx)
    def _():
        oc_ref[...] = y

    @pl.when(jnp.logical_not(is_ctx))
    def _():
        ol_ref[...] = y


def _halo_specs(t, tm, d):
    per = tm // HALO
    last_blk = t // HALO - 1
    return [
        pl.BlockSpec((HALO, d), lambda i, *_: (jnp.maximum(i * per - 1, 0), 0)),
        pl.BlockSpec((tm, d), lambda i, *_: (i, 0)),
        pl.BlockSpec((HALO, d), lambda i, *_: (jnp.minimum((i + 1) * per, last_blk), 0)),
    ]


def norm_proj_conv(x, g, sh, sc, w, cw, cb, *, tm=512, tn=512):
    t, d = x.shape
    n = w.shape[1]
    seg = lambda i: (_seg_of_tile(i, tm), 0, 0)
    whole = lambda a: pl.BlockSpec(a.shape, lambda i: (0, 0), pipeline_mode=pl.Buffered(1))
    cb = cb.reshape(1, n)
    return pl.pallas_call(
        functools.partial(_proj_conv_kernel, tm=tm, tn=tn),
        grid=(t // tm,),
        in_specs=_halo_specs(t, tm, d) + [
            pl.BlockSpec((1, d), lambda i: (0, 0)),
            pl.BlockSpec((None, 1, d), seg),
            pl.BlockSpec((None, 1, d), seg),
            whole(w), whole(cw), whole(cb),
        ],
        out_specs=pl.BlockSpec((n // d, tm, d), lambda i: (0, i, 0)),
        out_shape=jax.ShapeDtypeStruct((n // d, t, d), F32),
        compiler_params=_cparams("parallel"),
        name="norm_proj_conv",
    )(x, x, x, g.reshape(1, d), sh, sc, w, cw, cb)


def conv_ffn(x, g, sh, sc, gate, w_up, cw, cb, w_down, layer, final_g=None, *, tm=1024, tn=256):
    t, d = x.shape
    f = w_down.shape[1]
    seg = lambda i: (_seg_of_tile(i, tm), 0, 0)
    whole = lambda a: pl.BlockSpec(a.shape, lambda i: (0, 0), pipeline_mode=pl.Buffered(1))
    of_layer = lambda a: pl.BlockSpec((None,) + a.shape[1:], lambda i: (layer, 0, 0), pipeline_mode=pl.Buffered(1))
    cb = cb.reshape(1, 2 * f)
    in_specs = _halo_specs(t, tm, d) + [
        pl.BlockSpec((1, d), lambda i: (0, 0)),
        pl.BlockSpec((None, 1, d), seg),
        pl.BlockSpec((None, 1, d), seg),
        pl.BlockSpec((None, 1, d), seg),
        of_layer(w_up), whole(cw), whole(cb), of_layer(w_down),
    ]
    args = [x, x, x, g.reshape(1, d), sh, sc, gate, w_up, cw, cb, w_down]
    out_specs = pl.BlockSpec((tm, d), lambda i: (i, 0))
    out_shape = jax.ShapeDtypeStruct((t, d), F32)
    if final_g is not None:
        nctx = T_CTX // tm
        in_specs.append(pl.BlockSpec((1, d), lambda i: (0, 0)))
        args.append(final_g.reshape(1, d))
        out_specs = [pl.BlockSpec((tm, d), lambda i: (jnp.minimum(i, nctx - 1), 0)),
                     pl.BlockSpec((tm, d), lambda i: (jnp.maximum(i - nctx, 0), 0))]
        out_shape = [jax.ShapeDtypeStruct((T_CTX, d), F32), jax.ShapeDtypeStruct((T_LAT, d), F32)]
    return pl.pallas_call(
        functools.partial(_ffn_kernel, tm=tm, tn=tn, final=final_g is not None),
        grid=(t // tm,),
        in_specs=in_specs,
        out_specs=out_specs,
        out_shape=out_shape,
        compiler_params=_cparams("arbitrary"),
        name="conv_ffn",
    )(*args)


def _group_specs(a_ctx, a_lat, tm, width):
    nctx = T_CTX // tm
    off = nctx if a_lat.shape[0] == T_ALL else 0
    return [
        pl.BlockSpec((tm, width), lambda i, *_: (jnp.minimum(i, nctx - 1), 0)),
        pl.BlockSpec((tm, width), lambda i, *_: (off + jnp.maximum(i - nctx, 0), 0)),
    ]


def _pick_group(ac_ref, al_ref, tm):
    return jnp.where(pl.program_id(0) < T_CTX // tm, ac_ref[...], al_ref[...])


def _out_proj_kernel(ac_ref, al_ref, w_ref, xc_ref, xl_ref, gate_ref, o_ref, *, tm):
    a = _pick_group(ac_ref, al_ref, tm).astype(BF16)
    o_ref[...] = _pick_group(xc_ref, xl_ref, tm) + gate_ref[...] * _bdot(a, w_ref[...])


def out_proj_residual(a_ctx, a_lat, w, x_ctx, x_lat, gate, *, tm=512):
    k, d = w.shape
    seg = lambda i: (_seg_of_tile(i, tm), 0, 0)
    return pl.pallas_call(
        functools.partial(_out_proj_kernel, tm=tm),
        grid=(T_ALL // tm,),
        in_specs=_group_specs(a_ctx, a_lat, tm, k) + [pl.BlockSpec((k, d), lambda i: (0, 0))]
        + _group_specs(x_ctx, x_lat, tm, d) + [pl.BlockSpec((None, 1, d), seg)],
        out_specs=pl.BlockSpec((tm, d), lambda i: (i, 0)),
        out_shape=jax.ShapeDtypeStruct((T_ALL, d), F32),
        compiler_params=_cparams("parallel"),
        name="out_proj_residual",
    )(a_ctx, a_lat, w, x_ctx, x_lat, gate)


def _head_mean_sq(x, head_dim):
    n = x.shape[1]
    x2 = x * x
    hi = x2.astype(BF16)
    lo = (x2 - hi.astype(F32)).astype(BF16)
    blk = 256
    r = lax.broadcasted_iota(jnp.int32, (blk, blk), 0) // head_dim
    c = lax.broadcasted_iota(jnp.int32, (blk, blk), 1) // head_dim
    ones = (r == c).astype(BF16)
    parts = []
    for s in range(0, n, blk):
        parts.append(_bdot(hi[:, s:s + blk], ones) + _bdot(lo[:, s:s + blk], ones))
    ss = parts[0] if len(parts) == 1 else jnp.concatenate(parts, axis=1)
    return ss * (1.0 / head_dim)


def _rope(x, cos, sin):
    n = x.shape[1]
    lane = lax.broadcasted_iota(jnp.int32, (1, 128), 1)
    lower = (lane & 31) < 16
    outs = []
    for s in range(0, n, 128):
        xs = x[:, s:s + 128]
        partner = jnp.where(lower, pltpu.roll(xs, 128 - 16, 1), pltpu.roll(xs, 16, 1))
        outs.append(xs * cos + partner * sin)
    return jnp.concatenate(outs, axis=1)


def _store_cache(o_ref, x, cache):
    if cache == "seq_minor":
        for b in range(o_ref.shape[0]):
            o_ref[b] = x[b * SEQ:(b + 1) * SEQ].T
    else:
        for h in range(o_ref.shape[1]):
            o_ref[:, h, :] = x[:, h * 128:(h + 1) * 128]


def _qk_prep_kernel(*refs, nq, nk, norm, rope, scale, cache):
    it = iter(refs)
    q_ref, k_ref, v_ref = next(it), next(it), next(it)
    if norm:
        qg_ref, kg_ref = next(it), next(it)
    if rope:
        cos_ref, sin_ref = next(it), next(it)
    qo_ref, kvo_ref = next(it), next(it)
    q = q_ref[...]
    k = k_ref[...]
    v = v_ref[...]
    if norm:
        q = (q * lax.rsqrt(_head_mean_sq(q, A_HEAD_DIM) + NORM_EPS)) * qg_ref[...]
        k = (k * lax.rsqrt(_head_mean_sq(k, A_HEAD_DIM) + NORM_EPS)) * kg_ref[...]
    if rope:
        cos, sin = cos_ref[...], sin_ref[...]
        q = _rope(q, cos, sin)
        k = _rope(k, cos, sin)
    qo_ref[...] = (q * scale).astype(BF16)
    kvo_ref[:, 0:nk] = k.astype(BF16)
    kvo_ref[:, nk:] = v.astype(BF16)
    if cache is not None:
        _store_cache(next(it), k, cache)
        _store_cache(next(it), v, cache)


def qk_prep(qkv, *, row0, rows, nq, nk, q_g=None, k_g=None, rope=None, scale, cache=None, tm=512):
    norm = q_g is not None
    r0 = row0 // tm
    qb = nq // nk
    in_specs = [
        pl.BlockSpec((tm, nq), lambda i: (i + r0, 0)),
        pl.BlockSpec((tm, nk), lambda i: (i + r0, qb)),
        pl.BlockSpec((tm, nk), lambda i: (i + r0, qb + 1)),
    ]
    args = [qkv, qkv, qkv]
    if norm:
        in_specs += [pl.BlockSpec((1, nq), lambda i: (0, 0)), pl.BlockSpec((1, nk), lambda i: (0, 0))]
        args += [jnp.tile(q_g, nq // q_g.shape[0]).reshape(1, nq), jnp.tile(k_g, nk // k_g.shape[0]).reshape(1, nk)]
    if rope is not None:
        per = DEC_SEQ // tm
        in_specs += [pl.BlockSpec((tm, 128), lambda i: (i % per, 0))] * 2
        args += list(rope)
    out_specs = [pl.BlockSpec((tm, nq), lambda i: (i, 0)), pl.BlockSpec((tm, 2 * nk), lambda i: (i, 0))]
    out_shape = [jax.ShapeDtypeStruct((rows, nq), BF16), jax.ShapeDtypeStruct((rows, 2 * nk), BF16)]
    if cache == "seq_minor":
        out_specs += [pl.BlockSpec((tm // SEQ, nk, SEQ), lambda i: (i, 0, 0))] * 2
        out_shape += [jax.ShapeDtypeStruct((rows // SEQ, nk, SEQ), F32)] * 2
    elif cache == "head_tile":
        out_specs += [pl.BlockSpec((tm, nk // 128, 128), lambda i: (i, 0, 0))] * 2
        out_shape += [jax.ShapeDtypeStruct((rows, nk // 128, 128), F32)] * 2
    return pl.pallas_call(
        functools.partial(_qk_prep_kernel, nq=nq, nk=nk, norm=norm, rope=rope is not None, scale=scale, cache=cache),
        grid=(rows // tm,),
        in_specs=in_specs,
        out_specs=out_specs,
        out_shape=out_shape,
        compiler_params=_cparams("parallel"),
        name="qk_prep",
    )(*args)


def rope_tables():
    t = jnp.arange(DEC_SEQ)
    row = (t // GRID_W).astype(F32)
    col = (t % GRID_W).astype(F32)
    half = A_HEAD_DIM // 4
    inv_freq = ROPE_THETA ** (-jnp.arange(half, dtype=F32) / half)
    ar = row[:, None] * inv_freq[None, :]
    ac = col[:, None] * inv_freq[None, :]
    cos = jnp.concatenate([jnp.cos(ar), jnp.cos(ar), jnp.cos(ac), jnp.cos(ac)], axis=1)
    sin = jnp.concatenate([-jnp.sin(ar), jnp.sin(ar), -jnp.sin(ac), jnp.sin(ac)], axis=1)
    return jnp.tile(cos, (1, 2)), jnp.tile(sin, (1, 2))


LOG2E = math.log2(math.e)


def _transpose_bf16(x):
    return x.astype(F32).T.astype(BF16)


ATT_TK = 256
GQA_STREAM_LANES = 512


def _fill_vt(vt_ref, kv_ref, kw):
    @pl.when(pl.program_id(1) == 0)
    def _():
        vt_ref[...] = _transpose_bf16(kv_ref[:, kw:])


def _attend(n, lk, k_piece, qts, vt_piece):
    out = []
    prev_sts = prev_m = None
    for s in range(n + 1):
        sts, m, l, acc = [], None, None, None
        for r0 in range(0, lk, ATT_TK):
            if s < n:
                st = _bdot(k_piece(s, r0), qts[s])
                m_c = jnp.max(st, axis=0, keepdims=True)
                m = m_c if m is None else jnp.maximum(m, m_c)
                sts.append(st)
            if s > 0:
                p = jnp.exp2(prev_sts[r0 // ATT_TK] - prev_m)
                l_c = jnp.sum(p, axis=0, keepdims=True)
                pv = _bdot(vt_piece(s - 1, r0), p.astype(BF16))
                l = l_c if l is None else l + l_c
                acc = pv if acc is None else acc + pv
        if s > 0:
            out.append((acc, l))
        prev_sts, prev_m = sts, m
    return out


def _gqa_kernel(q_ref, kv_ref, o_ref, vt_ref, *, tq):
    d = A_HEAD_DIM
    kvw = A_KV_HEADS * d
    _fill_vt(vt_ref, kv_ref, kvw)
    qt = _transpose_bf16(q_ref[...])
    per = max(1, min(A_GROUP, GQA_STREAM_LANES // tq))
    heads = [(kh, kh * A_GROUP + g0) for kh in range(A_KV_HEADS) for g0 in range(0, A_GROUP, per)]
    qts = [jnp.concatenate([qt[(h0 + g) * d:(h0 + g + 1) * d] for g in range(per)], axis=1) for _, h0 in heads]

    k_piece = lambda s, r0: kv_ref[r0:r0 + ATT_TK, heads[s][0] * d:(heads[s][0] + 1) * d]
    vt_piece = lambda s, r0: vt_ref[heads[s][0] * d:(heads[s][0] + 1) * d, r0:r0 + ATT_TK]
    blocks = []
    for acc, l in _attend(len(heads), kv_ref.shape[0], k_piece, qts, vt_piece):
        ot = acc / l
        blocks += [ot[:, g * tq:(g + 1) * tq] for g in range(per)]
    o_ref[...] = jnp.concatenate(blocks, axis=0).T.astype(o_ref.dtype)


def gqa_attention(q, kv, *, nb, lq, lk, tq):
    nq = lq // tq
    kvw = kv.shape[1] // 2
    return pl.pallas_call(
        functools.partial(_gqa_kernel, tq=tq),
        grid=(nb, nq),
        in_specs=[
            pl.BlockSpec((tq, q.shape[1]), lambda b, i: (b * nq + i, 0)),
            pl.BlockSpec((lk, kv.shape[1]), lambda b, i: (b, 0), pipeline_mode=pl.Buffered(1)),
        ],
        out_specs=pl.BlockSpec((tq, q.shape[1]), lambda b, i: (b * nq + i, 0)),
        out_shape=jax.ShapeDtypeStruct(q.shape, BF16),
        scratch_shapes=[pltpu.VMEM((kvw, lk), BF16)],
        compiler_params=_cparams("parallel", "arbitrary"),
        name="gqa_attention",
    )(q, kv)


def _diff_kernel(q_ref, kv_ref, lam_ref, sg_ref, o_ref, vt_ref, *, lam_init):
    d = B_HEAD_DIM
    kw = B_HEADS * 2 * d
    _fill_vt(vt_ref, kv_ref, kw)
    lf = lam_ref[...]
    lam = (jnp.exp(jnp.sum(lf[0:1] * lf[1:2], axis=-1, keepdims=True))
           - jnp.exp(jnp.sum(lf[2:3] * lf[3:4], axis=-1, keepdims=True)) + lam_init)
    qt = _transpose_bf16(q_ref[...])

    qts = [qt[r * d:(r + 1) * d] for r in range(2 * B_HEADS)]
    k_piece = lambda r, r0: kv_ref[r0:r0 + ATT_TK, r * d:(r + 1) * d]
    vt_piece = lambda r, r0: vt_ref[(r // 2) * 2 * d:(r // 2 + 1) * 2 * d, r0:r0 + ATT_TK]
    res = _attend(2 * B_HEADS, kv_ref.shape[0], k_piece, qts, vt_piece)
    blocks = []
    for h in range(B_HEADS):
        (o1, l1), (o2, l2) = res[2 * h], res[2 * h + 1]
        ot = o1 * (1.0 / l1) - o2 * (lam / l2)
        ms = jnp.mean(ot * ot, axis=0, keepdims=True)
        blocks.append(((ot * lax.rsqrt(ms + B_SUBLN_EPS)) * sg_ref[...]) * (1.0 - lam_init))
    o_ref[...] = jnp.concatenate(blocks, axis=0).T.astype(o_ref.dtype)


def diff_attention(q, kv, lam, subln_g, *, nb, lq, lk, tq, lam_init):
    nq = lq // tq
    kw = kv.shape[1] // 2
    return pl.pallas_call(
        functools.partial(_diff_kernel, lam_init=lam_init),
        grid=(nb, nq),
        in_specs=[
            pl.BlockSpec((tq, q.shape[1]), lambda b, i: (b * nq + i, 0)),
            pl.BlockSpec((lk, kv.shape[1]), lambda b, i: (b, 0), pipeline_mode=pl.Buffered(1)),
            pl.BlockSpec(lam.shape, lambda b, i: (0, 0)),
            pl.BlockSpec((2 * B_HEAD_DIM, 1), lambda b, i: (0, 0)),
        ],
        out_specs=pl.BlockSpec((tq, q.shape[1]), lambda b, i: (b * nq + i, 0)),
        out_shape=jax.ShapeDtypeStruct(q.shape, BF16),
        scratch_shapes=[pltpu.VMEM((kw, lk), BF16)],
        compiler_params=_cparams("parallel", "arbitrary"),
        name="diff_attention",
    )(q, kv, lam, subln_g.reshape(2 * B_HEAD_DIM, 1))


def _decay(x, ld):
    return jnp.exp(-jnp.abs(x * ld))


def _ret_operands(q_ref, k_ref, v_ref, head=None):
    qk = slice(None) if head is None else slice(head * C_KEY_DIM, (head + 1) * C_KEY_DIM)
    vv = slice(None) if head is None else slice(head * C_VAL_DIM, (head + 1) * C_VAL_DIM)
    return q_ref[:, qk].astype(BF16), k_ref[:, qk].astype(F32) * (C_KEY_DIM ** -0.5), v_ref[:, vv].astype(BF16)


def _ret_intra(qb, k, vb, ld_f, ld_b):
    c_len = RET_CHUNK
    rel = (lax.broadcasted_iota(jnp.int32, (c_len, c_len), 0)
           - lax.broadcasted_iota(jnp.int32, (c_len, c_len), 1)).astype(F32)
    dmat = (jnp.where(rel >= 0, _decay(jnp.maximum(rel, 0.0), ld_f), 0.0)
            + jnp.where(rel <= 0, _decay(jnp.maximum(-rel, 0.0), ld_b), 0.0))
    a = lax.dot_general(qb, k.astype(BF16), (((1,), (1,)), ((), ())), preferred_element_type=F32)
    return _bdot((a * dmat).astype(BF16), vb)


def _ret_kv(k, vb, k_dec):
    kd = (k * k_dec).astype(BF16)
    return lax.dot_general(kd, vb, (((0,), (0,)), ((), ())), preferred_element_type=F32)


def _ret_idx():
    return lax.broadcasted_iota(jnp.int32, (RET_CHUNK, 1), 0).astype(F32)


def _ret_single_kernel(ld_ref, q_ref, k_ref, v_ref, o_ref, st_ref):
    idx = _ret_idx()
    for h in range(C_HEADS):
        ld_f, ld_b = ld_ref[0, h], ld_ref[1, h]
        qb, k, vb = _ret_operands(q_ref, k_ref, v_ref, h)
        o_ref[:, h * C_VAL_DIM:(h + 1) * C_VAL_DIM] = _ret_intra(qb, k, vb, ld_f, ld_b)
        st_ref[0, h] = _ret_kv(k, vb, _decay(RET_CHUNK - 1.0 - idx, ld_f))
        st_ref[1, h] = _ret_kv(k, vb, _decay(idx, ld_b))


def _ret_sweep_kernel(ld_ref, qf_ref, kf_ref, vf_ref, qr_ref, kr_ref, vr_ref, s0_ref, o_ref, st_ref, sf_ref, sr_ref, *, nc):
    c_len = RET_CHUNK
    h = pl.program_id(1)
    s = pl.program_id(2)
    ld_f, ld_b = ld_ref[0, h], ld_ref[1, h]
    idx = _ret_idx()
    full = jnp.full((1, 1), float(c_len), F32)

    @pl.when(s == 0)
    def _():
        sf_ref[...] = s0_ref[0]
        sr_ref[...] = s0_ref[1]
        o_ref[...] = jnp.zeros_like(o_ref)

    qb, k, vb = _ret_operands(qf_ref, kf_ref, vf_ref)
    row = pl.multiple_of(s * c_len, c_len)
    o_ref[pl.ds(row, c_len), :] += (_ret_intra(qb, k, vb, ld_f, ld_b)
                                    + _bdot(qb, sf_ref[...].astype(BF16)) * _decay(idx + 1.0, ld_f))
    sf_ref[...] = sf_ref[...] * _decay(full, ld_f) + _ret_kv(k, vb, _decay(c_len - 1.0 - idx, ld_f))

    qb, k, vb = _ret_operands(qr_ref, kr_ref, vr_ref)
    row = pl.multiple_of((nc - 1 - s) * c_len, c_len)
    o_ref[pl.ds(row, c_len), :] += _bdot(qb, sr_ref[...].astype(BF16)) * _decay(c_len - idx, ld_b)
    sr_ref[...] = sr_ref[...] * _decay(full, ld_b) + _ret_kv(k, vb, _decay(idx, ld_b))

    @pl.when(s == nc - 1)
    def _():
        st_ref[0] = sf_ref[...]
        st_ref[1] = sr_ref[...]


def retention(proj, log_decay, s0, *, row0, nb, seq):
    c_len = RET_CHUNK
    nc = seq // c_len
    r0 = row0 // c_len
    qw, vw = C_HEADS * C_KEY_DIM, C_HEADS * C_VAL_DIM
    out_shape = [jax.ShapeDtypeStruct((nb * seq, vw), F32),
                 jax.ShapeDtypeStruct((nb, 2, C_HEADS, C_KEY_DIM, C_VAL_DIM), F32)]
    smem = pl.BlockSpec(memory_space=pltpu.SMEM)
    if s0 is None:
        assert nc == 1
        return pl.pallas_call(
            _ret_single_kernel,
            grid=(nb,),
            in_specs=[smem,
                      pl.BlockSpec((c_len, qw), lambda b: (r0 + b, 0)),
                      pl.BlockSpec((c_len, qw), lambda b: (r0 + b, 1)),
                      pl.BlockSpec((c_len, vw), lambda b: (r0 + b, 2 * qw // vw))],
            out_specs=[pl.BlockSpec((seq, vw), lambda b: (b, 0)),
                       pl.BlockSpec((None, 2, C_HEADS, C_KEY_DIM, C_VAL_DIM), lambda b: (b, 0, 0, 0, 0))],
            out_shape=out_shape,
            compiler_params=_cparams("parallel"),
            name="retention_single",
        )(log_decay, proj, proj, proj)

    kblk, vblk = qw // C_KEY_DIM, 2 * qw // C_VAL_DIM
    fwd = lambda b, s: r0 + b * nc + s
    rev = lambda b, s: r0 + b * nc + nc - 1 - s
    qkv = lambda chunk: [pl.BlockSpec((c_len, C_KEY_DIM), lambda b, h, s: (chunk(b, s), h)),
                         pl.BlockSpec((c_len, C_KEY_DIM), lambda b, h, s: (chunk(b, s), kblk + h)),
                         pl.BlockSpec((c_len, C_VAL_DIM), lambda b, h, s: (chunk(b, s), vblk + h))]
    state = pl.BlockSpec((None, 2, None, C_KEY_DIM, C_VAL_DIM), lambda b, h, s: (b, 0, h, 0, 0))
    return pl.pallas_call(
        functools.partial(_ret_sweep_kernel, nc=nc),
        grid=(nb, C_HEADS, nc),
        in_specs=[smem] + qkv(fwd) + qkv(rev) + [state],
        out_specs=[pl.BlockSpec((seq, C_VAL_DIM), lambda b, h, s: (b, h)), state],
        out_shape=out_shape,
        scratch_shapes=[pltpu.VMEM((C_KEY_DIM, C_VAL_DIM), F32)] * 2,
        compiler_params=_cparams("parallel", "parallel", "arbitrary"),
        name="retention_sweep",
    )(log_decay, proj, proj, proj, proj, proj, proj, s0)


def _ret_out_kernel(oc_ref, ol_ref, g_ref, gn_ref, w_ref, x_ref, gate_ref, o_ref, *, tm):
    o = _pick_group(oc_ref, ol_ref, tm)
    parts = []
    for h in range(C_HEADS):
        oh = o[:, h * C_VAL_DIM:(h + 1) * C_VAL_DIM]
        ms = jnp.mean(oh * oh, axis=-1, keepdims=True)
        parts.append((oh * lax.rsqrt(ms + NORM_EPS)) * gn_ref[:, h * C_VAL_DIM:(h + 1) * C_VAL_DIM])
    a = (_silu(g_ref[...].astype(F32)) * jnp.concatenate(parts, axis=1)).astype(BF16)
    o_ref[...] = x_ref[...] + gate_ref[...] * _bdot(a, w_ref[...])


def retention_out(o_ctx, o_lat, proj, gn_g, w, x, gate, *, tm=512):
    t, d = x.shape
    vd = C_HEADS * C_VAL_DIM
    gblk = proj.shape[1] // vd - 1
    seg = lambda i: (_seg_of_tile(i, tm), 0, 0)
    return pl.pallas_call(
        functools.partial(_ret_out_kernel, tm=tm),
        grid=(t // tm,),
        in_specs=_group_specs(o_ctx, o_lat, tm, vd) + [
            pl.BlockSpec((tm, vd), lambda i: (i, gblk)),
            pl.BlockSpec((1, vd), lambda i: (0, 0)),
            pl.BlockSpec((vd, d), lambda i: (0, 0)),
            pl.BlockSpec((tm, d), lambda i: (i, 0)),
            pl.BlockSpec((None, 1, d), seg),
        ],
        out_specs=pl.BlockSpec((tm, d), lambda i: (i, 0)),
        out_shape=jax.ShapeDtypeStruct((t, d), F32),
        compiler_params=_cparams("parallel"),
        name="retention_out",
    )(o_ctx, o_lat, proj, gn_g.reshape(1, vd), w, x, gate)


def _filter_rows(t, seq, band_ref, w1_ref, b1_ref, w2_ref, b2_ref, w3_ref, fr_ref, delta_ref):
    t_norm = t / max(seq - 1, 1)
    lane = lax.broadcasted_iota(jnp.int32, (1, 128), 1)
    ang = (2.0 * math.pi * t) * band_ref[...] / seq
    feat = jnp.where(lane == 0, t_norm,
                     jnp.where(lane <= D_BANDS, jnp.cos(ang), jnp.where(lane <= 2 * D_BANDS, -jnp.sin(ang), 0.0)))
    a = jnp.sin(fr_ref[0:1] * (_bdot(feat.astype(BF16), w1_ref[...].astype(BF16)) + b1_ref[...]))
    a = jnp.sin(fr_ref[1:2] * (_bdot(a.astype(BF16), w2_ref[...].astype(BF16)) + b2_ref[...]))
    f = _bdot(a.astype(BF16), w3_ref[...].astype(BF16))
    window = jnp.exp(-t_norm * delta_ref[...]) + D_MOD_SHIFT
    return f * jnp.concatenate([window] * 4, axis=1)


def _filter_sum_kernel(band_ref, w1_ref, b1_ref, w2_ref, b2_ref, w3_ref, fr_ref, delta_ref, o_ref, *, seq, tm):
    i = pl.program_id(0)
    t = (i * tm + lax.broadcasted_iota(jnp.int32, (tm, 1), 0)).astype(F32)
    f = _filter_rows(t, seq, band_ref, w1_ref, b1_ref, w2_ref, b2_ref, w3_ref, fr_ref, delta_ref)
    part = jnp.sum(jnp.abs(f), axis=0, keepdims=True)

    @pl.when(i == 0)
    def _():
        o_ref[...] = part

    @pl.when(i > 0)
    def _():
        o_ref[...] += part


def _filter_gen_kernel(band_ref, w1_ref, b1_ref, w2_ref, b2_ref, w3_ref, fr_ref, delta_ref, sum_ref, o_ref, *, seq, tm):
    i = pl.program_id(0)
    r = i * tm + lax.broadcasted_iota(jnp.int32, (tm, 1), 0)
    t = jnp.where(r < seq, r, 2 * seq - r).astype(F32)
    f = _filter_rows(t, seq, band_ref, w1_ref, b1_ref, w2_ref, b2_ref, w3_ref, fr_ref, delta_ref)
    f = f / (sum_ref[...] + 1e-6)
    half = 2 * D_MODEL
    fwd, bwd = f[:, :half], f[:, half:]
    o_ref[...] = jnp.where(r < seq, fwd, jnp.where(r > seq, bwd, 0.0)) + jnp.where(r == 0, bwd, 0.0)


def hyena_circular_kernel(seq, band, w1p, b1, w2, b2, w3, freq, delta, *, tm):
    consts = [band, w1p, b1, w2, b2, w3, freq, delta]
    cspecs = [pl.BlockSpec(a.shape, lambda i: (0, 0)) for a in consts]
    nf = w3.shape[1]
    sums = pl.pallas_call(
        functools.partial(_filter_sum_kernel, seq=seq, tm=tm),
        grid=(seq // tm,),
        in_specs=cspecs,
        out_specs=pl.BlockSpec((1, nf), lambda i: (0, 0)),
        out_shape=jax.ShapeDtypeStruct((1, nf), F32),
        compiler_params=_cparams("arbitrary"),
        name="hyena_filter_sum",
    )(*consts)
    return pl.pallas_call(
        functools.partial(_filter_gen_kernel, seq=seq, tm=tm),
        grid=(2 * seq // tm,),
        in_specs=cspecs + [pl.BlockSpec((1, nf), lambda i: (0, 0))],
        out_specs=pl.BlockSpec((tm, nf // 2), lambda i: (i, 0)),
        out_shape=jax.ShapeDtypeStruct((2 * seq, nf // 2), F32),
        compiler_params=_cparams("parallel"),
        name="hyena_filter_gen",
    )(*consts, sums)


def _dft_cs(rows, cols, n):
    m = np.outer(np.arange(rows), np.arange(cols)) % n
    ang = 2.0 * np.pi * m / n
    return np.cos(ang), np.sin(ang)


def _stack_fwd(c, s):
    return np.block([[c, s], [-s, c]])


def _stack_inv(c, s):
    return np.block([[c, -s], [s, c]])


def _dft_mats():
    as_bf16 = lambda a: jnp.asarray(a, F32).astype(BF16)
    c1, s1 = _dft_cs(LAT_N1, LAT_N1, LAT_N1)
    c2, s2 = _dft_cs(LAT_N2, LAT_N2, LAT_N2)
    cc, sc = _dft_cs(CTX_N, CTX_N, CTX_N)
    h1, hc = LAT_N1 // 2, CTX_N // 2
    return dict(
        lat_g1=as_bf16(_kron_rows(_stack_fwd(c1[:, :h1], s1[:, :h1]))),
        lat_g1r=as_bf16(_kron_rows(np.concatenate([c1, -s1], axis=0))),
        lat_g2=as_bf16(_stack_fwd(c2, s2)),
        lat_g2i=as_bf16(_stack_inv(c2, s2)),
        lat_g1i=as_bf16(_kron_rows(_stack_inv(c1[:h1], s1[:h1]) / LAT_N)),
        ctx_g=as_bf16(_stack_fwd(cc[:, :hc], sc[:, :hc])),
        ctx_gr=as_bf16(np.concatenate([cc, -sc], axis=0)),
        ctx_gi=as_bf16(_stack_inv(cc[:hc], sc[:hc]) / CTX_N),
    )


def _lmul_kernel(g_ref, x_ref, o_ref):
    o_ref[...] = _bdot(g_ref[...], x_ref[...].astype(BF16))


def left_matmul(g, x, *, row_blk=0, tc):
    m, k = g.shape
    n = x.shape[1]
    return pl.pallas_call(
        _lmul_kernel,
        grid=(n // tc,),
        in_specs=[pl.BlockSpec((m, k), lambda j: (0, 0)), pl.BlockSpec((k, tc), lambda j: (row_blk, j))],
        out_specs=pl.BlockSpec((m, tc), lambda j: (0, j)),
        out_shape=jax.ShapeDtypeStruct((m, n), F32),
        compiler_params=_cparams("parallel"),
        name="left_matmul",
    )(g, x)


SLAB_ROWS = 8


def _kron_rows(g):
    return np.kron(g, np.eye(SLAB_ROWS))


def _slab_dot(g_ref, x_ref):
    k, r, w = x_ref.shape
    return _bdot(g_ref[...], x_ref[...].reshape(k * r, w).astype(BF16))


def _lmul_slab_kernel(g_ref, x_ref, o_ref):
    o_ref[...] = _slab_dot(g_ref, x_ref).reshape(o_ref.shape)


def left_matmul_slabs(g8, x3, *, row_blk=0):
    m, k = g8.shape[0] // SLAB_ROWS, g8.shape[1] // SLAB_ROWS
    _, s, w = x3.shape
    return pl.pallas_call(
        _lmul_slab_kernel,
        grid=(s // SLAB_ROWS,),
        in_specs=[pl.BlockSpec(g8.shape, lambda j: (0, 0)),
                  pl.BlockSpec((k, SLAB_ROWS, w), lambda j: (row_blk, j, 0))],
        out_specs=pl.BlockSpec((m, SLAB_ROWS, w), lambda j: (0, j, 0)),
        out_shape=jax.ShapeDtypeStruct((m, s, w), F32),
        compiler_params=_cparams("parallel"),
        name="left_matmul_slabs",
    )(g8, x3)


def _gated_skip(conv, gate, y, skip_ref):
    return gate * (conv + skip_ref[...] * y)


def _lat_last_kernel(g_ref, b_ref, gate_ref, y_ref, skip_ref, o_ref):
    conv = _slab_dot(g_ref, b_ref).reshape(o_ref.shape)
    o_ref[...] = _gated_skip(conv, gate_ref[...], y_ref[...], skip_ref).astype(o_ref.dtype)


def lat_last_stage(g, b3, gate3, gate_blk, y3, y_blk, skip, *, out_dtype):
    m, k = g.shape[0] // SLAB_ROWS, g.shape[1] // SLAB_ROWS
    _, s, w = b3.shape
    return pl.pallas_call(
        _lat_last_kernel,
        grid=(s // SLAB_ROWS,),
        in_specs=[
            pl.BlockSpec(g.shape, lambda j: (0, 0)),
            pl.BlockSpec((k, SLAB_ROWS, w), lambda j: (0, j, 0)),
            pl.BlockSpec((m, SLAB_ROWS, w), lambda j: (gate_blk, j, 0)),
            pl.BlockSpec((m, SLAB_ROWS, w), lambda j: (y_blk, j, 0)),
            pl.BlockSpec((1, w), lambda j: (0, 0)),
        ],
        out_specs=pl.BlockSpec((m, SLAB_ROWS, w), lambda j: (0, j, 0)),
        out_shape=jax.ShapeDtypeStruct((m, s, w), out_dtype),
        compiler_params=_cparams("parallel"),
        name="hyena_lat_last",
    )(g, b3, gate3, y3, skip)


def _lat_mid_kernel(a_ref, *rest, conv):
    if conv:
        h_ref, g2_ref, g2i_ref, o_ref = rest
    else:
        g2_ref, o_ref = rest
    k1 = pl.program_id(0)
    n2 = lax.broadcasted_iota(jnp.int32, (LAT_N2, 1), 0)
    ang = (k1 * n2).astype(F32) * (2.0 * math.pi / LAT_N)
    c, s = jnp.cos(ang), jnp.sin(ang)
    ar, ai = a_ref[0], a_ref[1]
    t = jnp.concatenate([ar * c + ai * s, ai * c - ar * s], axis=0).astype(BF16)
    x = _bdot(g2_ref[...], t)
    xr, xi = x[:LAT_N2], x[LAT_N2:]
    if not conv:
        o_ref[0] = xr
        o_ref[1] = xi
        return
    hr, hi = h_ref[0], h_ref[1]
    y = jnp.concatenate([xr * hr - xi * hi, xr * hi + xi * hr], axis=0).astype(BF16)
    b = _bdot(g2i_ref[...], y)
    br, bi = b[:LAT_N2], b[LAT_N2:]
    o_ref[0] = br * c - bi * s
    o_ref[1] = bi * c + br * s


def lat_mid_stage(a, mats, h=None, order=0):
    w = a.shape[-1]
    blk = lambda width, col: pl.BlockSpec((2, None, LAT_N2, width), lambda k1, j: (0, k1, 0, col(j)))
    gspec = pl.BlockSpec((2 * LAT_N2, 2 * LAT_N2), lambda k1, j: (0, 0))
    if h is None:
        in_specs = [blk(D_MODEL, lambda j: j), gspec]
        args = [a, mats["lat_g2"]]
    else:
        in_specs = [blk(D_MODEL, lambda j: j), blk(D_MODEL, lambda j: order), gspec, gspec]
        args = [a, h, mats["lat_g2"], mats["lat_g2i"]]
    return pl.pallas_call(
        functools.partial(_lat_mid_kernel, conv=h is not None),
        grid=(LAT_N1, w // D_MODEL),
        in_specs=in_specs,
        out_specs=blk(D_MODEL, lambda j: j),
        out_shape=jax.ShapeDtypeStruct(a.shape, F32),
        compiler_params=_cparams("parallel", "arbitrary"),
        name="hyena_lat_mid",
    )(*args)


def _ctx_conv_kernel(y_ref, gate_ref, h_ref, g_ref, gi_ref, skip_ref, o_ref):
    n = CTX_N
    y = y_ref[...]
    z = _bdot(g_ref[...], y.astype(BF16))
    zr, zi = z[:n], z[n:]
    hr, hi = h_ref[0:n], h_ref[n:]
    w = jnp.concatenate([zr * hr - zi * hi, zr * hi + zi * hr], axis=0).astype(BF16)
    conv = _bdot(gi_ref[...], w)
    o_ref[...] = _gated_skip(conv, gate_ref[...], y, skip_ref).astype(o_ref.dtype)


def ctx_conv(y, y_plane, gate, gate_plane, h, order, mats, skip, *, out_dtype):
    rows = 2 * SEQ
    return pl.pallas_call(
        _ctx_conv_kernel,
        grid=(BATCH // 2,),
        in_specs=[
            pl.BlockSpec((None, rows, D_MODEL), lambda p: (y_plane, p, 0)),
            pl.BlockSpec((None, rows, D_MODEL), lambda p: (gate_plane, p, 0)),
            pl.BlockSpec((2 * CTX_N, D_MODEL), lambda p: (0, order)),
            pl.BlockSpec((2 * CTX_N, rows), lambda p: (0, 0)),
            pl.BlockSpec((rows, 2 * CTX_N), lambda p: (0, 0)),
            pl.BlockSpec((1, D_MODEL), lambda p: (0, 0)),
        ],
        out_specs=pl.BlockSpec((rows, D_MODEL), lambda p: (p, 0)),
        out_shape=jax.ShapeDtypeStruct((T_CTX, D_MODEL), out_dtype),
        compiler_params=_cparams("parallel"),
        name="hyena_ctx_conv",
    )(y, gate, h, mats["ctx_g"], mats["ctx_gi"], skip)


def hyena_core(z3, f_w1, f_b1, f_w2, f_b2, f_w3, f_freq, f_skip):
    mats = _dft_mats()
    bands = jnp.linspace(1e-4, D_BANDS - 1, D_BANDS, dtype=F32)
    band = jnp.zeros((1, 128), F32).at[0, 1:1 + D_BANDS].set(bands).at[0, 1 + D_BANDS:1 + 2 * D_BANDS].set(bands)
    w1p = jnp.zeros((128, D_FILTER_HIDDEN), F32).at[:D_EMB].set(f_w1)
    max_decay = math.log(D_DECAY_TARGET) / D_FAST_DECAY_PCT
    min_decay = math.log(D_DECAY_TARGET) / D_SLOW_DECAY_PCT
    delta = jnp.abs(jnp.linspace(min_decay, max_decay, D_MODEL, dtype=F32)).reshape(1, D_MODEL)
    fargs = (band, w1p, f_b1.reshape(1, -1), f_w2, f_b2.reshape(1, -1), f_w3, f_freq, delta)

    kc_ctx = hyena_circular_kernel(SEQ, *fargs, tm=SEQ)
    h_ctx = left_matmul(mats["ctx_gr"], kc_ctx, tc=D_MODEL)
    kc_lat = hyena_circular_kernel(DEC_SEQ, *fargs, tm=512)
    a = left_matmul_slabs(mats["lat_g1r"], kc_lat.reshape(LAT_N1, LAT_N2, 2 * D_MODEL))
    h_lat = lat_mid_stage(a.reshape(2, LAT_N1, LAT_N2, 2 * D_MODEL), mats)

    y1 = ctx_conv(z3, 2, z3, 0, h_ctx, 0, mats, f_skip[0:1], out_dtype=F32)
    y_ctx = ctx_conv(y1[None], 0, z3, 1, h_ctx, 1, mats, f_skip[1:2], out_dtype=BF16)

    slabs = DEC_BATCH * LAT_N1 // 2
    z_slabs = z3.reshape(3 * T_ALL // LAT_N2, LAT_N2, D_MODEL)
    plane_blks = T_ALL // LAT_N2 // slabs
    lat_blk = lambda plane: plane * plane_blks + T_CTX // LAT_N2 // slabs
    y, y_blk = z_slabs, lat_blk(2)
    for n in range(2):
        a = left_matmul_slabs(mats["lat_g1"], y, row_blk=y_blk)
        b = lat_mid_stage(a.reshape(2, LAT_N1, LAT_N2, D_MODEL), mats, h_lat, n)
        y = lat_last_stage(mats["lat_g1i"], b.reshape(2 * LAT_N1, LAT_N2, D_MODEL), z_slabs, lat_blk(n), y, y_blk,
                           f_skip[n:n + 1], out_dtype=F32)
        y_blk = 0
    return y_ctx, y.reshape(T_LAT, D_MODEL)


def kernel(x_prompt, x_sample, cache_attn_k, cache_attn_v, cache_diff_k, cache_diff_v, state_ret, c, c_ctx, w_mod, b_mod, norm1_g, norm2_g, final_g, attn_w_qkv, attn_q_g, attn_k_g, attn_w_o, diff_w_qkv, diff_lambda, diff_subln_g, diff_w_o, ret_w_in, ret_log_decay, ret_gn_g, ret_w_o, hyena_w_in, hyena_sc_w, hyena_sc_b, hyena_f_w1, hyena_f_b1, hyena_f_w2, hyena_f_b2, hyena_f_w3, hyena_f_freq, hyena_f_skip, hyena_w_o, ffn_w_up, ffn_conv_w, ffn_conv_b, ffn_w_down):
    d = D_MODEL
    xs = (x_prompt.reshape(T_CTX, d), x_sample.reshape(T_LAT, d))
    cond =jnp.zeros((SEG_ROWS, d), F32).at[0].set(c_ctx).at[1:N_SEG].set(c)
    mod = modulation(cond, w_mod, b_mod)
    mod = mod.reshape(DEPTH, SEG_ROWS, 6, d).transpose(0, 2, 1, 3).reshape(DEPTH, 6, SEG_ROWS, 1, d)
    rope = rope_tables()
    bf = lambda w: w.astype(BF16)
    w_up, w_down = bf(ffn_w_up), bf(ffn_w_down)
    out = {}

    for l in range(DEPTH):
        m, j = l % 4, l // 4
        sh1, sc1, g1, sh2, sc2, g2 = (mod[l, i] for i in range(6))
        if m == 0:
            nq, nk = A_HEADS * A_HEAD_DIM, A_KV_HEADS * A_HEAD_DIM
            scale = A_HEAD_DIM ** -0.5 * LOG2E
            qkv = norm_proj(*xs, norm1_g[l], sh1, sc1, bf(attn_w_qkv[j]))
            q_c, kv_c, kt_c, vt_c = qk_prep(qkv, row0=0, rows=T_CTX, nq=nq, nk=nk, q_g=attn_q_g[j], k_g=attn_k_g[j],
                                            scale=scale, cache="seq_minor")
            q_l, kv_l = qk_prep(qkv, row0=T_CTX, rows=T_LAT, nq=nq, nk=nk, q_g=attn_q_g[j], k_g=attn_k_g[j],
                                rope=rope, scale=scale)
            as_cache = lambda t: t.reshape(BATCH, A_KV_HEADS, A_HEAD_DIM, SEQ).transpose(0, 3, 1, 2)[:, None]
            out["attn_k"] = as_cache(kt_c)
            out["attn_v"] = as_cache(vt_c)
            cache = jnp.concatenate([cache_attn_k[:, j].reshape(DEC_BATCH, PAST_LEN, nk),
                                     cache_attn_v[:, j].reshape(DEC_BATCH, PAST_LEN, nk)], axis=-1).astype(BF16)
            lk = PAST_LEN + DEC_SEQ
            kv_all = jnp.concatenate([cache, kv_l.reshape(DEC_BATCH, DEC_SEQ, 2 * nk)], axis=1).reshape(DEC_BATCH * lk, 2 * nk)
            o_c = gqa_attention(q_c, kv_c, nb=BATCH, lq=SEQ, lk=SEQ, tq=SEQ)
            o_l = gqa_attention(q_l, kv_all, nb=DEC_BATCH, lq=DEC_SEQ, lk=lk, tq=256)
            x = out_proj_residual(o_c, o_l, bf(attn_w_o[j]), *xs, g1)
        elif m == 1:
            nq = nk = B_HEADS * 2 * B_HEAD_DIM
            scale = B_HEAD_DIM ** -0.5 * LOG2E
            lam_init = 0.8 - 0.6 * math.exp(-0.3 * l)
            qkv = norm_proj(*xs, norm1_g[l], sh1, sc1, bf(diff_w_qkv[j]))
            q_c, kv_c, k3_c, v3_c = qk_prep(qkv, row0=0, rows=T_CTX, nq=nq, nk=nk, scale=scale, cache="head_tile")
            q_l, kv_l = qk_prep(qkv, row0=T_CTX, rows=T_LAT, nq=nq, nk=nk, rope=rope, scale=scale)
            out["diff_k"] = k3_c.reshape(BATCH, 1, SEQ, B_HEADS, 2 * B_HEAD_DIM)
            out["diff_v"] = v3_c.reshape(BATCH, 1, SEQ, B_HEADS, 2 * B_HEAD_DIM)
            cache = jnp.concatenate([cache_diff_k[:, j].reshape(DEC_BATCH, PAST_LEN, nk),
                                     cache_diff_v[:, j].reshape(DEC_BATCH, PAST_LEN, nk)], axis=-1).astype(BF16)
            lk = PAST_LEN + DEC_SEQ
            kv_all = jnp.concatenate([cache, kv_l.reshape(DEC_BATCH, DEC_SEQ, 2 * nk)], axis=1).reshape(DEC_BATCH * lk, 2 * nk)
            dargs = (diff_lambda[j], diff_subln_g[j])
            o_c = diff_attention(q_c, kv_c, *dargs, nb=BATCH, lq=SEQ, lk=SEQ, tq=SEQ, lam_init=lam_init)
            o_l = diff_attention(q_l, kv_all, *dargs, nb=DEC_BATCH, lq=DEC_SEQ, lk=lk, tq=256, lam_init=lam_init)
            x = out_proj_residual(o_c, o_l, bf(diff_w_o[j]), *xs, g1)
        elif m == 2:
            proj = norm_proj(*xs, norm1_g[l], sh1, sc1, bf(ret_w_in[j]), out_dtype=BF16)
            o_c, st = retention(proj, ret_log_decay[j], None, row0=0, nb=BATCH, seq=SEQ)
            o_l, _ = retention(proj, ret_log_decay[j], state_ret[:, j], row0=T_CTX, nb=DEC_BATCH, seq=DEC_SEQ)
            out["ret_s"] = st.reshape(BATCH, 1, 2, C_HEADS, C_KEY_DIM, C_VAL_DIM)
            x = retention_out(o_c, o_l, proj, ret_gn_g[j], bf(ret_w_o[j]), x, g1)
        else:
            z3 = norm_proj_conv(x, norm1_g[l], sh1, sc1, bf(hyena_w_in[j]), hyena_sc_w[j], hyena_sc_b[j])
            y_c, y_l = hyena_core(z3, hyena_f_w1[j], hyena_f_b1[j], hyena_f_w2[j], hyena_f_b2[j], hyena_f_w3[j],
                                  hyena_f_freq[j], hyena_f_skip[j])
            x = out_proj_residual(y_c, y_l, bf(hyena_w_o[j]), *xs, g1)
        x = conv_ffn(x, norm2_g[l], sh2, sc2, g2, w_up, ffn_conv_w[l], ffn_conv_b[l], w_down, l,
                     final_g if l == DEPTH - 1 else None)
        xs = (x, x)

    y_prompt = x[0].reshape(BATCH, SEQ, d)
    y_sample = x[1].reshape(DEC_BATCH, DEC_SEQ, d)
    return (y_prompt, y_sample, out["attn_k"], out["attn_v"], out["diff_k"], out["diff_v"], out["ret_s"])
```

```python
import functools
import math

import jax
import jax.numpy as jnp
import numpy as np
from jax import lax
from jax.experimental import pallas as pl
from jax.experimental.pallas import tpu as pltpu

F32 = jnp.float32
BF16 = jnp.bfloat16

D_MODEL = 1024
BATCH = 32
SEQ = 256
DEPTH = 4
DEC_BATCH = 2
DEC_SEQ = 4096
PAST_LEN = 256
GRID_W = 64
ROPE_THETA = 10000.0
NORM_EPS = 1e-6
A_HEADS = 16
A_KV_HEADS = 4
A_HEAD_DIM = 64
A_GROUP = A_HEADS // A_KV_HEADS
B_HEADS = 8
B_HEAD_DIM = 64
B_SUBLN_EPS = 1e-5
C_HEADS = 4
C_KEY_DIM = 256
C_VAL_DIM = 512
D_BANDS = 16
D_EMB = 1 + 2 * D_BANDS
D_FILTER_HIDDEN = 64
D_FAST_DECAY_PCT = 0.3
D_SLOW_DECAY_PCT = 1.5
D_DECAY_TARGET = 1e-2
D_MOD_SHIFT = 0.05
FFN_DIM = 2816

T_CTX = BATCH * SEQ
T_LAT = DEC_BATCH * DEC_SEQ
T_ALL = T_CTX + T_LAT
N_SEG = 1 + DEC_BATCH
SEG_ROWS = 8

HALO = 16
RET_CHUNK = 256
VMEM_LIMIT = 56 * 1024 * 1024

LAT_N = 2 * DEC_SEQ
LAT_N1 = 64
LAT_N2 = LAT_N // LAT_N1
CTX_N = 2 * SEQ


def _cparams(*sem):
    return pltpu.CompilerParams(dimension_semantics=sem, vmem_limit_bytes=VMEM_LIMIT)


def _seg_of_tile(i, tm):
    start = i * tm
    return jnp.where(start < T_CTX, 0, 1 + (start - T_CTX) // DEC_SEQ)


def _silu(x):
    return x * jax.nn.sigmoid(x)


def _bdot(a, b):
    return jnp.dot(a, b, preferred_element_type=F32)


def _norm_mod(x, g, sh, sc):
    ms = jnp.mean(x * x, axis=-1, keepdims=True)
    y = (x * lax.rsqrt(ms + NORM_EPS)) * g
    return y * (1.0 + sc) + sh


def _mod_kernel(c_ref, w_ref, b_ref, o_ref):
    s = _silu(c_ref[...]).astype(BF16)
    o_ref[...] = _bdot(s, w_ref[...].astype(BF16)) + b_ref[...]


def modulation(cond, w_mod, b_mod):
    tn = 1536
    n = w_mod.shape[-1]
    return pl.pallas_call(
        _mod_kernel,
        grid=(DEPTH, n // tn),
        in_specs=[
            pl.BlockSpec((SEG_ROWS, D_MODEL), lambda l, j: (0, 0)),
            pl.BlockSpec((None, D_MODEL, tn), lambda l, j: (l, 0, j)),
            pl.BlockSpec((None, 1, tn), lambda l, j: (l, 0, j)),
        ],
        out_specs=pl.BlockSpec((None, SEG_ROWS, tn), lambda l, j: (l, 0, j)),
        out_shape=jax.ShapeDtypeStruct((DEPTH, SEG_ROWS, n), F32),
        compiler_params=_cparams("arbitrary", "arbitrary"),
        name="modulation",
    )(cond, w_mod, b_mod.reshape(DEPTH, 1, n))


def _proj_kernel(xc_ref, xl_ref, g_ref, sh_ref, sc_ref, w_ref, o_ref, *, tm, tn):
    x = _pick_group(xc_ref, xl_ref, tm)
    h = _norm_mod(x, g_ref[...], sh_ref[...], sc_ref[...]).astype(BF16)
    for c in range(0, w_ref.shape[1], tn):
        o_ref[:, c:c + tn] = _bdot(h, w_ref[:, c:c + tn]).astype(o_ref.dtype)


def norm_proj(x_ctx, x_lat, g, sh, sc, w, *, tm=512, tn=1536, out_dtype=F32):
    d, n = w.shape
    seg = lambda i: (_seg_of_tile(i, tm), 0, 0)
    return pl.pallas_call(
        functools.partial(_proj_kernel, tm=tm, tn=tn),
        grid=(T_ALL // tm,),
        in_specs=_group_specs(x_ctx, x_lat, tm, d) + [
            pl.BlockSpec((1, d), lambda i: (0, 0)),
            pl.BlockSpec((None, 1, d), seg),
            pl.BlockSpec((None, 1, d), seg),
            pl.BlockSpec((d, n), lambda i: (0, 0), pipeline_mode=pl.Buffered(1)),
        ],
        out_specs=pl.BlockSpec((tm, n), lambda i: (i, 0)),
        out_shape=jax.ShapeDtypeStruct((T_ALL, n), out_dtype),
        compiler_params=_cparams("parallel"),
        name="norm_proj",
    )(x_ctx, x_lat, g.reshape(1, d), sh, sc, w)


def _conv3(u, cw, cb, i, tm):
    rows = u.shape[0]
    up = pltpu.roll(u, 1, 0)[HALO:HALO + tm]
    uc = u[HALO:HALO + tm]
    un = pltpu.roll(u, rows - 1, 0)[HALO:HALO + tm]
    sub = lax.broadcasted_iota(jnp.int32, (8, 1), 0)
    is_ctx = i * tm < T_CTX
    ups, uns = [], []
    for r in range(0, tm, SEQ):
        start = i * tm + r
        first = jnp.logical_or(is_ctx, (start & (DEC_SEQ - 1)) == 0)
        last = jnp.logical_or(is_ctx, ((start + SEQ) & (DEC_SEQ - 1)) == 0)
        ups += [jnp.where(jnp.logical_and(sub == 0, first), 0.0, up[r:r + 8]), up[r + 8:r + SEQ]]
        uns += [un[r:r + SEQ - 8], jnp.where(jnp.logical_and(sub == 7, last), 0.0, un[r + SEQ - 8:r + SEQ])]
    up = jnp.concatenate(ups, axis=0)
    un = jnp.concatenate(uns, axis=0)
    return up * cw[0:1] + uc * cw[1:2] + un * cw[2:3] + cb


def _halo_h(xp_ref, x, xn_ref, g_ref, sh_ref, sc_ref):
    g, sh, sc = g_ref[...], sh_ref[...], sc_ref[...]
    return jnp.concatenate([_norm_mod(xp_ref[...], g, sh, sc).astype(BF16), _norm_mod(x, g, sh, sc).astype(BF16),
                            _norm_mod(xn_ref[...], g, sh, sc).astype(BF16)], axis=0)


def _proj_conv_kernel(xp_ref, x_ref, xn_ref, g_ref, sh_ref, sc_ref, w_ref, cw_ref, cb_ref, o_ref, *, tm, tn):
    i = pl.program_id(0)
    h = _halo_h(xp_ref, x_ref[...], xn_ref, g_ref, sh_ref, sc_ref)
    d = o_ref.shape[-1]
    for c in range(0, w_ref.shape[1], tn):
        u = _conv3(_bdot(h, w_ref[:, c:c + tn]), cw_ref[:, c:c + tn], cb_ref[:, c:c + tn], i, tm)
        o_ref[c // d, :, c % d:c % d + tn] = u


def _ffn_kernel(xp_ref, x_ref, xn_ref, g_ref, sh_ref, sc_ref, gate_ref, wu_ref, cw_ref, cb_ref, wd_ref, *rest, tm, tn,
                final):
    if final:
        fg_ref, oc_ref, ol_ref = rest
    else:
        (o_ref,) = rest
    i = pl.program_id(0)
    x = x_ref[...]
    h = _halo_h(xp_ref, x, xn_ref, g_ref, sh_ref, sc_ref)
    f = wd_ref.shape[0]
    acts = []
    for c in range(0, f, tn):
        a = _conv3(_bdot(h, wu_ref[:, c:c + tn]), cw_ref[:, c:c + tn], cb_ref[:, c:c + tn], i, tm)
        b = _conv3(_bdot(h, wu_ref[:, f + c:f + c + tn]), cw_ref[:, f + c:f + c + tn], cb_ref[:, f + c:f + c + tn], i, tm)
        acts.append((_silu(a) * b).astype(BF16))
    y = x + gate_ref[...] * _bdot(jnp.concatenate(acts, axis=1), wd_ref[...])
    if not final:
        o_ref[...] = y
        return
    ms = jnp.mean(y * y, axis=-1, keepdims=True)
    y = (y * lax.rsqrt(ms + NORM_EPS)) * fg_ref[...]
    is_ctx = i < T_CTX // tm

    @pl.when(is_ctx)
    def _():
        oc_ref[...] = y

    @pl.when(jnp.logical_not(is_ctx))
    def _():
        ol_ref[...] = y


def _halo_specs(t, tm, d):
    per = tm // HALO
    last_blk = t // HALO - 1
    return [
        pl.BlockSpec((HALO, d), lambda i, *_: (jnp.maximum(i * per - 1, 0), 0)),
        pl.BlockSpec((tm, d), lambda i, *_: (i, 0)),
        pl.BlockSpec((HALO, d), lambda i, *_: (jnp.minimum((i + 1) * per, last_blk), 0)),
    ]


def norm_proj_conv(x, g, sh, sc, w, cw, cb, *, tm=512, tn=512):
    t, d = x.shape
    n = w.shape[1]
    seg = lambda i: (_seg_of_tile(i, tm), 0, 0)
    whole = lambda a: pl.BlockSpec(a.shape, lambda i: (0, 0), pipeline_mode=pl.Buffered(1))
    cb = cb.reshape(1, n)
    return pl.pallas_call(
        functools.partial(_proj_conv_kernel, tm=tm, tn=tn),
        grid=(t // tm,),
        in_specs=_halo_specs(t, tm, d) + [
            pl.BlockSpec((1, d), lambda i: (0, 0)),
            pl.BlockSpec((None, 1, d), seg),
            pl.BlockSpec((None, 1, d), seg),
            whole(w), whole(cw), whole(cb),
        ],
        out_specs=pl.BlockSpec((n // d, tm, d), lambda i: (0, i, 0)),
        out_shape=jax.ShapeDtypeStruct((n // d, t, d), F32),
        compiler_params=_cparams("parallel"),
        name="norm_proj_conv",
    )(x, x, x, g.reshape(1, d), sh, sc, w, cw, cb)


def conv_ffn(x, g, sh, sc, gate, w_up, cw, cb, w_down, layer, final_g=None, *, tm=1024, tn=256):
    t, d = x.shape
    f = w_down.shape[1]
    seg = lambda i: (_seg_of_tile(i, tm), 0, 0)
    whole = lambda a: pl.BlockSpec(a.shape, lambda i: (0, 0), pipeline_mode=pl.Buffered(1))
    of_layer = lambda a: pl.BlockSpec((None,) + a.shape[1:], lambda i: (layer, 0, 0), pipeline_mode=pl.Buffered(1))
    cb = cb.reshape(1, 2 * f)
    in_specs = _halo_specs(t, tm, d) + [
        pl.BlockSpec((1, d), lambda i: (0, 0)),
        pl.BlockSpec((None, 1, d), seg),
        pl.BlockSpec((None, 1, d), seg),
        pl.BlockSpec((None, 1, d), seg),
        of_layer(w_up), whole(cw), whole(cb), of_layer(w_down),
    ]
    args = [x, x, x, g.reshape(1, d), sh, sc, gate, w_up, cw, cb, w_down]
    out_specs = pl.BlockSpec((tm, d), lambda i: (i, 0))
    out_shape = jax.ShapeDtypeStruct((t, d), F32)
    if final_g is not None:
        nctx = T_CTX // tm
        in_specs.append(pl.BlockSpec((1, d), lambda i: (0, 0)))
        args.append(final_g.reshape(1, d))
        out_specs = [pl.BlockSpec((tm, d), lambda i: (jnp.minimum(i, nctx - 1), 0)),
                     pl.BlockSpec((tm, d), lambda i: (jnp.maximum(i - nctx, 0), 0))]
        out_shape = [jax.ShapeDtypeStruct((T_CTX, d), F32), jax.ShapeDtypeStruct((T_LAT, d), F32)]
    return pl.pallas_call(
        functools.partial(_ffn_kernel, tm=tm, tn=tn, final=final_g is not None),
        grid=(t // tm,),
        in_specs=in_specs,
        out_specs=out_specs,
        out_shape=out_shape,
        compiler_params=_cparams("arbitrary"),
        name="conv_ffn",
    )(*args)


def _group_specs(a_ctx, a_lat, tm, width):
    nctx = T_CTX // tm
    off = nctx if a_lat.shape[0] == T_ALL else 0
    return [
        pl.BlockSpec((tm, width), lambda i, *_: (jnp.minimum(i, nctx - 1), 0)),
        pl.BlockSpec((tm, width), lambda i, *_: (off + jnp.maximum(i - nctx, 0), 0)),
    ]


def _pick_group(ac_ref, al_ref, tm):
    return jnp.where(pl.program_id(0) < T_CTX // tm, ac_ref[...], al_ref[...])


def _out_proj_kernel(ac_ref, al_ref, w_ref, xc_ref, xl_ref, gate_ref, o_ref, *, tm):
    a = _pick_group(ac_ref, al_ref, tm).astype(BF16)
    o_ref[...] = _pick_group(xc_ref, xl_ref, tm) + gate_ref[...] * _bdot(a, w_ref[...])


def out_proj_residual(a_ctx, a_lat, w, x_ctx, x_lat, gate, *, tm=512):
    k, d = w.shape
    seg = lambda i: (_seg_of_tile(i, tm), 0, 0)
    return pl.pallas_call(
        functools.partial(_out_proj_kernel, tm=tm),
        grid=(T_ALL // tm,),
        in_specs=_group_specs(a_ctx, a_lat, tm, k) + [pl.BlockSpec((k, d), lambda i: (0, 0))]
        + _group_specs(x_ctx, x_lat, tm, d) + [pl.BlockSpec((None, 1, d), seg)],
        out_specs=pl.BlockSpec((tm, d), lambda i: (i, 0)),
        out_shape=jax.ShapeDtypeStruct((T_ALL, d), F32),
        compiler_params=_cparams("parallel"),
        name="out_proj_residual",
    )(a_ctx, a_lat, w, x_ctx, x_lat, gate)


def _head_mean_sq(x, head_dim):
    n = x.shape[1]
    x2 = x * x
    hi = x2.astype(BF16)
    lo = (x2 - hi.astype(F32)).astype(BF16)
    blk = 256
    r = lax.broadcasted_iota(jnp.int32, (blk, blk), 0) // head_dim
    c = lax.broadcasted_iota(jnp.int32, (blk, blk), 1) // head_dim
    ones = (r == c).astype(BF16)
    parts = []
    for s in range(0, n, blk):
        parts.append(_bdot(hi[:, s:s + blk], ones) + _bdot(lo[:, s:s + blk], ones))
    ss = parts[0] if len(parts) == 1 else jnp.concatenate(parts, axis=1)
    return ss * (1.0 / head_dim)


def _rope(x, cos, sin):
    n = x.shape[1]
    lane = lax.broadcasted_iota(jnp.int32, (1, 128), 1)
    lower = (lane & 31) < 16
    outs = []
    for s in range(0, n, 128):
        xs = x[:, s:s + 128]
        partner = jnp.where(lower, pltpu.roll(xs, 128 - 16, 1), pltpu.roll(xs, 16, 1))
        outs.append(xs * cos + partner * sin)
    return jnp.concatenate(outs, axis=1)


def _store_cache(o_ref, x, cache):
    if cache == "seq_minor":
        for b in range(o_ref.shape[0]):
            o_ref[b] = x[b * SEQ:(b + 1) * SEQ].T
    else:
        for h in range(o_ref.shape[1]):
            o_ref[:, h, :] = x[:, h * 128:(h + 1) * 128]


def _qk_prep_kernel(*refs, nq, nk, norm, rope, scale, cache):
    it = iter(refs)
    q_ref, k_ref, v_ref = next(it), next(it), next(it)
    if norm:
        qg_ref, kg_ref = next(it), next(it)
    if rope:
        cos_ref, sin_ref = next(it), next(it)
    qo_ref, kvo_ref = next(it), next(it)
    q = q_ref[...]
    k = k_ref[...]
    v = v_ref[...]
    if norm:
        q = (q * lax.rsqrt(_head_mean_sq(q, A_HEAD_DIM) + NORM_EPS)) * qg_ref[...]
        k = (k * lax.rsqrt(_head_mean_sq(k, A_HEAD_DIM) + NORM_EPS)) * kg_ref[...]
    if rope:
        cos, sin = cos_ref[...], sin_ref[...]
        q = _rope(q, cos, sin)
        k = _rope(k, cos, sin)
    qo_ref[...] = (q * scale).astype(BF16)
    kvo_ref[:, 0:nk] = k.astype(BF16)
    kvo_ref[:, nk:] = v.astype(BF16)
    if cache is not None:
        _store_cache(next(it), k, cache)
        _store_cache(next(it), v, cache)


def qk_prep(qkv, *, row0, rows, nq, nk, q_g=None, k_g=None, rope=None, scale, cache=None, tm=512):
    norm = q_g is not None
    r0 = row0 // tm
    qb = nq // nk
    in_specs = [
        pl.BlockSpec((tm, nq), lambda i: (i + r0, 0)),
        pl.BlockSpec((tm, nk), lambda i: (i + r0, qb)),
        pl.BlockSpec((tm, nk), lambda i: (i + r0, qb + 1)),
    ]
    args = [qkv, qkv, qkv]
    if norm:
        in_specs += [pl.BlockSpec((1, nq), lambda i: (0, 0)), pl.BlockSpec((1, nk), lambda i: (0, 0))]
        args += [jnp.tile(q_g, nq // q_g.shape[0]).reshape(1, nq), jnp.tile(k_g, nk // k_g.shape[0]).reshape(1, nk)]
    if rope is not None:
        per = DEC_SEQ // tm
        in_specs += [pl.BlockSpec((tm, 128), lambda i: (i % per, 0))] * 2
        args += list(rope)
    out_specs = [pl.BlockSpec((tm, nq), lambda i: (i, 0)), pl.BlockSpec((tm, 2 * nk), lambda i: (i, 0))]
    out_shape = [jax.ShapeDtypeStruct((rows, nq), BF16), jax.ShapeDtypeStruct((rows, 2 * nk), BF16)]
    if cache == "seq_minor":
        out_specs += [pl.BlockSpec((tm // SEQ, nk, SEQ), lambda i: (i, 0, 0))] * 2
        out_shape += [jax.ShapeDtypeStruct((rows // SEQ, nk, SEQ), F32)] * 2
    elif cache == "head_tile":
        out_specs += [pl.BlockSpec((tm, nk // 128, 128), lambda i: (i, 0, 0))] * 2
        out_shape += [jax.ShapeDtypeStruct((rows, nk // 128, 128), F32)] * 2
    return pl.pallas_call(
        functools.partial(_qk_prep_kernel, nq=nq, nk=nk, norm=norm, rope=rope is not None, scale=scale, cache=cache),
        grid=(rows // tm,),
        in_specs=in_specs,
        out_specs=out_specs,
        out_shape=out_shape,
        compiler_params=_cparams("parallel"),
        name="qk_prep",
    )(*args)


def rope_tables():
    t = jnp.arange(DEC_SEQ)
    row = (t // GRID_W).astype(F32)
    col = (t % GRID_W).astype(F32)
    half = A_HEAD_DIM // 4
    inv_freq = ROPE_THETA ** (-jnp.arange(half, dtype=F32) / half)
    ar = row[:, None] * inv_freq[None, :]
    ac = col[:, None] * inv_freq[None, :]
    cos = jnp.concatenate([jnp.cos(ar), jnp.cos(ar), jnp.cos(ac), jnp.cos(ac)], axis=1)
    sin = jnp.concatenate([-jnp.sin(ar), jnp.sin(ar), -jnp.sin(ac), jnp.sin(ac)], axis=1)
    return jnp.tile(cos, (1, 2)), jnp.tile(sin, (1, 2))


LOG2E = math.log2(math.e)


def _transpose_bf16(x):
    return x.astype(F32).T.astype(BF16)


ATT_TK = 256
GQA_STREAM_LANES = 512


def _fill_vt(vt_ref, kv_ref, kw):
    @pl.when(pl.program_id(1) == 0)
    def _():
        vt_ref[...] = _transpose_bf16(kv_ref[:, kw:])


def _attend(n, lk, k_piece, qts, vt_piece):
    out = []
    prev_sts = prev_m = None
    for s in range(n + 1):
        sts, m, l, acc = [], None, None, None
        for r0 in range(0, lk, ATT_TK):
            if s < n:
                st = _bdot(k_piece(s, r0), qts[s])
                m_c = jnp.max(st, axis=0, keepdims=True)
                m = m_c if m is None else jnp.maximum(m, m_c)
                sts.append(st)
            if s > 0:
                p = jnp.exp2(prev_sts[r0 // ATT_TK] - prev_m)
                l_c = jnp.sum(p, axis=0, keepdims=True)
                pv = _bdot(vt_piece(s - 1, r0), p.astype(BF16))
                l = l_c if l is None else l + l_c
                acc = pv if acc is None else acc + pv
        if s > 0:
            out.append((acc, l))
        prev_sts, prev_m = sts, m
    return out


def _gqa_kernel(q_ref, kv_ref, o_ref, vt_ref, *, tq):
    d = A_HEAD_DIM
    kvw = A_KV_HEADS * d
    _fill_vt(vt_ref, kv_ref, kvw)
    qt = _transpose_bf16(q_ref[...])
    per = max(1, min(A_GROUP, GQA_STREAM_LANES // tq))
    heads = [(kh, kh * A_GROUP + g0) for kh in range(A_KV_HEADS) for g0 in range(0, A_GROUP, per)]
    qts = [jnp.concatenate([qt[(h0 + g) * d:(h0 + g + 1) * d] for g in range(per)], axis=1) for _, h0 in heads]

    k_piece = lambda s, r0: kv_ref[r0:r0 + ATT_TK, heads[s][0] * d:(heads[s][0] + 1) * d]
    vt_piece = lambda s, r0: vt_ref[heads[s][0] * d:(heads[s][0] + 1) * d, r0:r0 + ATT_TK]
    blocks = []
    for acc, l in _attend(len(heads), kv_ref.shape[0], k_piece, qts, vt_piece):
        ot = acc / l
        blocks += [ot[:, g * tq:(g + 1) * tq] for g in range(per)]
    o_ref[...] = jnp.concatenate(blocks, axis=0).T.astype(o_ref.dtype)


def gqa_attention(q, kv, *, nb, lq, lk, tq):
    nq = lq // tq
    kvw = kv.shape[1] // 2
    return pl.pallas_call(
        functools.partial(_gqa_kernel, tq=tq),
        grid=(nb, nq),
        in_specs=[
            pl.BlockSpec((tq, q.shape[1]), lambda b, i: (b * nq + i, 0)),
            pl.BlockSpec((lk, kv.shape[1]), lambda b, i: (b, 0), pipeline_mode=pl.Buffered(1)),
        ],
        out_specs=pl.BlockSpec((tq, q.shape[1]), lambda b, i: (b * nq + i, 0)),
        out_shape=jax.ShapeDtypeStruct(q.shape, BF16),
        scratch_shapes=[pltpu.VMEM((kvw, lk), BF16)],
        compiler_params=_cparams("parallel", "arbitrary"),
        name="gqa_attention",
    )(q, kv)


def _diff_kernel(q_ref, kv_ref, lam_ref, sg_ref, o_ref, vt_ref, *, lam_init):
    d = B_HEAD_DIM
    kw = B_HEADS * 2 * d
    _fill_vt(vt_ref, kv_ref, kw)
    lf = lam_ref[...]
    lam = (jnp.exp(jnp.sum(lf[0:1] * lf[1:2], axis=-1, keepdims=True))
           - jnp.exp(jnp.sum(lf[2:3] * lf[3:4], axis=-1, keepdims=True)) + lam_init)
    qt = _transpose_bf16(q_ref[...])

    qts = [qt[r * d:(r + 1) * d] for r in range(2 * B_HEADS)]
    k_piece = lambda r, r0: kv_ref[r0:r0 + ATT_TK, r * d:(r + 1) * d]
    vt_piece = lambda r, r0: vt_ref[(r // 2) * 2 * d:(r // 2 + 1) * 2 * d, r0:r0 + ATT_TK]
    res = _attend(2 * B_HEADS, kv_ref.shape[0], k_piece, qts, vt_piece)
    blocks = []
    for h in range(B_HEADS):
        (o1, l1), (o2, l2) = res[2 * h], res[2 * h + 1]
        ot = o1 * (1.0 / l1) - o2 * (lam / l2)
        ms = jnp.mean(ot * ot, axis=0, keepdims=True)
        blocks.append(((ot * lax.rsqrt(ms + B_SUBLN_EPS)) * sg_ref[...]) * (1.0 - lam_init))
    o_ref[...] = jnp.concatenate(blocks, axis=0).T.astype(o_ref.dtype)


def diff_attention(q, kv, lam, subln_g, *, nb, lq, lk, tq, lam_init):
    nq = lq // tq
    kw = kv.shape[1] // 2
    return pl.pallas_call(
        functools.partial(_diff_kernel, lam_init=lam_init),
        grid=(nb, nq),
        in_specs=[
            pl.BlockSpec((tq, q.shape[1]), lambda b, i: (b * nq + i, 0)),
            pl.BlockSpec((lk, kv.shape[1]), lambda b, i: (b, 0), pipeline_mode=pl.Buffered(1)),
            pl.BlockSpec(lam.shape, lambda b, i: (0, 0)),
            pl.BlockSpec((2 * B_HEAD_DIM, 1), lambda b, i: (0, 0)),
        ],
        out_specs=pl.BlockSpec((tq, q.shape[1]), lambda b, i: (b * nq + i, 0)),
        out_shape=jax.ShapeDtypeStruct(q.shape, BF16),
        scratch_shapes=[pltpu.VMEM((kw, lk), BF16)],
        compiler_params=_cparams("parallel", "arbitrary"),
        name="diff_attention",
    )(q, kv, lam, subln_g.reshape(2 * B_HEAD_DIM, 1))


def _decay(x, ld):
    return jnp.exp(-jnp.abs(x * ld))


def _ret_operands(q_ref, k_ref, v_ref, head=None):
    qk = slice(None) if head is None else slice(head * C_KEY_DIM, (head + 1) * C_KEY_DIM)
    vv = slice(None) if head is None else slice(head * C_VAL_DIM, (head + 1) * C_VAL_DIM)
    return q_ref[:, qk].astype(BF16), k_ref[:, qk].astype(F32) * (C_KEY_DIM ** -0.5), v_ref[:, vv].astype(BF16)


def _ret_intra(qb, k, vb, ld_f, ld_b):
    c_len = RET_CHUNK
    rel = (lax.broadcasted_iota(jnp.int32, (c_len, c_len), 0)
           - lax.broadcasted_iota(jnp.int32, (c_len, c_len), 1)).astype(F32)
    dmat = (jnp.where(rel >= 0, _decay(jnp.maximum(rel, 0.0), ld_f), 0.0)
            + jnp.where(rel <= 0, _decay(jnp.maximum(-rel, 0.0), ld_b), 0.0))
    a = lax.dot_general(qb, k.astype(BF16), (((1,), (1,)), ((), ())), preferred_element_type=F32)
    return _bdot((a * dmat).astype(BF16), vb)


def _ret_kv(k, vb, k_dec):
    kd = (k * k_dec).astype(BF16)
    return lax.dot_general(kd, vb, (((0,), (0,)), ((), ())), preferred_element_type=F32)


def _ret_idx():
    return lax.broadcasted_iota(jnp.int32, (RET_CHUNK, 1), 0).astype(F32)


def _ret_single_kernel(ld_ref, q_ref, k_ref, v_ref, o_ref, st_ref):
    idx = _ret_idx()
    for h in range(C_HEADS):
        ld_f, ld_b = ld_ref[0, h], ld_ref[1, h]
        qb, k, vb = _ret_operands(q_ref, k_ref, v_ref, h)
        o_ref[:, h * C_VAL_DIM:(h + 1) * C_VAL_DIM] = _ret_intra(qb, k, vb, ld_f, ld_b)
        st_ref[0, h] = _ret_kv(k, vb, _decay(RET_CHUNK - 1.0 - idx, ld_f))
        st_ref[1, h] = _ret_kv(k, vb, _decay(idx, ld_b))


def _ret_sweep_kernel(ld_ref, qf_ref, kf_ref, vf_ref, qr_ref, kr_ref, vr_ref, s0_ref, o_ref, st_ref, sf_ref, sr_ref, *, nc):
    c_len = RET_CHUNK
    h = pl.program_id(1)
    s = pl.program_id(2)
    ld_f, ld_b = ld_ref[0, h], ld_ref[1, h]
    idx = _ret_idx()
    full = jnp.full((1, 1), float(c_len), F32)

    @pl.when(s == 0)
    def _():
        sf_ref[...] = s0_ref[0]
        sr_ref[...] = s0_ref[1]
        o_ref[...] = jnp.zeros_like(o_ref)

    qb, k, vb = _ret_operands(qf_ref, kf_ref, vf_ref)
    row = pl.multiple_of(s * c_len, c_len)
    o_ref[pl.ds(row, c_len), :] += (_ret_intra(qb, k, vb, ld_f, ld_b)
                                    + _bdot(qb, sf_ref[...].astype(BF16)) * _decay(idx + 1.0, ld_f))
    sf_ref[...] = sf_ref[...] * _decay(full, ld_f) + _ret_kv(k, vb, _decay(c_len - 1.0 - idx, ld_f))

    qb, k, vb = _ret_operands(qr_ref, kr_ref, vr_ref)
    row = pl.multiple_of((nc - 1 - s) * c_len, c_len)
    o_ref[pl.ds(row, c_len), :] += _bdot(qb, sr_ref[...].astype(BF16)) * _decay(c_len - idx, ld_b)
    sr_ref[...] = sr_ref[...] * _decay(full, ld_b) + _ret_kv(k, vb, _decay(idx, ld_b))

    @pl.when(s == nc - 1)
    def _():
        st_ref[0] = sf_ref[...]
        st_ref[1] = sr_ref[...]


def retention(proj, log_decay, s0, *, row0, nb, seq):
    c_len = RET_CHUNK
    nc = seq // c_len
    r0 = row0 // c_len
    qw, vw = C_HEADS * C_KEY_DIM, C_HEADS * C_VAL_DIM
    out_shape = [jax.ShapeDtypeStruct((nb * seq, vw), F32),
                 jax.ShapeDtypeStruct((nb, 2, C_HEADS, C_KEY_DIM, C_VAL_DIM), F32)]
    smem = pl.BlockSpec(memory_space=pltpu.SMEM)
    if s0 is None:
        assert nc == 1
        return pl.pallas_call(
            _ret_single_kernel,
            grid=(nb,),
            in_specs=[smem,
                      pl.BlockSpec((c_len, qw), lambda b: (r0 + b, 0)),
                      pl.BlockSpec((c_len, qw), lambda b: (r0 + b, 1)),
                      pl.BlockSpec((c_len, vw), lambda b: (r0 + b, 2 * qw // vw))],
            out_specs=[pl.BlockSpec((seq, vw), lambda b: (b, 0)),
                       pl.BlockSpec((None, 2, C_HEADS, C_KEY_DIM, C_VAL_DIM), lambda b: (b, 0, 0, 0, 0))],
            out_shape=out_shape,
            compiler_params=_cparams("parallel"),
            name="retention_single",
        )(log_decay, proj, proj, proj)

    kblk, vblk = qw // C_KEY_DIM, 2 * qw // C_VAL_DIM
    fwd = lambda b, s: r0 + b * nc + s
    rev = lambda b, s: r0 + b * nc + nc - 1 - s
    qkv = lambda chunk: [pl.BlockSpec((c_len, C_KEY_DIM), lambda b, h, s: (chunk(b, s), h)),
                         pl.BlockSpec((c_len, C_KEY_DIM), lambda b, h, s: (chunk(b, s), kblk + h)),
                         pl.BlockSpec((c_len, C_VAL_DIM), lambda b, h, s: (chunk(b, s), vblk + h))]
    state = pl.BlockSpec((None, 2, None, C_KEY_DIM, C_VAL_DIM), lambda b, h, s: (b, 0, h, 0, 0))
    return pl.pallas_call(
        functools.partial(_ret_sweep_kernel, nc=nc),
        grid=(nb, C_HEADS, nc),
        in_specs=[smem] + qkv(fwd) + qkv(rev) + [state],
        out_specs=[pl.BlockSpec((seq, C_VAL_DIM), lambda b, h, s: (b, h)), state],
        out_shape=out_shape,
        scratch_shapes=[pltpu.VMEM((C_KEY_DIM, C_VAL_DIM), F32)] * 2,
        compiler_params=_cparams("parallel", "parallel", "arbitrary"),
        name="retention_sweep",
    )(log_decay, proj, proj, proj, proj, proj, proj, s0)


def _ret_out_kernel(oc_ref, ol_ref, g_ref, gn_ref, w_ref, x_ref, gate_ref, o_ref, *, tm):
    o = _pick_group(oc_ref, ol_ref, tm)
    parts = []
    for h in range(C_HEADS):
        oh = o[:, h * C_VAL_DIM:(h + 1) * C_VAL_DIM]
        ms = jnp.mean(oh * oh, axis=-1, keepdims=True)
        parts.append((oh * lax.rsqrt(ms + NORM_EPS)) * gn_ref[:, h * C_VAL_DIM:(h + 1) * C_VAL_DIM])
    a = (_silu(g_ref[...].astype(F32)) * jnp.concatenate(parts, axis=1)).astype(BF16)
    o_ref[...] = x_ref[...] + gate_ref[...] * _bdot(a, w_ref[...])


def retention_out(o_ctx, o_lat, proj, gn_g, w, x, gate, *, tm=512):
    t, d = x.shape
    vd = C_HEADS * C_VAL_DIM
    gblk = proj.shape[1] // vd - 1
    seg = lambda i: (_seg_of_tile(i, tm), 0, 0)
    return pl.pallas_call(
        functools.partial(_ret_out_kernel, tm=tm),
        grid=(t // tm,),
        in_specs=_group_specs(o_ctx, o_lat, tm, vd) + [
            pl.BlockSpec((tm, vd), lambda i: (i, gblk)),
            pl.BlockSpec((1, vd), lambda i: (0, 0)),
            pl.BlockSpec((vd, d), lambda i: (0, 0)),
            pl.BlockSpec((tm, d), lambda i: (i, 0)),
            pl.BlockSpec((None, 1, d), seg),
        ],
        out_specs=pl.BlockSpec((tm, d), lambda i: (i, 0)),
        out_shape=jax.ShapeDtypeStruct((t, d), F32),
        compiler_params=_cparams("parallel"),
        name="retention_out",
    )(o_ctx, o_lat, proj, gn_g.reshape(1, vd), w, x, gate)


def _filter_rows(t, seq, band_ref, w1_ref, b1_ref, w2_ref, b2_ref, w3_ref, fr_ref, delta_ref):
    t_norm = t / max(seq - 1, 1)
    lane = lax.broadcasted_iota(jnp.int32, (1, 128), 1)
    ang = (2.0 * math.pi * t) * band_ref[...] / seq
    feat = jnp.where(lane == 0, t_norm,
                     jnp.where(lane <= D_BANDS, jnp.cos(ang), jnp.where(lane <= 2 * D_BANDS, -jnp.sin(ang), 0.0)))
    a = jnp.sin(fr_ref[0:1] * (_bdot(feat.astype(BF16), w1_ref[...].astype(BF16)) + b1_ref[...]))
    a = jnp.sin(fr_ref[1:2] * (_bdot(a.astype(BF16), w2_ref[...].astype(BF16)) + b2_ref[...]))
    f = _bdot(a.astype(BF16), w3_ref[...].astype(BF16))
    window = jnp.exp(-t_norm * delta_ref[...]) + D_MOD_SHIFT
    return f * jnp.concatenate([window] * 4, axis=1)


def _filter_sum_kernel(band_ref, w1_ref, b1_ref, w2_ref, b2_ref, w3_ref, fr_ref, delta_ref, o_ref, *, seq, tm):
    i = pl.program_id(0)
    t = (i * tm + lax.broadcasted_iota(jnp.int32, (tm, 1), 0)).astype(F32)
    f = _filter_rows(t, seq, band_ref, w1_ref, b1_ref, w2_ref, b2_ref, w3_ref, fr_ref, delta_ref)
    part = jnp.sum(jnp.abs(f), axis=0, keepdims=True)

    @pl.when(i == 0)
    def _():
        o_ref[...] = part

    @pl.when(i > 0)
    def _():
        o_ref[...] += part


def _filter_gen_kernel(band_ref, w1_ref, b1_ref, w2_ref, b2_ref, w3_ref, fr_ref, delta_ref, sum_ref, o_ref, *, seq, tm):
    i = pl.program_id(0)
    r = i * tm + lax.broadcasted_iota(jnp.int32, (tm, 1), 0)
    t = jnp.where(r < seq, r, 2 * seq - r).astype(F32)
    f = _filter_rows(t, seq, band_ref, w1_ref, b1_ref, w2_ref, b2_ref, w3_ref, fr_ref, delta_ref)
    f = f / (sum_ref[...] + 1e-6)
    half = 2 * D_MODEL
    fwd, bwd = f[:, :half], f[:, half:]
    o_ref[...] = jnp.where(r < seq, fwd, jnp.where(r > seq, bwd, 0.0)) + jnp.where(r == 0, bwd, 0.0)


def hyena_circular_kernel(seq, band, w1p, b1, w2, b2, w3, freq, delta, *, tm):
    consts = [band, w1p, b1, w2, b2, w3, freq, delta]
    cspecs = [pl.BlockSpec(a.shape, lambda i: (0, 0)) for a in consts]
    nf = w3.shape[1]
    sums = pl.pallas_call(
        functools.partial(_filter_sum_kernel, seq=seq, tm=tm),
        grid=(seq // tm,),
        in_specs=cspecs,
        out_specs=pl.BlockSpec((1, nf), lambda i: (0, 0)),
        out_shape=jax.ShapeDtypeStruct((1, nf), F32),
        compiler_params=_cparams("arbitrary"),
        name="hyena_filter_sum",
    )(*consts)
    return pl.pallas_call(
        functools.partial(_filter_gen_kernel, seq=seq, tm=tm),
        grid=(2 * seq // tm,),
        in_specs=cspecs + [pl.BlockSpec((1, nf), lambda i: (0, 0))],
        out_specs=pl.BlockSpec((tm, nf // 2), lambda i: (i, 0)),
        out_shape=jax.ShapeDtypeStruct((2 * seq, nf // 2), F32),
        compiler_params=_cparams("parallel"),
        name="hyena_filter_gen",
    )(*consts, sums)


def _dft_cs(rows, cols, n):
    m = np.outer(np.arange(rows), np.arange(cols)) % n
    ang = 2.0 * np.pi * m / n
    return np.cos(ang), np.sin(ang)


def _stack_fwd(c, s):
    return np.block([[c, s], [-s, c]])


def _stack_inv(c, s):
    return np.block([[c, -s], [s, c]])


def _dft_mats():
    as_bf16 = lambda a: jnp.asarray(a, F32).astype(BF16)
    c1, s1 = _dft_cs(LAT_N1, LAT_N1, LAT_N1)
    c2, s2 = _dft_cs(LAT_N2, LAT_N2, LAT_N2)
    cc, sc = _dft_cs(CTX_N, CTX_N, CTX_N)
    h1, hc = LAT_N1 // 2, CTX_N // 2
    return dict(
        lat_g1=as_bf16(_kron_rows(_stack_fwd(c1[:, :h1], s1[:, :h1]))),
        lat_g1r=as_bf16(_kron_rows(np.concatenate([c1, -s1], axis=0))),
        lat_g2=as_bf16(_stack_fwd(c2, s2)),
        lat_g2i=as_bf16(_stack_inv(c2, s2)),
        lat_g1i=as_bf16(_kron_rows(_stack_inv(c1[:h1], s1[:h1]) / LAT_N)),
        ctx_g=as_bf16(_stack_fwd(cc[:, :hc], sc[:, :hc])),
        ctx_gr=as_bf16(np.concatenate([cc, -sc], axis=0)),
        ctx_gi=as_bf16(_stack_inv(cc[:hc], sc[:hc]) / CTX_N),
    )


def _lmul_kernel(g_ref, x_ref, o_ref):
    o_ref[...] = _bdot(g_ref[...], x_ref[...].astype(BF16))


def left_matmul(g, x, *, row_blk=0, tc):
    m, k = g.shape
    n = x.shape[1]
    return pl.pallas_call(
        _lmul_kernel,
        grid=(n // tc,),
        in_specs=[pl.BlockSpec((m, k), lambda j: (0, 0)), pl.BlockSpec((k, tc), lambda j: (row_blk, j))],
        out_specs=pl.BlockSpec((m, tc), lambda j: (0, j)),
        out_shape=jax.ShapeDtypeStruct((m, n), F32),
        compiler_params=_cparams("parallel"),
        name="left_matmul",
    )(g, x)


SLAB_ROWS = 8


def _kron_rows(g):
    return np.kron(g, np.eye(SLAB_ROWS))


def _slab_dot(g_ref, x_ref):
    k, r, w = x_ref.shape
    return _bdot(g_ref[...], x_ref[...].reshape(k * r, w).astype(BF16))


def _lmul_slab_kernel(g_ref, x_ref, o_ref):
    o_ref[...] = _slab_dot(g_ref, x_ref).reshape(o_ref.shape)


def left_matmul_slabs(g8, x3, *, row_blk=0):
    m, k = g8.shape[0] // SLAB_ROWS, g8.shape[1] // SLAB_ROWS
    _, s, w = x3.shape
    return pl.pallas_call(
        _lmul_slab_kernel,
        grid=(s // SLAB_ROWS,),
        in_specs=[pl.BlockSpec(g8.shape, lambda j: (0, 0)),
                  pl.BlockSpec((k, SLAB_ROWS, w), lambda j: (row_blk, j, 0))],
        out_specs=pl.BlockSpec((m, SLAB_ROWS, w), lambda j: (0, j, 0)),
        out_shape=jax.ShapeDtypeStruct((m, s, w), F32),
        compiler_params=_cparams("parallel"),
        name="left_matmul_slabs",
    )(g8, x3)


def _gated_skip(conv, gate, y, skip_ref):
    return gate * (conv + skip_ref[...] * y)


def _lat_last_kernel(g_ref, b_ref, gate_ref, y_ref, skip_ref, *rest):
    o_ref = rest[-2] if len(rest) == 3 else rest[0]
    conv = _slab_dot(g_ref, b_ref).reshape(o_ref.shape)
    y_new = _gated_skip(conv, gate_ref[...], y_ref[...], skip_ref)
    o_ref[...] = y_new.astype(o_ref.dtype)
    if len(rest) == 3:
        gn_ref, _, a_ref = rest
        m, r, w = y_new.shape
        a_ref[...] = _bdot(gn_ref[...], y_new.reshape(m * r, w).astype(BF16)).reshape(a_ref.shape)


def lat_last_stage(g, b3, gate3, gate_blk, y3, y_blk, skip, g_next=None, *, out_dtype):
    m, k = g.shape[0] // SLAB_ROWS, g.shape[1] // SLAB_ROWS
    _, s, w = b3.shape
    in_specs = [
        pl.BlockSpec(g.shape, lambda j: (0, 0)),
        pl.BlockSpec((k, SLAB_ROWS, w), lambda j: (0, j, 0)),
        pl.BlockSpec((m, SLAB_ROWS, w), lambda j: (gate_blk, j, 0)),
        pl.BlockSpec((m, SLAB_ROWS, w), lambda j: (y_blk, j, 0)),
        pl.BlockSpec((1, w), lambda j: (0, 0)),
    ]
    args = [g, b3, gate3, y3, skip]
    out_specs = pl.BlockSpec((m, SLAB_ROWS, w), lambda j: (0, j, 0))
    out_shape = jax.ShapeDtypeStruct((m, s, w), out_dtype)
    if g_next is not None:
        mn = g_next.shape[0] // SLAB_ROWS
        in_specs.append(pl.BlockSpec(g_next.shape, lambda j: (0, 0)))
        args.append(g_next)
        out_specs = [out_specs, pl.BlockSpec((mn, SLAB_ROWS, w), lambda j: (0, j, 0))]
        out_shape = [out_shape, jax.ShapeDtypeStruct((mn, s, w), F32)]
    return pl.pallas_call(
        _lat_last_kernel,
        grid=(s // SLAB_ROWS,),
        in_specs=in_specs,
        out_specs=out_specs,
        out_shape=out_shape,
        compiler_params=_cparams("parallel"),
        name="hyena_lat_last",
    )(*args)


def _lat_mid_kernel(a_ref, *rest, conv):
    if conv:
        h_ref, g2_ref, g2i_ref, o_ref = rest
    else:
        g2_ref, o_ref = rest
    k1 = pl.program_id(0)
    n2 = lax.broadcasted_iota(jnp.int32, (LAT_N2, 1), 0)
    ang = (k1 * n2).astype(F32) * (2.0 * math.pi / LAT_N)
    c, s = jnp.cos(ang), jnp.sin(ang)
    ar, ai = a_ref[0], a_ref[1]
    t = jnp.concatenate([ar * c + ai * s, ai * c - ar * s], axis=0).astype(BF16)
    x = _bdot(g2_ref[...], t)
    xr, xi = x[:LAT_N2], x[LAT_N2:]
    if not conv:
        o_ref[0] = xr
        o_ref[1] = xi
        return
    hr, hi = h_ref[0], h_ref[1]
    y = jnp.concatenate([xr * hr - xi * hi, xr * hi + xi * hr], axis=0).astype(BF16)
    b = _bdot(g2i_ref[...], y)
    br, bi = b[:LAT_N2], b[LAT_N2:]
    o_ref[0] = br * c - bi * s
    o_ref[1] = bi * c + br * s


def lat_mid_stage(a, mats, h=None, order=0):
    w = a.shape[-1]
    blk = lambda width, col: pl.BlockSpec((2, None, LAT_N2, width), lambda k1, j: (0, k1, 0, col(j)))
    gspec = pl.BlockSpec((2 * LAT_N2, 2 * LAT_N2), lambda k1, j: (0, 0))
    if h is None:
        in_specs = [blk(D_MODEL, lambda j: j), gspec]
        args = [a, mats["lat_g2"]]
    else:
        in_specs = [blk(D_MODEL, lambda j: j), blk(D_MODEL, lambda j: order), gspec, gspec]
        args = [a, h, mats["lat_g2"], mats["lat_g2i"]]
    return pl.pallas_call(
        functools.partial(_lat_mid_kernel, conv=h is not None),
        grid=(LAT_N1, w // D_MODEL),
        in_specs=in_specs,
        out_specs=blk(D_MODEL, lambda j: j),
        out_shape=jax.ShapeDtypeStruct(a.shape, F32),
        compiler_params=_cparams("parallel", "arbitrary"),
        name="hyena_lat_mid",
    )(*args)


def _ctx_conv_kernel(y_ref, gate_ref, h_ref, g_ref, gi_ref, skip_ref, o_ref):
    n = CTX_N
    y = y_ref[...]
    z = _bdot(g_ref[...], y.astype(BF16))
    zr, zi = z[:n], z[n:]
    hr, hi = h_ref[0:n], h_ref[n:]
    w = jnp.concatenate([zr * hr - zi * hi, zr * hi + zi * hr], axis=0).astype(BF16)
    conv = _bdot(gi_ref[...], w)
    o_ref[...] = _gated_skip(conv, gate_ref[...], y, skip_ref).astype(o_ref.dtype)


def ctx_conv(y, y_plane, gate, gate_plane, h, order, mats, skip, *, out_dtype):
    rows = 2 * SEQ
    return pl.pallas_call(
        _ctx_conv_kernel,
        grid=(BATCH // 2,),
        in_specs=[
            pl.BlockSpec((None, rows, D_MODEL), lambda p: (y_plane, p, 0)),
            pl.BlockSpec((None, rows, D_MODEL), lambda p: (gate_plane, p, 0)),
            pl.BlockSpec((2 * CTX_N, D_MODEL), lambda p: (0, order)),
            pl.BlockSpec((2 * CTX_N, rows), lambda p: (0, 0)),
            pl.BlockSpec((rows, 2 * CTX_N), lambda p: (0, 0)),
            pl.BlockSpec((1, D_MODEL), lambda p: (0, 0)),
        ],
        out_specs=pl.BlockSpec((rows, D_MODEL), lambda p: (p, 0)),
        out_shape=jax.ShapeDtypeStruct((T_CTX, D_MODEL), out_dtype),
        compiler_params=_cparams("parallel"),
        name="hyena_ctx_conv",
    )(y, gate, h, mats["ctx_g"], mats["ctx_gi"], skip)


def hyena_core(z3, f_w1, f_b1, f_w2, f_b2, f_w3, f_freq, f_skip):
    mats = _dft_mats()
    bands = jnp.linspace(1e-4, D_BANDS - 1, D_BANDS, dtype=F32)
    band = jnp.zeros((1, 128), F32).at[0, 1:1 + D_BANDS].set(bands).at[0, 1 + D_BANDS:1 + 2 * D_BANDS].set(bands)
    w1p = jnp.zeros((128, D_FILTER_HIDDEN), F32).at[:D_EMB].set(f_w1)
    max_decay = math.log(D_DECAY_TARGET) / D_FAST_DECAY_PCT
    min_decay = math.log(D_DECAY_TARGET) / D_SLOW_DECAY_PCT
    delta = jnp.abs(jnp.linspace(min_decay, max_decay, D_MODEL, dtype=F32)).reshape(1, D_MODEL)
    fargs = (band, w1p, f_b1.reshape(1, -1), f_w2, f_b2.reshape(1, -1), f_w3, f_freq, delta)

    kc_ctx = hyena_circular_kernel(SEQ, *fargs, tm=SEQ)
    h_ctx = left_matmul(mats["ctx_gr"], kc_ctx, tc=D_MODEL)
    kc_lat = hyena_circular_kernel(DEC_SEQ, *fargs, tm=512)
    a = left_matmul_slabs(mats["lat_g1r"], kc_lat.reshape(LAT_N1, LAT_N2, 2 * D_MODEL))
    h_lat = lat_mid_stage(a.reshape(2, LAT_N1, LAT_N2, 2 * D_MODEL), mats)

    y1 = ctx_conv(z3, 2, z3, 0, h_ctx, 0, mats, f_skip[0:1], out_dtype=F32)
    y_ctx = ctx_conv(y1[None], 0, z3, 1, h_ctx, 1, mats, f_skip[1:2], out_dtype=BF16)

    slabs = DEC_BATCH * LAT_N1 // 2
    z_slabs = z3.reshape(3 * T_ALL // LAT_N2, LAT_N2, D_MODEL)
    plane_blks = T_ALL // LAT_N2 // slabs
    lat_blk = lambda plane: plane * plane_blks + T_CTX // LAT_N2 // slabs
    y, y_blk = z_slabs, lat_blk(2)
    a = left_matmul_slabs(mats["lat_g1"], y, row_blk=y_blk)
    for n in range(2):
        b = lat_mid_stage(a.reshape(2, LAT_N1, LAT_N2, D_MODEL), mats, h_lat, n)
        b = b.reshape(2 * LAT_N1, LAT_N2, D_MODEL)
        if n == 0:
            y, a = lat_last_stage(mats["lat_g1i"], b, z_slabs, lat_blk(n), y, y_blk, f_skip[n:n + 1], mats["lat_g1"],
                                  out_dtype=F32)
        else:
            y = lat_last_stage(mats["lat_g1i"], b, z_slabs, lat_blk(n), y, y_blk, f_skip[n:n + 1], out_dtype=F32)
        y_blk = 0
    return y_ctx, y.reshape(T_LAT, D_MODEL)


def kernel(x_prompt, x_sample, cache_attn_k, cache_attn_v, cache_diff_k, cache_diff_v, state_ret, c, c_ctx, w_mod, b_mod, norm1_g, norm2_g, final_g, attn_w_qkv, attn_q_g, attn_k_g, attn_w_o, diff_w_qkv, diff_lambda, diff_subln_g, diff_w_o, ret_w_in, ret_log_decay, ret_gn_g, ret_w_o, hyena_w_in, hyena_sc_w, hyena_sc_b, hyena_f_w1, hyena_f_b1, hyena_f_w2, hyena_f_b2, hyena_f_w3, hyena_f_freq, hyena_f_skip, hyena_w_o, ffn_w_up, ffn_conv_w, ffn_conv_b, ffn_w_down):
    d = D_MODEL
    xs = (x_prompt.reshape(T_CTX, d), x_sample.reshape(T_LAT, d))
    cond =jnp.zeros((SEG_ROWS, d), F32).at[0].set(c_ctx).at[1:N_SEG].set(c)
    mod = modulation(cond, w_mod, b_mod)
    mod = mod.reshape(DEPTH, SEG_ROWS, 6, d).transpose(0, 2, 1, 3).reshape(DEPTH, 6, SEG_ROWS, 1, d)
    rope = rope_tables()
    bf = lambda w: w.astype(BF16)
    w_up, w_down = bf(ffn_w_up), bf(ffn_w_down)
    out = {}

    for l in range(DEPTH):
        m, j = l % 4, l // 4
        sh1, sc1, g1, sh2, sc2, g2 = (mod[l, i] for i in range(6))
        if m == 0:
            nq, nk = A_HEADS * A_HEAD_DIM, A_KV_HEADS * A_HEAD_DIM
            scale = A_HEAD_DIM ** -0.5 * LOG2E
            qkv = norm_proj(*xs, norm1_g[l], sh1, sc1, bf(attn_w_qkv[j]))
            q_c, kv_c, kt_c, vt_c = qk_prep(qkv, row0=0, rows=T_CTX, nq=nq, nk=nk, q_g=attn_q_g[j], k_g=attn_k_g[j],
                                            scale=scale, cache="seq_minor")
            q_l, kv_l = qk_prep(qkv, row0=T_CTX, rows=T_LAT, nq=nq, nk=nk, q_g=attn_q_g[j], k_g=attn_k_g[j],
                                rope=rope, scale=scale)
            as_cache = lambda t: t.reshape(BATCH, A_KV_HEADS, A_HEAD_DIM, SEQ).transpose(0, 3, 1, 2)[:, None]
            out["attn_k"] = as_cache(kt_c)
            out["attn_v"] = as_cache(vt_c)
            cache = jnp.concatenate([cache_attn_k[:, j].reshape(DEC_BATCH, PAST_LEN, nk),
                                     cache_attn_v[:, j].reshape(DEC_BATCH, PAST_LEN, nk)], axis=-1).astype(BF16)
            lk = PAST_LEN + DEC_SEQ
            kv_all = jnp.concatenate([cache, kv_l.reshape(DEC_BATCH, DEC_SEQ, 2 * nk)], axis=1).reshape(DEC_BATCH * lk, 2 * nk)
            o_c = gqa_attention(q_c, kv_c, nb=BATCH, lq=SEQ, lk=SEQ, tq=SEQ)
            o_l = gqa_attention(q_l, kv_all, nb=DEC_BATCH, lq=DEC_SEQ, lk=lk, tq=256)
            x = out_proj_residual(o_c, o_l, bf(attn_w_o[j]), *xs, g1)
        elif m == 1:
            nq = nk = B_HEADS * 2 * B_HEAD_DIM
            scale = B_HEAD_DIM ** -0.5 * LOG2E
            lam_init = 0.8 - 0.6 * math.exp(-0.3 * l)
            qkv = norm_proj(*xs, norm1_g[l], sh1, sc1, bf(diff_w_qkv[j]))
            q_c, kv_c, k3_c, v3_c = qk_prep(qkv, row0=0, rows=T_CTX, nq=nq, nk=nk, scale=scale, cache="head_tile")
            q_l, kv_l = qk_prep(qkv, row0=T_CTX, rows=T_LAT, nq=nq, nk=nk, rope=rope, scale=scale)
            out["diff_k"] = k3_c.reshape(BATCH, 1, SEQ, B_HEADS, 2 * B_HEAD_DIM)
            out["diff_v"] = v3_c.reshape(BATCH, 1, SEQ, B_HEADS, 2 * B_HEAD_DIM)
            cache = jnp.concatenate([cache_diff_k[:, j].reshape(DEC_BATCH, PAST_LEN, nk),
                                     cache_diff_v[:, j].reshape(DEC_BATCH, PAST_LEN, nk)], axis=-1).astype(BF16)
            lk = PAST_LEN + DEC_SEQ
            kv_all = jnp.concatenate([cache, kv_l.reshape(DEC_BATCH, DEC_SEQ, 2 * nk)], axis=1).reshape(DEC_BATCH * lk, 2 * nk)
            dargs = (diff_lambda[j], diff_subln_g[j])
            o_c = diff_attention(q_c, kv_c, *dargs, nb=BATCH, lq=SEQ, lk=SEQ, tq=SEQ, lam_init=lam_init)
            o_l = diff_attention(q_l, kv_all, *dargs, nb=DEC_BATCH, lq=DEC_SEQ, lk=lk, tq=256, lam_init=lam_init)
            x = out_proj_residual(o_c, o_l, bf(diff_w_o[j]), *xs, g1)
        elif m == 2:
            proj = norm_proj(*xs, norm1_g[l], sh1, sc1, bf(ret_w_in[j]), out_dtype=BF16)
            o_c, st = retention(proj, ret_log_decay[j], None, row0=0, nb=BATCH, seq=SEQ)
            o_l, _ = retention(proj, ret_log_decay[j], state_ret[:, j], row0=T_CTX, nb=DEC_BATCH, seq=DEC_SEQ)
            out["ret_s"] = st.reshape(BATCH, 1, 2, C_HEADS, C_KEY_DIM, C_VAL_DIM)
            x = retention_out(o_c, o_l, proj, ret_gn_g[j], bf(ret_w_o[j]), x, g1)
        else:
            z3 = norm_proj_conv(x, norm1_g[l], sh1, sc1, bf(hyena_w_in[j]), hyena_sc_w[j], hyena_sc_b[j])
            y_c, y_l = hyena_core(z3, hyena_f_w1[j], hyena_f_b1[j], hyena_f_w2[j], hyena_f_b2[j], hyena_f_w3[j],
                                  hyena_f_freq[j], hyena_f_skip[j])
            x = out_proj_residual(y_c, y_l, bf(hyena_w_o[j]), *xs, g1)
        x = conv_ffn(x, norm2_g[l], sh2, sc2, g2, w_up, ffn_conv_w[l], ffn_conv_b[l], w_down, l,
                     final_g if l == DEPTH - 1 else None)
        xs = (x, x)

    y_prompt = x[0].reshape(BATCH, SEQ, d)
    y_sample = x[1].reshape(DEC_BATCH, DEC_SEQ, d)
    return (y_prompt, y_sample, out["attn_k"], out["attn_v"], out["diff_k"], out["diff_v"], out["ret_s"])
```
